```python
import math
import jax, jax.numpy as jnp
from jax import lax
import numpy as np

D_MODEL = 2048
BATCH = 4
SEQ = 4096
DEPTH = 1
DEC_BATCH = 32
DEC_SEQ = 1
PAST_LEN = 16384
PAGE_SIZE = 128

N_HEADS = 8
ATT_W = D_MODEL // 2
DV = ATT_W // N_HEADS
DK = DV // 2
QK_W = N_HEADS * 2 * DK
V_W = N_HEADS * DV
SCALE = DK ** -0.5
SUBLN_EPS = 1e-5
POOL_W = D_MODEL - ATT_W
POOL_WINDOWS = (2, 4, 8, 16)
N_POOL_GROUPS = len(POOL_WINDOWS)
POOL_GC = POOL_W // N_POOL_GROUPS
POOL_BUF = max(POOL_WINDOWS) - 1
IN_W = 2 * QK_W + V_W + POOL_W
MIX_W = V_W + POOL_W
D_FF = ((8 * D_MODEL // 3 + 127) // 128) * 128
NUM_BUCKETS = 32
MAX_EXACT = NUM_BUCKETS // 2
MAX_DISTANCE = 128
Q_BLOCK = 128
EPS = 1e-6
NEG = -1e30

kernel_name = "hymba_diffattn_pool_macaron_step"


def _rmsnorm(x, g, eps=EPS):
    xf = x.astype(jnp.float32)
    y = xf * lax.rsqrt(jnp.mean(xf * xf, axis=-1, keepdims=True) + eps)
    return (y * g.astype(jnp.float32)).astype(x.dtype)


def _swiglu(h, wg, wu, wd):
    return (jax.nn.silu(h @ wg) * (h @ wu)) @ wd


def _rel_bias(q_pos, k_pos, table):
    n = jnp.maximum(q_pos[:, None] - k_pos[None, :], 0)
    nf = jnp.maximum(n, 1).astype(jnp.float32)
    large = MAX_EXACT + (jnp.log(nf / MAX_EXACT) / math.log(MAX_DISTANCE / MAX_EXACT)
                         * (NUM_BUCKETS - MAX_EXACT)).astype(jnp.int32)
    large = jnp.minimum(large, NUM_BUCKETS - 1)
    bucket = jnp.where(n < MAX_EXACT, n, large)
    return jnp.transpose(table[bucket].astype(jnp.float32), (2, 0, 1))


def _diff_logits(q5, k5, q_pos, k_pos, table):
    bias = _rel_bias(q_pos, k_pos, table)
    s = jnp.einsum('bqhmd,bkhmd->mbhqk', q5, k5).astype(jnp.float32) * SCALE + bias[None, None]
    mask = k_pos[None, :] <= q_pos[:, None]
    return jnp.where(mask[None, None, None], s, NEG)


def _diff_attn_prompt(q, k, v, lam, table):
    B, S, H, _ = q.shape
    pos = jnp.arange(S, dtype=jnp.int32)
    k5 = k.reshape(B, S, H, 2, DK)

    def block(args):
        qb, qpos = args
        s = _diff_logits(qb.reshape(B, -1, H, 2, DK), k5, qpos, pos, table)
        p = jax.nn.softmax(s, axis=-1)
        w = (p[0] - lam * p[1]).astype(v.dtype)
        return jnp.einsum('bhqk,bkhd->bqhd', w, v)

    if S > Q_BLOCK and S % Q_BLOCK == 0:
        nb = S // Q_BLOCK
        qb = q.reshape(B, nb, Q_BLOCK, H, 2 * DK).transpose(1, 0, 2, 3, 4)
        out = lax.map(block, (qb, pos.reshape(nb, Q_BLOCK)))
        return out.transpose(1, 0, 2, 3, 4).reshape(B, S, H, DV)
    return block((q, pos))


def _diff_attn_sample(q, k_new, v_new, k_past, v_past, lam, table):
    DB, SD, H, _ = q.shape
    n_past = k_past.shape[1]
    q_pos = n_past + jnp.arange(SD, dtype=jnp.int32)
    past_pos = jnp.arange(n_past, dtype=jnp.int32)
    q5 = q.reshape(DB, SD, H, 2, DK)
    s = jnp.concatenate([
        _diff_logits(q5, k_past.reshape(DB, n_past, H, 2, DK), q_pos, past_pos, table),
        _diff_logits(q5, k_new.reshape(DB, SD, H, 2, DK), q_pos, q_pos, table)], axis=-1)
    p = jax.nn.softmax(s, axis=-1)
    w = (p[0] - lam * p[1]).astype(v_new.dtype)
    return (jnp.einsum('bhqk,bkhd->bqhd', w[..., :n_past], v_past)
            + jnp.einsum('bhqk,bkhd->bqhd', w[..., n_past:], v_new))


def _diff_finish(o, lam_init, g):
    B, S, H, D = o.shape
    o = _rmsnorm(o, g, SUBLN_EPS) * (1.0 - lam_init)
    return o.reshape(B, S, H * D)


def _pool_mix(xp, prev, pos0, w, scale):
    B, S, C = xp.shape
    full = jnp.concatenate([prev, xp], axis=1)
    cs = jnp.cumsum(full.astype(jnp.float32), axis=1)
    cs = jnp.concatenate([jnp.zeros((B, 1, C), jnp.float32), cs], axis=1)
    pos = pos0 + jnp.arange(S, dtype=jnp.int32)
    end = cs[:, POOL_BUF + 1:]
    xf = xp.astype(jnp.float32)
    diffs = []
    for g, win in enumerate(POOL_WINDOWS):
        sl = slice(g * POOL_GC, (g + 1) * POOL_GC)
        start = cs[:, POOL_BUF + 1 - win:POOL_BUF + 1 - win + S, sl]
        cnt = jnp.minimum(pos + 1, win).astype(jnp.float32)[None, :, None]
        diffs.append((end[..., sl] - start) / cnt - xf[..., sl])
    d = jnp.stack(diffs, axis=2)
    y = jnp.einsum('bsgc,gcd->bsgd', d, w.astype(jnp.float32)).reshape(B, S, C)
    y = y * scale.astype(jnp.float32)
    return y.astype(xp.dtype), full[:, -POOL_BUF:]


def _split_proj(h, w_in):
    p = h @ w_in
    B, S, _ = p.shape
    q = p[..., :QK_W].reshape(B, S, N_HEADS, 2 * DK)
    k = p[..., QK_W:2 * QK_W].reshape(B, S, N_HEADS, 2 * DK)
    v = p[..., 2 * QK_W:2 * QK_W + V_W].reshape(B, S, N_HEADS, DV)
    xp = p[..., 2 * QK_W + V_W:]
    return q, k, v, xp


def setup_inputs(seed: int = 0) -> dict:
    key = jax.random.key(seed)
    ks = jax.random.split(key, 32)
    f32 = jnp.float32
    n_pages = PAST_LEN // PAGE_SIZE
    n_phys = (DEC_BATCH * n_pages * 5) // 4

    def nrm(k, shape, scale):
        return jax.random.normal(k, shape, f32) * scale

    def gain(k, shape):
        return 1.0 + 0.05 * jax.random.normal(k, shape, f32)

    page_table = jax.random.permutation(ks[5], n_phys)[:DEC_BATCH * n_pages]
    page_table = page_table.reshape(DEC_BATCH, n_pages).astype(jnp.int32)
    return {
        "x_prompt": nrm(ks[0], (BATCH, SEQ, D_MODEL), 1.0),
        "x_sample": nrm(ks[1], (DEC_BATCH, DEC_SEQ, D_MODEL), 1.0),
        "cache_k": nrm(ks[2], (DEPTH, n_phys, PAGE_SIZE, N_HEADS, 2 * DK), 1.0),
        "cache_v": nrm(ks[3], (DEPTH, n_phys, PAGE_SIZE, N_HEADS, DV), 1.0),
        "state_pool": nrm(ks[4], (DEPTH, DEC_BATCH, POOL_BUF, POOL_W), 1.0),
        "page_table": page_table,
        "w_in": nrm(ks[6], (DEPTH, D_MODEL, IN_W), D_MODEL ** -0.5),
        "w_out": nrm(ks[7], (DEPTH, MIX_W, D_MODEL), MIX_W ** -0.5),
        "ffn1_norm": gain(ks[8], (DEPTH, D_MODEL)),
        "ffn1_gate": nrm(ks[9], (DEPTH, D_MODEL, D_FF), D_MODEL ** -0.5),
        "ffn1_up": nrm(ks[10], (DEPTH, D_MODEL, D_FF), D_MODEL ** -0.5),
        "ffn1_down": nrm(ks[11], (DEPTH, D_FF, D_MODEL), D_FF ** -0.5),
        "mix_norm": gain(ks[12], (DEPTH, D_MODEL)),
        "ffn2_norm": gain(ks[13], (DEPTH, D_MODEL)),
        "ffn2_gate": nrm(ks[14], (DEPTH, D_MODEL, D_FF), D_MODEL ** -0.5),
        "ffn2_up": nrm(ks[15], (DEPTH, D_MODEL, D_FF), D_MODEL ** -0.5),
        "ffn2_down": nrm(ks[16], (DEPTH, D_FF, D_MODEL), D_FF ** -0.5),
        "lambda_q1": nrm(ks[17], (DEPTH, DK), 0.1),
        "lambda_k1": nrm(ks[18], (DEPTH, DK), 0.1),
        "lambda_q2": nrm(ks[19], (DEPTH, DK), 0.1),
        "lambda_k2": nrm(ks[20], (DEPTH, DK), 0.1),
        "subln_g": gain(ks[21], (DEPTH, DV)),
        "pool_w": nrm(ks[22], (DEPTH, N_POOL_GROUPS, POOL_GC, POOL_GC), POOL_GC ** -0.5),
        "pool_scale": gain(ks[23], (DEPTH, POOL_W)),
        "rel_bias": nrm(ks[24], (NUM_BUCKETS, N_HEADS), 0.5),
        "final_norm": gain(ks[25], (D_MODEL,)),
    }


def reference(x_prompt, x_sample, cache_k, cache_v, state_pool, page_table,
              w_in, w_out, ffn1_norm, ffn1_gate, ffn1_up, ffn1_down, mix_norm,
              ffn2_norm, ffn2_gate, ffn2_up, ffn2_down,
              lambda_q1, lambda_k1, lambda_q2, lambda_k2, subln_g,
              pool_w, pool_scale, rel_bias, final_norm):
    f32 = jnp.float32
    B = x_prompt.shape[0]
    DB = x_sample.shape[0]
    n_past = page_table.shape[1] * PAGE_SIZE
    yp, ys = x_prompt, x_sample
    kp_l, vp_l, pp_l, ks_l, vs_l, ps_l = [], [], [], [], [], []
    for l in range(DEPTH):
        lam_init = 0.8 - 0.6 * math.exp(-0.3 * l)
        lam = (jnp.exp(jnp.sum(lambda_q1[l].astype(f32) * lambda_k1[l].astype(f32)))
               - jnp.exp(jnp.sum(lambda_q2[l].astype(f32) * lambda_k2[l].astype(f32)))
               + lam_init)
        yp = yp + 0.5 * _swiglu(_rmsnorm(yp, ffn1_norm[l]), ffn1_gate[l], ffn1_up[l], ffn1_down[l])
        ys = ys + 0.5 * _swiglu(_rmsnorm(ys, ffn1_norm[l]), ffn1_gate[l], ffn1_up[l], ffn1_down[l])

        q, k, v, xpp = _split_proj(_rmsnorm(yp, mix_norm[l]), w_in[l])
        att = _diff_finish(_diff_attn_prompt(q, k, v, lam, rel_bias), lam_init, subln_g[l])
        pool_prev = jnp.zeros((B, POOL_BUF, POOL_W), xpp.dtype)
        pool_out, pool_new = _pool_mix(xpp, pool_prev, 0, pool_w[l], pool_scale[l])
        yp = yp + jnp.concatenate([att, pool_out], axis=-1) @ w_out[l]
        kp_l.append(k); vp_l.append(v); pp_l.append(pool_new)

        qs, ks, vs, xps = _split_proj(_rmsnorm(ys, mix_norm[l]), w_in[l])
        k_past = cache_k[l, page_table].reshape(DB, n_past, N_HEADS, 2 * DK)
        v_past = cache_v[l, page_table].reshape(DB, n_past, N_HEADS, DV)
        att_s = _diff_finish(_diff_attn_sample(qs, ks, vs, k_past, v_past, lam, rel_bias),
                             lam_init, subln_g[l])
        pool_out_s, pool_new_s = _pool_mix(xps, state_pool[l].astype(xps.dtype), n_past,
                                           pool_w[l], pool_scale[l])
        ys = ys + jnp.concatenate([att_s, pool_out_s], axis=-1) @ w_out[l]
        ks_l.append(ks); vs_l.append(vs); ps_l.append(pool_new_s)

        yp = yp + 0.5 * _swiglu(_rmsnorm(yp, ffn2_norm[l]), ffn2_gate[l], ffn2_up[l], ffn2_down[l])
        ys = ys + 0.5 * _swiglu(_rmsnorm(ys, ffn2_norm[l]), ffn2_gate[l], ffn2_up[l], ffn2_down[l])

    y_prompt = _rmsnorm(yp, final_norm)
    y_sample = _rmsnorm(ys, final_norm)
    k_prompt = jnp.stack(kp_l, axis=0)
    v_prompt = jnp.stack(vp_l, axis=0)
    pool_prompt = jnp.stack(pp_l, axis=0)
    k_sample = jnp.stack(ks_l, axis=0)
    v_sample = jnp.stack(vs_l, axis=0)
    pool_sample = jnp.stack(ps_l, axis=0)
    return (y_prompt, y_sample, k_prompt, v_prompt, pool_prompt, k_sample, v_sample, pool_sample)
```

```python
import functools
import math

import jax
import jax.numpy as jnp
from jax import lax
from jax.experimental import pallas as pl
from jax.experimental.pallas import tpu as pltpu

F32 = jnp.float32
BF16 = jnp.bfloat16

V7X_LANES = 128
V7X_VMEM_BYTES = 64 * 1024 * 1024
V7X_VMEM_CAP = V7X_VMEM_BYTES - 8 * 1024 * 1024

EPS = 1e-6
SUBLN_EPS = 1e-5
NEG = -1e30
POOL_WINDOWS = (2, 4, 8, 16)
POOL_HALO = 16
NUM_BUCKETS = 32
MAX_EXACT = NUM_BUCKETS // 2
MAX_DISTANCE = 128
PAGE_SIZE = 128


def _vmem_limit(*nbytes):
    est = int(sum(nbytes))
    return min(max(est, 16 * 1024 * 1024), V7X_VMEM_CAP)


def _rms(x, g, eps):
    return x * lax.rsqrt(jnp.mean(x * x, axis=-1, keepdims=True) + eps) * g


def _bucket(n):
    n = jnp.maximum(n, 0)
    nf = jnp.maximum(n, 1).astype(F32)
    large = MAX_EXACT + (jnp.log(nf / MAX_EXACT) / math.log(MAX_DISTANCE / MAX_EXACT)
                         * (NUM_BUCKETS - MAX_EXACT)).astype(jnp.int32)
    large = jnp.minimum(large, NUM_BUCKETS - 1)
    return jnp.where(n < MAX_EXACT, n, large)


def _ffn_kernel(x_ref, g_ref, wg_ref, wu_ref, wd_ref, fg_ref, o_ref, h_scr, *, n_f, final):
    f = pl.program_id(1)

    @pl.when(f == 0)
    def _():
        h_scr[...] = _rms(x_ref[...], g_ref[...], EPS).astype(BF16)

    h = h_scr[...]
    gate = jnp.dot(h, wg_ref[...], preferred_element_type=F32)
    up = jnp.dot(h, wu_ref[...], preferred_element_type=F32)
    act = (gate / (1.0 + jnp.exp(-gate)) * up).astype(BF16)
    part = jnp.dot(act, wd_ref[...], preferred_element_type=F32)

    @pl.when(f == 0)
    def _():
        o_ref[...] = part

    @pl.when(f > 0)
    def _():
        o_ref[...] += part

    @pl.when(f == n_f - 1)
    def _():
        y = x_ref[...] + 0.5 * o_ref[...]
        if final:
            y = _rms(y, fg_ref[...], EPS)
        o_ref[...] = y


def _ffn(x, norm_g, wg, wu, wd, final_g, *, tm, tf, final, name):
    m, d = x.shape
    f_dim = wg.shape[1]
    n_f = f_dim // tf
    vmem = _vmem_limit(2 * 2 * tm * d * 4,
                       2 * 3 * d * tf * 2,
                       tm * d * 2,
                       4 * tm * tf * 4 + tm * d * 4)
    return pl.pallas_call(
        functools.partial(_ffn_kernel, n_f=n_f, final=final),
        grid=(m // tm, n_f),
        in_specs=[
            pl.BlockSpec((tm, d), lambda i, f: (i, 0)),
            pl.BlockSpec((1, d), lambda i, f: (0, 0)),
            pl.BlockSpec((d, tf), lambda i, f: (0, f)),
            pl.BlockSpec((d, tf), lambda i, f: (0, f)),
            pl.BlockSpec((tf, d), lambda i, f: (f, 0)),
            pl.BlockSpec((1, d), lambda i, f: (0, 0)),
        ],
        out_specs=pl.BlockSpec((tm, d), lambda i, f: (i, 0)),
        out_shape=jax.ShapeDtypeStruct((m, d), F32),
        scratch_shapes=[pltpu.VMEM((tm, d), BF16)],
        compiler_params=pltpu.CompilerParams(
            dimension_semantics=("arbitrary", "arbitrary"), vmem_limit_bytes=vmem),
        name=name,
    )(x, norm_g, wg, wu, wd, final_g)


def _proj_kernel(x_ref, g_ref, w_ref, q_ref, k_ref, v_ref, p_ref, h_scr, *, scale):
    j = pl.program_id(1)

    @pl.when(j == 0)
    def _():
        h_scr[...] = _rms(x_ref[...], g_ref[...], EPS).astype(BF16)

    p = jnp.dot(h_scr[...], w_ref[...], preferred_element_type=F32)

    @pl.when(j == 0)
    def _():
        q_ref[...] = (p * scale).astype(BF16)

    @pl.when(j == 1)
    def _():
        k_ref[...] = p

    @pl.when(j == 2)
    def _():
        v_ref[...] = p

    @pl.when(j == 3)
    def _():
        p_ref[...] = p


def _proj(x, norm_g, w_in, *, tm, scale, name):
    m, d = x.shape
    w = w_in.shape[1] // 4
    out_spec = pl.BlockSpec((tm, w), lambda i, j: (i, 0))
    vmem = _vmem_limit(2 * tm * d * 4, 2 * d * w * 2, 2 * tm * w * (2 + 4 + 4 + 4),
                       tm * d * 2, 2 * tm * w * 4)
    return pl.pallas_call(
        functools.partial(_proj_kernel, scale=scale),
        grid=(m // tm, 4),
        in_specs=[
            pl.BlockSpec((tm, d), lambda i, j: (i, 0)),
            pl.BlockSpec((1, d), lambda i, j: (0, 0)),
            pl.BlockSpec((d, w), lambda i, j: (0, j)),
        ],
        out_specs=[out_spec, out_spec, out_spec, out_spec],
        out_shape=[jax.ShapeDtypeStruct((m, w), BF16)] + [jax.ShapeDtypeStruct((m, w), F32)] * 3,
        scratch_shapes=[pltpu.VMEM((tm, d), BF16)],
        compiler_params=pltpu.CompilerParams(
            dimension_semantics=("arbitrary", "arbitrary"), vmem_limit_bytes=vmem),
        name=name,
    )(x, norm_g, w_in)


def _bias_tile_kernel(tab_ref, o_ref, *, t):
    h = pl.program_id(0)
    i = lax.broadcasted_iota(jnp.int32, (t, t), 0)
    j = lax.broadcasted_iota(jnp.int32, (t, t), 1)
    far = tab_ref[h * NUM_BUCKETS + NUM_BUCKETS - 1]
    for which, n in ((0, i - j), (1, t + i - j)):
        bucket = _bucket(n)
        val = jnp.zeros((t, t), F32)
        for b in range(NUM_BUCKETS - 1):
            val = jnp.where(bucket == b, tab_ref[h * NUM_BUCKETS + b] - far, val)
        if which == 0:
            val = jnp.where(j <= i, val, NEG)
        o_ref[which, 0:t, :] = val
        o_ref[which, t:2 * t, :] = val


def _bias_tiles(rel_bias, *, t):
    n_heads = rel_bias.shape[1]
    tab = rel_bias.T.reshape(-1)
    return pl.pallas_call(
        functools.partial(_bias_tile_kernel, t=t),
        grid=(n_heads,),
        in_specs=[pl.BlockSpec(memory_space=pltpu.SMEM)],
        out_specs=pl.BlockSpec((None, 2, 2 * t, t), lambda h: (h, 0, 0, 0)),
        out_shape=jax.ShapeDtypeStruct((n_heads, 2, 2 * t, t), F32),
        compiler_params=pltpu.CompilerParams(
            dimension_semantics=("arbitrary",),
            vmem_limit_bytes=_vmem_limit(2 * 2 * 2 * t * t * 4, 8 * t * t * 4)),
        name="rel_bias_tiles",
    )(tab)


def _lambda(lq1_ref, lk1_ref, lq2_ref, lk2_ref, lam_init):
    a = jnp.sum(lq1_ref[...] * lk1_ref[...], axis=-1, keepdims=True)
    b = jnp.sum(lq2_ref[...] * lk2_ref[...], axis=-1, keepdims=True)
    return jnp.exp(a) - jnp.exp(b) + lam_init


def _attn_kernel(lq1_ref, lk1_ref, lq2_ref, lk2_ref, g_ref, q_ref, k_ref, v_ref, bias_ref,
                 o_ref, kb_scr, vb_scr, m_scr, l_scr, acc_scr, *, t, dk, lam_init):
    qi = pl.program_id(2)

    @pl.when(qi == 0)
    def _():
        kb_scr[...] = k_ref[...].astype(BF16)
        vb_scr[...] = v_ref[...].astype(BF16)

    q = q_ref[...].astype(F32)
    lane = lax.broadcasted_iota(jnp.int32, q.shape, 1)
    qq = jnp.concatenate([jnp.where(lane < dk, q, 0.0), jnp.where(lane >= dk, q, 0.0)],
                         axis=0).astype(BF16)

    m_scr[...] = jnp.full(m_scr.shape, NEG, F32)
    l_scr[...] = jnp.zeros(l_scr.shape, F32)
    acc_scr[...] = jnp.zeros(acc_scr.shape, F32)

    def step(kj, bias):
        start = pl.multiple_of(kj * t, t)
        kblk = kb_scr[pl.ds(start, t), :]
        vblk = vb_scr[pl.ds(start, t), :]
        s = lax.dot_general(qq, kblk, (((1,), (1,)), ((), ())), preferred_element_type=F32)
        if bias is not None:
            s = s + bias
        m_prev = m_scr[...]
        m_new = jnp.maximum(m_prev, jnp.max(s, axis=-1, keepdims=True))
        p = jnp.exp(s - m_new)
        alpha = jnp.exp(m_prev - m_new)
        l_scr[...] = alpha * l_scr[...] + jnp.sum(p, axis=-1, keepdims=True)
        acc_scr[...] = alpha * acc_scr[...] + jnp.dot(p.astype(BF16), vblk,
                                                      preferred_element_type=F32)
        m_scr[...] = m_new

    def far_body(kj, carry):
        step(kj, None)
        return carry

    lax.fori_loop(0, qi - 1, far_body, 0)

    @pl.when(qi >= 1)
    def _():
        step(qi - 1, bias_ref[1])

    step(qi, bias_ref[0])

    o = acc_scr[...] / l_scr[...]
    lam = _lambda(lq1_ref, lk1_ref, lq2_ref, lk2_ref, lam_init)
    w = o[0:t] - lam * o[t:2 * t]
    o_ref[...] = (_rms(w, g_ref[...], SUBLN_EPS) * (1.0 - lam_init)).astype(o_ref.dtype)


def _attn_prompt(q, k, v, bias, lams, subln_g, *, batch, seq, n_heads, t, lam_init):
    dv = q.shape[1] // n_heads
    nq = seq // t
    lam_spec = pl.BlockSpec((1, dv // 2), lambda h, b, i: (0, 0))
    kv_spec = pl.BlockSpec((seq, dv), lambda h, b, i: (b, h))
    vmem = _vmem_limit(2 * 2 * seq * dv * 4, 2 * seq * dv * 2, 2 * 2 * 2 * t * t * 4,
                       4 * t * dv * 2, 2 * t * (dv + 2 * V7X_LANES) * 4,
                       3 * 2 * t * t * 4)
    return pl.pallas_call(
        functools.partial(_attn_kernel, t=t, dk=dv // 2, lam_init=lam_init),
        grid=(n_heads, batch, nq),
        in_specs=[lam_spec, lam_spec, lam_spec, lam_spec,
                  pl.BlockSpec((1, dv), lambda h, b, i: (0, 0)),
                  pl.BlockSpec((t, dv), lambda h, b, i: (b * nq + i, h)),
                  kv_spec, kv_spec,
                  pl.BlockSpec((None, 2, 2 * t, t), lambda h, b, i: (h, 0, 0, 0))],
        out_specs=pl.BlockSpec((t, dv), lambda h, b, i: (b * nq + i, h)),
        out_shape=jax.ShapeDtypeStruct(q.shape, BF16),
        scratch_shapes=[pltpu.VMEM((seq, dv), BF16), pltpu.VMEM((seq, dv), BF16),
                        pltpu.VMEM((2 * t, 1), F32), pltpu.VMEM((2 * t, 1), F32),
                        pltpu.VMEM((2 * t, dv), F32)],
        compiler_params=pltpu.CompilerParams(
            dimension_semantics=("arbitrary", "arbitrary", "arbitrary"), vmem_limit_bytes=vmem),
        name="attn_prompt",
    )(*lams, subln_g, q, k, v, bias)


def _decode_kernel(pt_ref, lq1_ref, lk1_ref, lq2_ref, lk2_ref, g_ref, tab_ref, q_ref, kn_ref,
                   vn_ref, *rest, n_pages_step, n_heads, dv, lam_init):
    del pt_ref
    p_n = n_pages_step
    k_refs = rest[:p_n]
    v_refs = rest[p_n:2 * p_n]
    o_ref, qm_scr, bias_scr, m_scr, l_scr, acc_scr = rest[2 * p_n:]
    c = pl.program_id(1)
    n_c = pl.num_programs(1)
    rows = 2 * n_heads
    page_cols = PAGE_SIZE * n_heads

    @pl.when(c == 0)
    def _():
        q = q_ref[...].astype(F32)
        lane = lax.broadcasted_iota(jnp.int32, q.shape, 1)
        qm_scr[...] = jnp.concatenate(
            [jnp.where(lane < dv // 2, q, 0.0), jnp.where(lane >= dv // 2, q, 0.0)],
            axis=0).astype(BF16)
        key_row = lax.broadcasted_iota(jnp.int32, (1, page_cols), 1) // n_heads
        bucket = _bucket(PAGE_SIZE - key_row)
        val = jnp.zeros((rows, page_cols), F32)
        for b in range(NUM_BUCKETS - 1):
            val = jnp.where(bucket == b, tab_ref[:, b:b + 1], val)
        bias_scr[...] = val
        m_scr[...] = jnp.full(m_scr.shape, NEG, F32)
        l_scr[...] = jnp.zeros(l_scr.shape, F32)
        acc_scr[...] = jnp.zeros(acc_scr.shape, F32)

    qm = qm_scr[...]
    row = lax.broadcasted_iota(jnp.int32, (rows, page_cols), 0)
    col = lax.broadcasted_iota(jnp.int32, (rows, page_cols), 1)
    own = (col % n_heads) == (row % n_heads)
    near = jnp.where(c == n_c - 1, bias_scr[...], 0.0)
    pieces = []
    for j in range(p_n):
        kf = k_refs[j][...].reshape(page_cols, dv).astype(BF16)
        sj = lax.dot_general(qm, kf, (((1,), (1,)), ((), ())), preferred_element_type=F32)
        if j == p_n - 1:
            sj = sj + near
        pieces.append(jnp.where(own, sj, NEG))
    s = jnp.concatenate(pieces, axis=1)

    def update(s, pv_fn):
        m_prev = m_scr[...]
        m_new = jnp.maximum(m_prev, jnp.max(s, axis=-1, keepdims=True))
        p = jnp.exp(s - m_new)
        alpha = jnp.exp(m_prev - m_new)
        l_scr[...] = alpha * l_scr[...] + jnp.sum(p, axis=-1, keepdims=True)
        acc_scr[...] = alpha * acc_scr[...] + pv_fn(p)
        m_scr[...] = m_new

    def pv_pages(p):
        pb = p.astype(BF16)
        out = None
        for j in range(p_n):
            vf = v_refs[j][...].reshape(page_cols, dv).astype(BF16)
            term = jnp.dot(pb[:, j * page_cols:(j + 1) * page_cols], vf,
                           preferred_element_type=F32)
            out = term if out is None else out + term
        return out

    update(s, pv_pages)

    @pl.when(c == n_c - 1)
    def _():
        kn = jnp.concatenate([kn_ref[...]] * 2, axis=0)
        vn = jnp.concatenate([vn_ref[...]] * 2, axis=0)
        s_self = jnp.sum(qm.astype(F32) * kn, axis=-1, keepdims=True) + tab_ref[:, 0:1]
        update(s_self, lambda p: p * vn)
        o = acc_scr[...] / l_scr[...]
        lam = _lambda(lq1_ref, lk1_ref, lq2_ref, lk2_ref, lam_init)
        w = o[0:n_heads] - lam * o[n_heads:rows]
        o_ref[...] = _rms(w, g_ref[...], SUBLN_EPS) * (1.0 - lam_init)


def _attn_decode(q, k_new, v_new, cache_k, cache_v, page_table, tab, lams, subln_g, *,
                 n_pages_step, lam_init):
    db, n_heads, dv = q.shape
    n_pages = page_table.shape[1]
    p_n = n_pages_step
    rows = 2 * n_heads
    page_cols = PAGE_SIZE * n_heads
    assert n_pages % p_n == 0 and PAGE_SIZE >= MAX_DISTANCE
    small = lambda shape: pl.BlockSpec(shape, lambda b, c, pt: (0,) * len(shape))
    row_spec = pl.BlockSpec((None, n_heads, dv), lambda b, c, pt: (b, 0, 0))

    def page_spec(j):
        return pl.BlockSpec((None, PAGE_SIZE, n_heads, dv),
                            lambda b, c, pt: (pt[b, c * p_n + j], 0, 0, 0))

    vmem = _vmem_limit(2 * 2 * p_n * page_cols * dv * 4,
                       2 * p_n * page_cols * dv * 2,
                       6 * rows * p_n * page_cols * 4)
    grid_spec = pltpu.PrefetchScalarGridSpec(
        num_scalar_prefetch=1,
        grid=(db, n_pages // p_n),
        in_specs=[small((1, dv // 2))] * 4 + [small((1, dv)), small((rows, NUM_BUCKETS)),
                                              row_spec, row_spec, row_spec]
        + [page_spec(j) for j in range(p_n)] * 2,
        out_specs=pl.BlockSpec((None, n_heads, dv), lambda b, c, pt: (b, 0, 0)),
        scratch_shapes=[pltpu.VMEM((rows, dv), BF16), pltpu.VMEM((rows, page_cols), F32),
                        pltpu.VMEM((rows, 1), F32), pltpu.VMEM((rows, 1), F32),
                        pltpu.VMEM((rows, dv), F32)],
    )
    return pl.pallas_call(
        functools.partial(_decode_kernel, n_pages_step=p_n, n_heads=n_heads, dv=dv,
                          lam_init=lam_init),
        grid_spec=grid_spec,
        out_shape=jax.ShapeDtypeStruct((db, n_heads, dv), F32),
        compiler_params=pltpu.CompilerParams(
            dimension_semantics=("arbitrary", "arbitrary"), vmem_limit_bytes=vmem),
        name="attn_decode",
    )(page_table, *lams, subln_g, tab, q, k_new, v_new,
      *([cache_k] * p_n), *([cache_v] * p_n))


def _pool_project(diffs, pw_ref, ps_ref):
    outs = [jnp.dot(d.astype(BF16), pw_ref[g], preferred_element_type=F32)
            for g, d in enumerate(diffs)]
    return (jnp.concatenate(outs, axis=1) * ps_ref[...]).astype(BF16)


def _mix_kernel(att_ref, xp_ref, halo_ref, pw_ref, ps_ref, wo_ref, y_ref, o_ref, full_scr,
                *, tm, tiles_per_seq):
    t_in_seq = lax.rem(pl.program_id(0), tiles_per_seq)
    att_w = att_ref.shape[1]
    gc = xp_ref.shape[1] // len(POOL_WINDOWS)
    full_scr[0:POOL_HALO, :] = jnp.where(t_in_seq == 0, 0.0, halo_ref[...])
    full_scr[POOL_HALO:, :] = xp_ref[...]
    pos = t_in_seq * tm + lax.broadcasted_iota(jnp.int32, (tm, 1), 0)
    diffs = []
    for g, win in enumerate(POOL_WINDOWS):
        cols = slice(g * gc, (g + 1) * gc)
        x = xp_ref[:, cols]
        acc = x
        for d in range(1, win):
            acc = acc + full_scr[POOL_HALO - d:POOL_HALO - d + tm, cols]
        cnt = jnp.minimum(pos + 1, win).astype(F32)
        diffs.append(acc / cnt - x)
    pool = _pool_project(diffs, pw_ref, ps_ref)
    o_ref[...] = (y_ref[...]
                  + jnp.dot(att_ref[...], wo_ref[0:att_w, :], preferred_element_type=F32)
                  + jnp.dot(pool, wo_ref[att_w:, :], preferred_element_type=F32))


def _mix_prompt(att, xp, pool_w, pool_scale, w_out, y, *, seq, tm):
    m, d = y.shape
    aw, pw = att.shape[1], xp.shape[1]
    n_g, gc = pool_w.shape[0], pool_w.shape[1]
    tiles_per_seq = seq // tm
    halo_blocks = tm // POOL_HALO
    vmem = _vmem_limit(2 * tm * (aw * 2 + pw * 4 + 2 * d * 4), 2 * POOL_HALO * pw * 4,
                       2 * (aw + pw) * d * 2, 2 * n_g * gc * gc * 2,
                       (tm + POOL_HALO) * pw * 4, 4 * tm * pw * 4)
    return pl.pallas_call(
        functools.partial(_mix_kernel, tm=tm, tiles_per_seq=tiles_per_seq),
        grid=(m // tm,),
        in_specs=[
            pl.BlockSpec((tm, aw), lambda i: (i, 0)),
            pl.BlockSpec((tm, pw), lambda i: (i, 0)),
            pl.BlockSpec((POOL_HALO, pw), lambda i: (jnp.maximum(i * halo_blocks - 1, 0), 0)),
            pl.BlockSpec((n_g, gc, gc), lambda i: (0, 0, 0)),
            pl.BlockSpec((1, pw), lambda i: (0, 0)),
            pl.BlockSpec((aw + pw, d), lambda i: (0, 0)),
            pl.BlockSpec((tm, d), lambda i: (i, 0)),
        ],
        out_specs=pl.BlockSpec((tm, d), lambda i: (i, 0)),
        out_shape=jax.ShapeDtypeStruct((m, d), F32),
        scratch_shapes=[pltpu.VMEM((tm + POOL_HALO, pw), F32)],
        compiler_params=pltpu.CompilerParams(
            dimension_semantics=("arbitrary",), vmem_limit_bytes=vmem),
        name="mix_prompt",
    )(att, xp, xp, pool_w, pool_scale, w_out, y)


def _mix_sample_kernel(att_ref, xp_ref, st_ref, pw_ref, ps_ref, wo_ref, y_ref, o_ref, *, pos0):
    att_w = att_ref.shape[1]
    gc = xp_ref.shape[1] // len(POOL_WINDOWS)
    row = lax.broadcasted_iota(jnp.int32, (1, POOL_HALO, 1), 1)
    diffs = []
    for g, win in enumerate(POOL_WINDOWS):
        cols = slice(g * gc, (g + 1) * gc)
        x = xp_ref[:, cols]
        prev = jnp.where(row >= POOL_HALO - (win - 1), st_ref[:, :, cols], 0.0)
        acc = x + jnp.sum(prev, axis=1)
        diffs.append(acc / float(min(pos0 + 1, win)) - x)
    pool = _pool_project(diffs, pw_ref, ps_ref)
    o_ref[...] = (y_ref[...]
                  + jnp.dot(att_ref[...].astype(BF16), wo_ref[0:att_w, :],
                            preferred_element_type=F32)
                  + jnp.dot(pool, wo_ref[att_w:, :], preferred_element_type=F32))


def _mix_sample(att, xp, state16, pool_w, pool_scale, w_out, y, *, pos0):
    whole = lambda a: pl.BlockSpec(a.shape, lambda: (0,) * a.ndim)
    args = (att, xp, state16, pool_w, pool_scale, w_out, y)
    return pl.pallas_call(
        functools.partial(_mix_sample_kernel, pos0=pos0),
        in_specs=[whole(a) for a in args],
        out_specs=whole(y),
        out_shape=jax.ShapeDtypeStruct(y.shape, F32),
        compiler_params=pltpu.CompilerParams(
            vmem_limit_bytes=_vmem_limit(*(2 * a.size * a.dtype.itemsize for a in args),
                                         4 * state16.size * 4)),
        name="mix_sample",
    )(*args)


def _pad_cols(w, mult):
    pad = (-w.shape[-1]) % mult
    return jnp.pad(w, ((0, 0), (0, pad))) if pad else w


def _pad_rows(w, mult):
    pad = (-w.shape[0]) % mult
    return jnp.pad(w, ((0, pad), (0, 0))) if pad else w


def kernel(x_prompt, x_sample, cache_k, cache_v, state_pool, page_table, w_in, w_out, ffn1_norm,
           ffn1_gate, ffn1_up, ffn1_down, mix_norm, ffn2_norm, ffn2_gate, ffn2_up, ffn2_down,
           lambda_q1, lambda_k1, lambda_q2, lambda_k2, subln_g, pool_w, pool_scale, rel_bias,
           final_norm):
    batch, seq, d = x_prompt.shape
    db, sd, _ = x_sample.shape
    depth, n_phys, page, n_heads, dv = cache_v.shape
    dk = dv // 2
    width = n_heads * dv
    pool_width = state_pool.shape[-1]
    assert sd == 1 and page == PAGE_SIZE and cache_k.shape[-1] == 2 * dk
    assert w_in.shape[-1] == 4 * width and pool_width == width
    assert state_pool.shape[2] == POOL_HALO - 1 and rel_bias.shape[0] == NUM_BUCKETS
    n_past = page_table.shape[1] * PAGE_SIZE
    scale = dk ** -0.5

    tm_p = min(512, batch * seq)
    tf = 512
    t_att = min(512, seq)
    assert seq % tm_p == 0 and seq % t_att == 0 and t_att >= MAX_DISTANCE

    row2 = lambda a: a.reshape(1, -1)
    yp = x_prompt.reshape(batch * seq, d)
    ys = x_sample.reshape(db * sd, d)
    bias = _bias_tiles(rel_bias, t=t_att)
    far = rel_bias[NUM_BUCKETS - 1]
    tab_dec = jnp.tile((rel_bias - far[None, :]).T, (2, 1))
    outs = {k: [] for k in ("kp", "vp", "pp", "ks", "vs", "ps")}

    for l in range(depth):
        lam_init = 0.8 - 0.6 * math.exp(-0.3 * l)
        lams = tuple(row2(a[l]) for a in (lambda_q1, lambda_k1, lambda_q2, lambda_k2))
        ffn_w = []
        for gate, up, down in ((ffn1_gate, ffn1_up, ffn1_down), (ffn2_gate, ffn2_up, ffn2_down)):
            ffn_w.append((_pad_cols(gate[l].astype(BF16), tf), _pad_cols(up[l].astype(BF16), tf),
                          _pad_rows(down[l].astype(BF16), tf)))
        w_in_b = w_in[l].astype(BF16)
        w_out_b = w_out[l].astype(BF16)
        pool_w_b = pool_w[l].astype(BF16)
        pool_s = row2(pool_scale[l])
        last = l == depth - 1
        fin = row2(final_norm)

        yp = _ffn(yp, row2(ffn1_norm[l]), *ffn_w[0], fin, tm=tm_p, tf=tf, final=False,
                  name="ffn1_prompt")
        q, k, v, xp = _proj(yp, row2(mix_norm[l]), w_in_b, tm=tm_p, scale=scale,
                            name="proj_prompt")
        att = _attn_prompt(q, k, v, bias, lams, row2(subln_g[l]), batch=batch, seq=seq,
                           n_heads=n_heads, t=t_att, lam_init=lam_init)
        yp = _mix_prompt(att, xp, pool_w_b, pool_s, w_out_b, yp, seq=seq, tm=tm_p)
        yp = _ffn(yp, row2(ffn2_norm[l]), *ffn_w[1], fin, tm=tm_p, tf=tf, final=last,
                  name="ffn2_prompt")
        outs["kp"].append(k.reshape(batch, seq, n_heads, dv))
        outs["vp"].append(v.reshape(batch, seq, n_heads, dv))
        outs["pp"].append(xp.reshape(batch, seq, pool_width)[:, seq - (POOL_HALO - 1):])

        ys = _ffn(ys, row2(ffn1_norm[l]), *ffn_w[0], fin, tm=db, tf=tf, final=False,
                  name="ffn1_sample")
        qs, ks, vs, xps = _proj(ys, row2(mix_norm[l]), w_in_b, tm=db, scale=scale,
                                name="proj_sample")
        att_s = _attn_decode(qs.reshape(db, n_heads, dv), ks.reshape(db, n_heads, dv),
                             vs.reshape(db, n_heads, dv), cache_k[l], cache_v[l], page_table,
                             tab_dec, lams, row2(subln_g[l]), n_pages_step=8, lam_init=lam_init)
        state16 = jnp.pad(state_pool[l], ((0, 0), (1, 0), (0, 0)))
        ys = _mix_sample(att_s.reshape(db, width), xps, state16, pool_w_b, pool_s, w_out_b, ys,
                         pos0=n_past)
        ys = _ffn(ys, row2(ffn2_norm[l]), *ffn_w[1], fin, tm=db, tf=tf, final=last,
                  name="ffn2_sample")
        outs["ks"].append(ks.reshape(db, sd, n_heads, dv))
        outs["vs"].append(vs.reshape(db, sd, n_heads, dv))
        outs["ps"].append(jnp.concatenate([state_pool[l][:, 1:], xps[:, None, :]], axis=1))

    st = lambda name: jnp.stack(outs[name], axis=0)
    return (yp.reshape(batch, seq, d), ys.reshape(db, sd, d), st("kp"), st("vp"), st("pp"),
            st("ks"), st("vs"), st("ps"))
```

```python
import functools
import math

import jax
import jax.numpy as jnp
from jax import lax
from jax.experimental import pallas as pl
from jax.experimental.pallas import tpu as pltpu

F32 = jnp.float32
BF16 = jnp.bfloat16

V7X_LANES = 128
V7X_VMEM_BYTES = 64 * 1024 * 1024
V7X_VMEM_CAP = V7X_VMEM_BYTES - 8 * 1024 * 1024

EPS = 1e-6
SUBLN_EPS = 1e-5
NEG = -1e30
POOL_WINDOWS = (2, 4, 8, 16)
POOL_HALO = 16
NUM_BUCKETS = 32
MAX_EXACT = NUM_BUCKETS // 2
MAX_DISTANCE = 128
LOG2E = math.log2(math.e)
PAGE_SIZE = 128


def _vmem_limit(*nbytes):
    est = int(sum(nbytes))
    return min(max(est, 16 * 1024 * 1024), V7X_VMEM_CAP)


def _rms(x, g, eps):
    return x * lax.rsqrt(jnp.mean(x * x, axis=-1, keepdims=True) + eps) * g


def _bucket(n):
    n = jnp.maximum(n, 0)
    nf = jnp.maximum(n, 1).astype(F32)
    large = MAX_EXACT + (jnp.log(nf / MAX_EXACT) / math.log(MAX_DISTANCE / MAX_EXACT)
                         * (NUM_BUCKETS - MAX_EXACT)).astype(jnp.int32)
    large = jnp.minimum(large, NUM_BUCKETS - 1)
    return jnp.where(n < MAX_EXACT, n, large)


def _ffn_kernel(x_ref, g_ref, wg_ref, wu_ref, wd_ref, fg_ref, o_ref, h_scr, *, n_f, final):
    f = pl.program_id(1)

    @pl.when(f == 0)
    def _():
        h_scr[...] = _rms(x_ref[...], g_ref[...], EPS).astype(BF16)

    h = h_scr[...]
    gate = jnp.dot(h, wg_ref[...], preferred_element_type=F32)
    up = jnp.dot(h, wu_ref[...], preferred_element_type=F32)
    act = (gate / (1.0 + jnp.exp(-gate)) * up).astype(BF16)
    part = jnp.dot(act, wd_ref[...], preferred_element_type=F32)

    @pl.when(f == 0)
    def _():
        o_ref[...] = part

    @pl.when(f > 0)
    def _():
        o_ref[...] += part

    @pl.when(f == n_f - 1)
    def _():
        y = x_ref[...] + 0.5 * o_ref[...]
        if final:
            y = _rms(y, fg_ref[...], EPS)
        o_ref[...] = y


def _ffn(x, norm_g, wg, wu, wd, final_g, *, tm, tf, final, name):
    m, d = x.shape
    f_dim = wg.shape[1]
    n_f = f_dim // tf
    vmem = _vmem_limit(2 * 2 * tm * d * 4,
                       2 * 3 * d * tf * 2,
                       tm * d * 2,
                       4 * tm * tf * 4 + tm * d * 4)
    return pl.pallas_call(
        functools.partial(_ffn_kernel, n_f=n_f, final=final),
        grid=(m // tm, n_f),
        in_specs=[
            pl.BlockSpec((tm, d), lambda i, f: (i, 0)),
            pl.BlockSpec((1, d), lambda i, f: (0, 0)),
            pl.BlockSpec((d, tf), lambda i, f: (0, f)),
            pl.BlockSpec((d, tf), lambda i, f: (0, f)),
            pl.BlockSpec((tf, d), lambda i, f: (f, 0)),
            pl.BlockSpec((1, d), lambda i, f: (0, 0)),
        ],
        out_specs=pl.BlockSpec((tm, d), lambda i, f: (i, 0)),
        out_shape=jax.ShapeDtypeStruct((m, d), F32),
        scratch_shapes=[pltpu.VMEM((tm, d), BF16)],
        compiler_params=pltpu.CompilerParams(
            dimension_semantics=("arbitrary", "arbitrary"), vmem_limit_bytes=vmem),
        name=name,
    )(x, norm_g, wg, wu, wd, final_g)


def _proj_kernel(x_ref, g_ref, w_ref, q_ref, k_ref, v_ref, p_ref, h_scr, *, scale):
    j = pl.program_id(1)

    @pl.when(j == 0)
    def _():
        h_scr[...] = _rms(x_ref[...], g_ref[...], EPS).astype(BF16)

    p = jnp.dot(h_scr[...], w_ref[...], preferred_element_type=F32)

    @pl.when(j == 0)
    def _():
        q_ref[...] = (p * scale).astype(BF16)

    @pl.when(j == 1)
    def _():
        k_ref[...] = p

    @pl.when(j == 2)
    def _():
        v_ref[...] = p

    @pl.when(j == 3)
    def _():
        p_ref[...] = p


def _proj(x, norm_g, w_in, *, tm, scale, name):
    m, d = x.shape
    w = w_in.shape[1] // 4
    out_spec = pl.BlockSpec((tm, w), lambda i, j: (i, 0))
    vmem = _vmem_limit(2 * tm * d * 4, 2 * d * w * 2, 2 * tm * w * (2 + 4 + 4 + 4),
                       tm * d * 2, 2 * tm * w * 4)
    return pl.pallas_call(
        functools.partial(_proj_kernel, scale=scale),
        grid=(m // tm, 4),
        in_specs=[
            pl.BlockSpec((tm, d), lambda i, j: (i, 0)),
            pl.BlockSpec((1, d), lambda i, j: (0, 0)),
            pl.BlockSpec((d, w), lambda i, j: (0, j)),
        ],
        out_specs=[out_spec, out_spec, out_spec, out_spec],
        out_shape=[jax.ShapeDtypeStruct((m, w), BF16)] + [jax.ShapeDtypeStruct((m, w), F32)] * 3,
        scratch_shapes=[pltpu.VMEM((tm, d), BF16)],
        compiler_params=pltpu.CompilerParams(
            dimension_semantics=("arbitrary", "arbitrary"), vmem_limit_bytes=vmem),
        name=name,
    )(x, norm_g, w_in)


def _bias_tile_kernel(tab_ref, o_ref, *, tq):
    h = pl.program_id(0)
    r = lax.broadcasted_iota(jnp.int32, (2 * tq, 2 * tq), 0)
    c = lax.broadcasted_iota(jnp.int32, (2 * tq, 2 * tq), 1)
    key = r - tq
    qry = c % tq
    far = tab_ref[h * NUM_BUCKETS + NUM_BUCKETS - 1]
    bucket = _bucket(qry - key)
    val = jnp.zeros(r.shape, F32)
    for b in range(NUM_BUCKETS - 1):
        val = jnp.where(bucket == b, (tab_ref[h * NUM_BUCKETS + b] - far) * LOG2E, val)
    o_ref[...] = jnp.where(key <= qry, val, NEG)


def _bias_tiles(rel_bias, *, tq):
    n_heads = rel_bias.shape[1]
    tab = rel_bias.T.reshape(-1)
    return pl.pallas_call(
        functools.partial(_bias_tile_kernel, tq=tq),
        grid=(n_heads,),
        in_specs=[pl.BlockSpec(memory_space=pltpu.SMEM)],
        out_specs=pl.BlockSpec((None, 2 * tq, 2 * tq), lambda h: (h, 0, 0)),
        out_shape=jax.ShapeDtypeStruct((n_heads, 2 * tq, 2 * tq), F32),
        compiler_params=pltpu.CompilerParams(
            dimension_semantics=("arbitrary",),
            vmem_limit_bytes=_vmem_limit(16 * 2 * tq * 2 * tq * 4)),
        name="rel_bias_tiles",
    )(tab)


def _lambda(lq1_ref, lk1_ref, lq2_ref, lk2_ref, lam_init):
    a = jnp.sum(lq1_ref[...] * lk1_ref[...], axis=-1, keepdims=True)
    b = jnp.sum(lq2_ref[...] * lk2_ref[...], axis=-1, keepdims=True)
    return jnp.exp(a) - jnp.exp(b) + lam_init


def _attn_kernel(lq1_ref, lk1_ref, lq2_ref, lk2_ref, g_ref, q_ref, k_ref, v_ref, bias_ref,
                 o_ref, kb_scr, vt_scr, m_scr, l_scr, acc_scr, *, t, tq, dk, lam_init):
    qi = pl.program_id(2)
    n_strips = t // tq
    seq = k_ref.shape[0]
    sw = 2 * tq

    @pl.when(qi == 0)
    def _():
        kb_scr[0:tq, :] = jnp.zeros((tq, kb_scr.shape[1]), BF16)
        vt_scr[0] = jnp.zeros(vt_scr.shape[1:], BF16)
        kb_scr[tq:, :] = k_ref[...].astype(BF16)
        for c in range(seq // tq):
            vt_scr[c + 1] = v_ref[c * tq:(c + 1) * tq, :].T.astype(BF16)

    qa = q_ref[...].astype(F32)
    lane = lax.broadcasted_iota(jnp.int32, qa.shape, 1)
    q1 = jnp.where(lane < dk, qa, 0.0).astype(BF16)
    q2 = jnp.where(lane >= dk, qa, 0.0).astype(BF16)
    strips = [jnp.concatenate([q1[u * tq:(u + 1) * tq], q2[u * tq:(u + 1) * tq]], axis=0)
              for u in range(n_strips)]
    qs_all = jnp.concatenate(strips, axis=0)

    m_scr[...] = jnp.full(m_scr.shape, NEG, F32)
    l_scr[...] = jnp.zeros(l_scr.shape, F32)
    acc_scr[...] = jnp.zeros(acc_scr.shape, F32)

    def keys(row0, n):
        return kb_scr[pl.ds(pl.multiple_of(row0, tq), n), :]

    def values_t(chunk0, n_chunks):
        return jnp.concatenate([vt_scr[chunk0 + i] for i in range(n_chunks)], axis=1)

    def fold(lanes, pieces):
        m_prev = m_scr[:, lanes]
        m_new = m_prev
        for s, _ in pieces:
            m_new = jnp.maximum(m_new, jnp.max(s, axis=0, keepdims=True))
        alpha = jnp.exp2(m_prev - m_new)
        l_new = alpha * l_scr[:, lanes]
        pv = None
        for s, vt in pieces:
            p = jnp.exp2(s - m_new)
            l_new = l_new + jnp.sum(p, axis=0, keepdims=True)
            term = jnp.dot(vt, p.astype(BF16), preferred_element_type=F32)
            pv = term if pv is None else pv + term
        l_scr[:, lanes] = l_new
        acc_scr[:, lanes] = alpha * acc_scr[:, lanes] + pv
        m_scr[:, lanes] = m_new

    def scores(kc, qs):
        return lax.dot_general(kc, qs, (((1,), (1,)), ((), ())), preferred_element_type=F32)

    every = slice(0, n_strips * sw)
    far_chunks = t // tq

    def far_body(j, carry):
        fold(every, [(scores(keys(tq + j * t, t), qs_all),
                      values_t(1 + j * far_chunks, far_chunks))])
        return carry

    lax.fori_loop(0, qi - 1, far_body, 0)

    @pl.when(qi > 0)
    def _():
        fold(every, [(scores(keys(tq + (qi - 1) * t, t - tq), qs_all),
                      values_t(1 + (qi - 1) * far_chunks, far_chunks - 1))])

    n_absent = jnp.where(qi == 0, tq, 0)
    for u in range(n_strips):
        lanes = slice(u * sw, (u + 1) * sw)
        qs = strips[u]
        pieces = []
        if u > 0:
            s_a = scores(keys(qi * t, u * tq), qs)
            row = lax.broadcasted_iota(jnp.int32, s_a.shape, 0)
            s_a = jnp.where(row < n_absent, NEG, s_a)
            pieces.append((s_a, values_t(qi * far_chunks, u)))
        bias = bias_ref[...]
        if u == 0:
            row = lax.broadcasted_iota(jnp.int32, bias.shape, 0)
            bias = jnp.where(row < n_absent, NEG, bias)
        s_b = scores(keys(qi * t + u * tq, 2 * tq), qs) + bias
        pieces.append((s_b, values_t(qi * far_chunks + u, 2)))
        fold(lanes, pieces)

    o = acc_scr[...] / l_scr[...]
    lam = _lambda(lq1_ref, lk1_ref, lq2_ref, lk2_ref, lam_init)
    for u in range(n_strips):
        w = o[:, u * sw:u * sw + tq] - lam * o[:, u * sw + tq:(u + 1) * sw]
        y = (w * lax.rsqrt(jnp.mean(w * w, axis=0, keepdims=True) + SUBLN_EPS)
             * g_ref[...] * (1.0 - lam_init))
        o_ref[u * tq:(u + 1) * tq, :] = y.T.astype(o_ref.dtype)


def _attn_prompt(q, k, v, bias, lams, subln_g, *, batch, seq, n_heads, t, tq, lam_init):
    dv = q.shape[1] // n_heads
    nq = seq // t
    assert t % tq == 0 and tq >= MAX_DISTANCE
    g_cols = jnp.broadcast_to(subln_g.reshape(dv, 1), (dv, tq))
    lam_spec = pl.BlockSpec((1, dv // 2), lambda h, b, i: (0, 0))
    kv_spec = pl.BlockSpec((seq, dv), lambda h, b, i: (b, h))
    vmem = _vmem_limit(2 * 2 * seq * dv * 4, 2 * (seq + tq) * dv * 2, 2 * 2 * tq * 2 * tq * 4,
                       4 * t * dv * 2, dv * 2 * t * 4, 4 * t * 2 * t * 4)
    return pl.pallas_call(
        functools.partial(_attn_kernel, t=t, tq=tq, dk=dv // 2, lam_init=lam_init),
        grid=(n_heads, batch, nq),
        in_specs=[lam_spec, lam_spec, lam_spec, lam_spec,
                  pl.BlockSpec((dv, tq), lambda h, b, i: (0, 0)),
                  pl.BlockSpec((t, dv), lambda h, b, i: (b * nq + i, h)),
                  kv_spec, kv_spec,
                  pl.BlockSpec((None, 2 * tq, 2 * tq), lambda h, b, i: (h, 0, 0))],
        out_specs=pl.BlockSpec((t, dv), lambda h, b, i: (b * nq + i, h)),
        out_shape=jax.ShapeDtypeStruct(q.shape, BF16),
        scratch_shapes=[pltpu.VMEM((seq + tq, dv), BF16),
                        pltpu.VMEM((seq // tq + 1, dv, tq), BF16),
                        pltpu.VMEM((1, 2 * t), F32), pltpu.VMEM((1, 2 * t), F32),
                        pltpu.VMEM((dv, 2 * t), F32)],
        compiler_params=pltpu.CompilerParams(
            dimension_semantics=("arbitrary", "arbitrary", "arbitrary"), vmem_limit_bytes=vmem),
        name="attn_prompt",
    )(*lams, g_cols, q, k, v, bias)


def _decode_kernel(pt_ref, lq1_ref, lk1_ref, lq2_ref, lk2_ref, g_ref, tab_ref, q_ref, kn_ref,
                   vn_ref, *rest, n_pages_step, n_heads, dv, lam_init):
    del pt_ref
    p_n = n_pages_step
    k_refs = rest[:p_n]
    v_refs = rest[p_n:2 * p_n]
    o_ref, qm_scr, bias_scr, m_scr, l_scr, acc_scr = rest[2 * p_n:]
    c = pl.program_id(1)
    n_c = pl.num_programs(1)
    rows = 2 * n_heads
    page_cols = PAGE_SIZE * n_heads

    @pl.when(c == 0)
    def _():
        q = q_ref[...].astype(F32)
        lane = lax.broadcasted_iota(jnp.int32, q.shape, 1)
        qm_scr[...] = jnp.concatenate(
            [jnp.where(lane < dv // 2, q, 0.0), jnp.where(lane >= dv // 2, q, 0.0)],
            axis=0).astype(BF16)
        key_row = lax.broadcasted_iota(jnp.int32, (1, page_cols), 1) // n_heads
        bucket = _bucket(PAGE_SIZE - key_row)
        val = jnp.zeros((rows, page_cols), F32)
        for b in range(NUM_BUCKETS - 1):
            val = jnp.where(bucket == b, tab_ref[:, b:b + 1], val)
        bias_scr[...] = val
        m_scr[...] = jnp.full(m_scr.shape, NEG, F32)
        l_scr[...] = jnp.zeros(l_scr.shape, F32)
        acc_scr[...] = jnp.zeros(acc_scr.shape, F32)

    qm = qm_scr[...]
    row = lax.broadcasted_iota(jnp.int32, (rows, page_cols), 0)
    col = lax.broadcasted_iota(jnp.int32, (rows, page_cols), 1)
    own = (col % n_heads) == (row % n_heads)
    near = jnp.where(c == n_c - 1, bias_scr[...], 0.0)
    pieces = []
    for j in range(p_n):
        kf = k_refs[j][...].reshape(page_cols, dv).astype(BF16)
        sj = lax.dot_general(qm, kf, (((1,), (1,)), ((), ())), preferred_element_type=F32)
        if j == p_n - 1:
            sj = sj + near
        pieces.append(jnp.where(own, sj, NEG))
    s = jnp.concatenate(pieces, axis=1)

    def update(s, pv_fn):
        m_prev = m_scr[...]
        m_new = jnp.maximum(m_prev, jnp.max(s, axis=-1, keepdims=True))
        p = jnp.exp2(s - m_new)
        alpha = jnp.exp2(m_prev - m_new)
        l_scr[...] = alpha * l_scr[...] + jnp.sum(p, axis=-1, keepdims=True)
        acc_scr[...] = alpha * acc_scr[...] + pv_fn(p)
        m_scr[...] = m_new

    def pv_pages(p):
        pb = p.astype(BF16)
        out = None
        for j in range(p_n):
            vf = v_refs[j][...].reshape(page_cols, dv).astype(BF16)
            term = jnp.dot(pb[:, j * page_cols:(j + 1) * page_cols], vf,
                           preferred_element_type=F32)
            out = term if out is None else out + term
        return out

    update(s, pv_pages)

    @pl.when(c == n_c - 1)
    def _():
        kn = jnp.concatenate([kn_ref[...]] * 2, axis=0)
        vn = jnp.concatenate([vn_ref[...]] * 2, axis=0)
        s_self = jnp.sum(qm.astype(F32) * kn, axis=-1, keepdims=True) + tab_ref[:, 0:1]
        update(s_self, lambda p: p * vn)
        o = acc_scr[...] / l_scr[...]
        lam = _lambda(lq1_ref, lk1_ref, lq2_ref, lk2_ref, lam_init)
        w = o[0:n_heads] - lam * o[n_heads:rows]
        o_ref[...] = _rms(w, g_ref[...], SUBLN_EPS) * (1.0 - lam_init)


def _attn_decode(q, k_new, v_new, cache_k, cache_v, page_table, tab, lams, subln_g, *,
                 n_pages_step, lam_init):
    db, n_heads, dv = q.shape
    n_pages = page_table.shape[1]
    p_n = n_pages_step
    rows = 2 * n_heads
    page_cols = PAGE_SIZE * n_heads
    assert n_pages % p_n == 0 and PAGE_SIZE >= MAX_DISTANCE
    small = lambda shape: pl.BlockSpec(shape, lambda b, c, pt: (0,) * len(shape))
    row_spec = pl.BlockSpec((None, n_heads, dv), lambda b, c, pt: (b, 0, 0))

    def page_spec(j):
        return pl.BlockSpec((None, PAGE_SIZE, n_heads, dv),
                            lambda b, c, pt: (pt[b, c * p_n + j], 0, 0, 0))

    vmem = _vmem_limit(2 * 2 * p_n * page_cols * dv * 4,
                       2 * p_n * page_cols * dv * 2,
                       6 * rows * p_n * page_cols * 4)
    grid_spec = pltpu.PrefetchScalarGridSpec(
        num_scalar_prefetch=1,
        grid=(db, n_pages // p_n),
        in_specs=[small((1, dv // 2))] * 4 + [small((1, dv)), small((rows, NUM_BUCKETS)),
                                              row_spec, row_spec, row_spec]
        + [page_spec(j) for j in range(p_n)] * 2,
        out_specs=pl.BlockSpec((None, n_heads, dv), lambda b, c, pt: (b, 0, 0)),
        scratch_shapes=[pltpu.VMEM((rows, dv), BF16), pltpu.VMEM((rows, page_cols), F32),
                        pltpu.VMEM((rows, 1), F32), pltpu.VMEM((rows, 1), F32),
                        pltpu.VMEM((rows, dv), F32)],
    )
    return pl.pallas_call(
        functools.partial(_decode_kernel, n_pages_step=p_n, n_heads=n_heads, dv=dv,
                          lam_init=lam_init),
        grid_spec=grid_spec,
        out_shape=jax.ShapeDtypeStruct((db, n_heads, dv), F32),
        compiler_params=pltpu.CompilerParams(
            dimension_semantics=("arbitrary", "arbitrary"), vmem_limit_bytes=vmem),
        name="attn_decode",
    )(page_table, *lams, subln_g, tab, q, k_new, v_new,
      *([cache_k] * p_n), *([cache_v] * p_n))


def _pool_project(diffs, pw_ref, ps_ref):
    outs = [jnp.dot(d.astype(BF16), pw_ref[g], preferred_element_type=F32)
            for g, d in enumerate(diffs)]
    return (jnp.concatenate(outs, axis=1) * ps_ref[...]).astype(BF16)


def _mix_kernel(att_ref, xp_ref, halo_ref, pw_ref, ps_ref, wo_ref, y_ref, o_ref, full_scr,
                *, tm, tiles_per_seq):
    t_in_seq = lax.rem(pl.program_id(0), tiles_per_seq)
    att_w = att_ref.shape[1]
    gc = xp_ref.shape[1] // len(POOL_WINDOWS)
    full_scr[0:POOL_HALO, :] = jnp.where(t_in_seq == 0, 0.0, halo_ref[...])
    full_scr[POOL_HALO:, :] = xp_ref[...]
    pos = t_in_seq * tm + lax.broadcasted_iota(jnp.int32, (tm, 1), 0)
    diffs = []
    for g, win in enumerate(POOL_WINDOWS):
        cols = slice(g * gc, (g + 1) * gc)
        x = xp_ref[:, cols]
        acc = x
        for d in range(1, win):
            acc = acc + full_scr[POOL_HALO - d:POOL_HALO - d + tm, cols]
        cnt = jnp.minimum(pos + 1, win).astype(F32)
        diffs.append(acc / cnt - x)
    pool = _pool_project(diffs, pw_ref, ps_ref)
    o_ref[...] = (y_ref[...]
                  + jnp.dot(att_ref[...], wo_ref[0:att_w, :], preferred_element_type=F32)
                  + jnp.dot(pool, wo_ref[att_w:, :], preferred_element_type=F32))


def _mix_prompt(att, xp, pool_w, pool_scale, w_out, y, *, seq, tm):
    m, d = y.shape
    aw, pw = att.shape[1], xp.shape[1]
    n_g, gc = pool_w.shape[0], pool_w.shape[1]
    tiles_per_seq = seq // tm
    halo_blocks = tm // POOL_HALO
    vmem = _vmem_limit(2 * tm * (aw * 2 + pw * 4 + 2 * d * 4), 2 * POOL_HALO * pw * 4,
                       2 * (aw + pw) * d * 2, 2 * n_g * gc * gc * 2,
                       (tm + POOL_HALO) * pw * 4, 4 * tm * pw * 4)
    return pl.pallas_call(
        functools.partial(_mix_kernel, tm=tm, tiles_per_seq=tiles_per_seq),
        grid=(m // tm,),
        in_specs=[
            pl.BlockSpec((tm, aw), lambda i: (i, 0)),
            pl.BlockSpec((tm, pw), lambda i: (i, 0)),
            pl.BlockSpec((POOL_HALO, pw), lambda i: (jnp.maximum(i * halo_blocks - 1, 0), 0)),
            pl.BlockSpec((n_g, gc, gc), lambda i: (0, 0, 0)),
            pl.BlockSpec((1, pw), lambda i: (0, 0)),
            pl.BlockSpec((aw + pw, d), lambda i: (0, 0)),
            pl.BlockSpec((tm, d), lambda i: (i, 0)),
        ],
        out_specs=pl.BlockSpec((tm, d), lambda i: (i, 0)),
        out_shape=jax.ShapeDtypeStruct((m, d), F32),
        scratch_shapes=[pltpu.VMEM((tm + POOL_HALO, pw), F32)],
        compiler_params=pltpu.CompilerParams(
            dimension_semantics=("arbitrary",), vmem_limit_bytes=vmem),
        name="mix_prompt",
    )(att, xp, xp, pool_w, pool_scale, w_out, y)


def _mix_sample_kernel(att_ref, xp_ref, st_ref, pw_ref, ps_ref, wo_ref, y_ref, o_ref, *, pos0):
    att_w = att_ref.shape[1]
    gc = xp_ref.shape[1] // len(POOL_WINDOWS)
    row = lax.broadcasted_iota(jnp.int32, (1, POOL_HALO, 1), 1)
    diffs = []
    for g, win in enumerate(POOL_WINDOWS):
        cols = slice(g * gc, (g + 1) * gc)
        x = xp_ref[:, cols]
        prev = jnp.where(row >= POOL_HALO - (win - 1), st_ref[:, :, cols], 0.0)
        acc = x + jnp.sum(prev, axis=1)
        diffs.append(acc / float(min(pos0 + 1, win)) - x)
    pool = _pool_project(diffs, pw_ref, ps_ref)
    o_ref[...] = (y_ref[...]
                  + jnp.dot(att_ref[...].astype(BF16), wo_ref[0:att_w, :],
                            preferred_element_type=F32)
                  + jnp.dot(pool, wo_ref[att_w:, :], preferred_element_type=F32))


def _mix_sample(att, xp, state16, pool_w, pool_scale, w_out, y, *, pos0):
    whole = lambda a: pl.BlockSpec(a.shape, lambda: (0,) * a.ndim)
    args = (att, xp, state16, pool_w, pool_scale, w_out, y)
    return pl.pallas_call(
        functools.partial(_mix_sample_kernel, pos0=pos0),
        in_specs=[whole(a) for a in args],
        out_specs=whole(y),
        out_shape=jax.ShapeDtypeStruct(y.shape, F32),
        compiler_params=pltpu.CompilerParams(
            vmem_limit_bytes=_vmem_limit(*(2 * a.size * a.dtype.itemsize for a in args),
                                         4 * state16.size * 4)),
        name="mix_sample",
    )(*args)


def _pad_cols(w, mult):
    pad = (-w.shape[-1]) % mult
    return jnp.pad(w, ((0, 0), (0, pad))) if pad else w


def _pad_rows(w, mult):
    pad = (-w.shape[0]) % mult
    return jnp.pad(w, ((0, pad), (0, 0))) if pad else w


def kernel(x_prompt, x_sample, cache_k, cache_v, state_pool, page_table, w_in, w_out, ffn1_norm,
           ffn1_gate, ffn1_up, ffn1_down, mix_norm, ffn2_norm, ffn2_gate, ffn2_up, ffn2_down,
           lambda_q1, lambda_k1, lambda_q2, lambda_k2, subln_g, pool_w, pool_scale, rel_bias,
           final_norm):
    batch, seq, d = x_prompt.shape
    db, sd, _ = x_sample.shape
    depth, n_phys, page, n_heads, dv = cache_v.shape
    dk = dv // 2
    width = n_heads * dv
    pool_width = state_pool.shape[-1]
    assert sd == 1 and page == PAGE_SIZE and cache_k.shape[-1] == 2 * dk
    assert w_in.shape[-1] == 4 * width and pool_width == width
    assert state_pool.shape[2] == POOL_HALO - 1 and rel_bias.shape[0] == NUM_BUCKETS
    n_past = page_table.shape[1] * PAGE_SIZE
    scale = dk ** -0.5 * LOG2E

    tm_p = min(512, batch * seq)
    tf = 512
    t_att = min(512, seq)
    tq_att = 128
    assert seq % tm_p == 0 and seq % t_att == 0

    row2 = lambda a: a.reshape(1, -1)
    yp = x_prompt.reshape(batch * seq, d)
    ys = x_sample.reshape(db * sd, d)
    bias = _bias_tiles(rel_bias, tq=tq_att)
    far = rel_bias[NUM_BUCKETS - 1]
    tab_dec = jnp.tile((rel_bias - far[None, :]).T * LOG2E, (2, 1))
    outs = {k: [] for k in ("kp", "vp", "pp", "ks", "vs", "ps")}

    for l in range(depth):
        lam_init = 0.8 - 0.6 * math.exp(-0.3 * l)
        lams = tuple(row2(a[l]) for a in (lambda_q1, lambda_k1, lambda_q2, lambda_k2))
        ffn_w = []
        for gate, up, down in ((ffn1_gate, ffn1_up, ffn1_down), (ffn2_gate, ffn2_up, ffn2_down)):
            ffn_w.append((_pad_cols(gate[l].astype(BF16), tf), _pad_cols(up[l].astype(BF16), tf),
                          _pad_rows(down[l].astype(BF16), tf)))
        w_in_b = w_in[l].astype(BF16)
        w_out_b = w_out[l].astype(BF16)
        pool_w_b = pool_w[l].astype(BF16)
        pool_s = row2(pool_scale[l])
        last = l == depth - 1
        fin = row2(final_norm)

        yp = _ffn(yp, row2(ffn1_norm[l]), *ffn_w[0], fin, tm=tm_p, tf=tf, final=False,
                  name="ffn1_prompt")
        q, k, v, xp = _proj(yp, row2(mix_norm[l]), w_in_b, tm=tm_p, scale=scale,
                            name="proj_prompt")
        att = _attn_prompt(q, k, v, bias, lams, row2(subln_g[l]), batch=batch, seq=seq,
                           n_heads=n_heads, t=t_att, tq=tq_att, lam_init=lam_init)
        yp = _mix_prompt(att, xp, pool_w_b, pool_s, w_out_b, yp, seq=seq, tm=tm_p)
        yp = _ffn(yp, row2(ffn2_norm[l]), *ffn_w[1], fin, tm=tm_p, tf=tf, final=last,
                  name="ffn2_prompt")
        outs["kp"].append(k.reshape(batch, seq, n_heads, dv))
        outs["vp"].append(v.reshape(batch, seq, n_heads, dv))
        outs["pp"].append(xp.reshape(batch, seq, pool_width)[:, seq - (POOL_HALO - 1):])

        ys = _ffn(ys, row2(ffn1_norm[l]), *ffn_w[0], fin, tm=db, tf=tf, final=False,
                  name="ffn1_sample")
        qs, ks, vs, xps = _proj(ys, row2(mix_norm[l]), w_in_b, tm=db, scale=scale,
                                name="proj_sample")
        att_s = _attn_decode(qs.reshape(db, n_heads, dv), ks.reshape(db, n_heads, dv),
                             vs.reshape(db, n_heads, dv), cache_k[l], cache_v[l], page_table,
                             tab_dec, lams, row2(subln_g[l]), n_pages_step=8, lam_init=lam_init)
        state16 = jnp.pad(state_pool[l], ((0, 0), (1, 0), (0, 0)))
        ys = _mix_sample(att_s.reshape(db, width), xps, state16, pool_w_b, pool_s, w_out_b, ys,
                         pos0=n_past)
        ys = _ffn(ys, row2(ffn2_norm[l]), *ffn_w[1], fin, tm=db, tf=tf, final=last,
                  name="ffn2_sample")
        outs["ks"].append(ks.reshape(db, sd, n_heads, dv))
        outs["vs"].append(vs.reshape(db, sd, n_heads, dv))
        outs["ps"].append(jnp.concatenate([state_pool[l][:, 1:], xps[:, None, :]], axis=1))

    st = lambda name: jnp.stack(outs[name], axis=0)
    return (yp.reshape(batch, seq, d), ys.reshape(db, sd, d), st("kp"), st("vp"), st("pp"),
            st("ks"), st("vs"), st("ps"))
```

```python
import functools
import math

import jax
import jax.numpy as jnp
from jax import lax
from jax.experimental import pallas as pl
from jax.experimental.pallas import tpu as pltpu

F32 = jnp.float32
BF16 = jnp.bfloat16

V7X_LANES = 128
V7X_VMEM_BYTES = 64 * 1024 * 1024
V7X_VMEM_CAP = V7X_VMEM_BYTES - 8 * 1024 * 1024

EPS = 1e-6
SUBLN_EPS = 1e-5
NEG = -1e30
POOL_WINDOWS = (2, 4, 8, 16)
POOL_HALO = 16
NUM_BUCKETS = 32
MAX_EXACT = NUM_BUCKETS // 2
MAX_DISTANCE = 128
LOG2E = math.log2(math.e)
PAGE_SIZE = 128


def _vmem_limit(*nbytes):
    est = int(sum(nbytes))
    return min(max(est, 16 * 1024 * 1024), V7X_VMEM_CAP)


def _rms(x, g, eps):
    return x * lax.rsqrt(jnp.mean(x * x, axis=-1, keepdims=True) + eps) * g


def _bucket(n):
    n = jnp.maximum(n, 0)
    nf = jnp.maximum(n, 1).astype(F32)
    large = MAX_EXACT + (jnp.log(nf / MAX_EXACT) / math.log(MAX_DISTANCE / MAX_EXACT)
                         * (NUM_BUCKETS - MAX_EXACT)).astype(jnp.int32)
    large = jnp.minimum(large, NUM_BUCKETS - 1)
    return jnp.where(n < MAX_EXACT, n, large)


def _ffn_kernel(x_ref, g_ref, wg_ref, wu_ref, wd_ref, fg_ref, o_ref, h_scr, *, n_f, last_tf,
                final):
    f = pl.program_id(1)
    tf = wg_ref.shape[1]

    @pl.when(f == 0)
    def _():
        h_scr[...] = _rms(x_ref[...], g_ref[...], EPS).astype(BF16)
        o_ref[...] = jnp.zeros(o_ref.shape, F32)

    def accumulate(cols):
        h = h_scr[...]
        gate = jnp.dot(h, wg_ref[:, 0:cols], preferred_element_type=F32)
        up = jnp.dot(h, wu_ref[:, 0:cols], preferred_element_type=F32)
        act = (gate / (1.0 + jnp.exp(-gate)) * up).astype(BF16)
        o_ref[...] += jnp.dot(act, wd_ref[0:cols, :], preferred_element_type=F32)

    if last_tf == tf:
        accumulate(tf)
    else:
        @pl.when(f < n_f - 1)
        def _():
            accumulate(tf)

        @pl.when(f == n_f - 1)
        def _():
            accumulate(last_tf)

    @pl.when(f == n_f - 1)
    def _():
        y = x_ref[...] + 0.5 * o_ref[...]
        if final:
            y = _rms(y, fg_ref[...], EPS)
        o_ref[...] = y


def _ffn(x, norm_g, wg, wu, wd, final_g, *, tm, tf, final, name):
    m, d = x.shape
    f_dim = wg.shape[1]
    n_f = pl.cdiv(f_dim, tf)
    last_tf = f_dim - (n_f - 1) * tf
    assert last_tf % V7X_LANES == 0
    vmem = _vmem_limit(2 * 2 * tm * d * 4,
                       2 * 3 * d * tf * 2,
                       tm * d * 2,
                       4 * tm * tf * 4 + tm * d * 4)
    return pl.pallas_call(
        functools.partial(_ffn_kernel, n_f=n_f, last_tf=last_tf, final=final),
        grid=(m // tm, n_f),
        in_specs=[
            pl.BlockSpec((tm, d), lambda i, f: (i, 0)),
            pl.BlockSpec((1, d), lambda i, f: (0, 0)),
            pl.BlockSpec((d, tf), lambda i, f: (0, f)),
            pl.BlockSpec((d, tf), lambda i, f: (0, f)),
            pl.BlockSpec((tf, d), lambda i, f: (f, 0)),
            pl.BlockSpec((1, d), lambda i, f: (0, 0)),
        ],
        out_specs=pl.BlockSpec((tm, d), lambda i, f: (i, 0)),
        out_shape=jax.ShapeDtypeStruct((m, d), F32),
        scratch_shapes=[pltpu.VMEM((tm, d), BF16)],
        compiler_params=pltpu.CompilerParams(
            dimension_semantics=("arbitrary", "arbitrary"), vmem_limit_bytes=vmem),
        name=name,
    )(x, norm_g, wg, wu, wd, final_g)


def _proj_kernel(x_ref, g_ref, w_ref, q_ref, k_ref, v_ref, p_ref, h_scr, *, scale):
    j = pl.program_id(1)

    @pl.when(j == 0)
    def _():
        h_scr[...] = _rms(x_ref[...], g_ref[...], EPS).astype(BF16)

    p = jnp.dot(h_scr[...], w_ref[...], preferred_element_type=F32)

    @pl.when(j == 0)
    def _():
        q_ref[...] = (p * scale).astype(BF16)

    @pl.when(j == 1)
    def _():
        k_ref[...] = p

    @pl.when(j == 2)
    def _():
        v_ref[...] = p

    @pl.when(j == 3)
    def _():
        p_ref[...] = p


def _proj(x, norm_g, w_in, *, tm, scale, name):
    m, d = x.shape
    w = w_in.shape[1] // 4
    out_spec = pl.BlockSpec((tm, w), lambda i, j: (i, 0))
    vmem = _vmem_limit(2 * tm * d * 4, 2 * d * w * 2, 2 * tm * w * (2 + 4 + 4 + 4),
                       tm * d * 2, 2 * tm * w * 4)
    return pl.pallas_call(
        functools.partial(_proj_kernel, scale=scale),
        grid=(m // tm, 4),
        in_specs=[
            pl.BlockSpec((tm, d), lambda i, j: (i, 0)),
            pl.BlockSpec((1, d), lambda i, j: (0, 0)),
            pl.BlockSpec((d, w), lambda i, j: (0, j)),
        ],
        out_specs=[out_spec, out_spec, out_spec, out_spec],
        out_shape=[jax.ShapeDtypeStruct((m, w), BF16)] + [jax.ShapeDtypeStruct((m, w), F32)] * 3,
        scratch_shapes=[pltpu.VMEM((tm, d), BF16)],
        compiler_params=pltpu.CompilerParams(
            dimension_semantics=("arbitrary", "arbitrary"), vmem_limit_bytes=vmem),
        name=name,
    )(x, norm_g, w_in)


def _bias_tile_kernel(tab_ref, o_ref, *, tq):
    h = pl.program_id(0)
    r = lax.broadcasted_iota(jnp.int32, (2 * tq, 2 * tq), 0)
    c = lax.broadcasted_iota(jnp.int32, (2 * tq, 2 * tq), 1)
    key = r - tq
    qry = c % tq
    far = tab_ref[h * NUM_BUCKETS + NUM_BUCKETS - 1]
    bucket = _bucket(qry - key)
    val = jnp.zeros(r.shape, F32)
    for b in range(NUM_BUCKETS - 1):
        val = jnp.where(bucket == b, (tab_ref[h * NUM_BUCKETS + b] - far) * LOG2E, val)
    o_ref[...] = jnp.where(key <= qry, val, NEG)


def _bias_tiles(rel_bias, *, tq):
    n_heads = rel_bias.shape[1]
    tab = rel_bias.T.reshape(-1)
    return pl.pallas_call(
        functools.partial(_bias_tile_kernel, tq=tq),
        grid=(n_heads,),
        in_specs=[pl.BlockSpec(memory_space=pltpu.SMEM)],
        out_specs=pl.BlockSpec((None, 2 * tq, 2 * tq), lambda h: (h, 0, 0)),
        out_shape=jax.ShapeDtypeStruct((n_heads, 2 * tq, 2 * tq), F32),
        compiler_params=pltpu.CompilerParams(
            dimension_semantics=("arbitrary",),
            vmem_limit_bytes=_vmem_limit(16 * 2 * tq * 2 * tq * 4)),
        name="rel_bias_tiles",
    )(tab)


def _lambda(lq1_ref, lk1_ref, lq2_ref, lk2_ref, lam_init):
    a = jnp.sum(lq1_ref[...] * lk1_ref[...], axis=-1, keepdims=True)
    b = jnp.sum(lq2_ref[...] * lk2_ref[...], axis=-1, keepdims=True)
    return jnp.exp(a) - jnp.exp(b) + lam_init


def _attn_kernel(lq1_ref, lk1_ref, lq2_ref, lk2_ref, g_ref, q_ref, k_ref, v_ref, bias_ref,
                 o_ref, kb_scr, vt_scr, m_scr, l_scr, acc_scr, *, t, tq, far_group, dk,
                 lam_init):
    qi = pl.program_id(2)
    n_strips = t // tq
    seq = k_ref.shape[0]
    sw = 2 * tq

    @pl.when(qi == 0)
    def _():
        kb_scr[0:tq, :] = jnp.zeros((tq, kb_scr.shape[1]), BF16)
        vt_scr[0] = jnp.zeros(vt_scr.shape[1:], BF16)
        kb_scr[tq:, :] = k_ref[...].astype(BF16)
        for c in range(seq // tq):
            vt_scr[c + 1] = v_ref[c * tq:(c + 1) * tq, :].T.astype(BF16)

    qa = q_ref[...].astype(F32)
    lane = lax.broadcasted_iota(jnp.int32, qa.shape, 1)
    q1 = jnp.where(lane < dk, qa, 0.0).astype(BF16)
    q2 = jnp.where(lane >= dk, qa, 0.0).astype(BF16)
    strips = [jnp.concatenate([q1[u * tq:(u + 1) * tq], q2[u * tq:(u + 1) * tq]], axis=0)
              for u in range(n_strips)]
    qs_all = jnp.concatenate(strips, axis=0)

    m_scr[...] = jnp.full(m_scr.shape, NEG, F32)
    l_scr[...] = jnp.zeros(l_scr.shape, F32)
    acc_scr[...] = jnp.zeros(acc_scr.shape, F32)

    def keys(row0, n):
        return kb_scr[pl.ds(pl.multiple_of(row0, tq), n), :]

    def values_t(chunk0, n_chunks):
        return jnp.concatenate([vt_scr[chunk0 + i] for i in range(n_chunks)], axis=1)

    def fold(lanes, pieces):
        m_prev = m_scr[:, lanes]
        m_new = m_prev
        for s, _ in pieces:
            m_new = jnp.maximum(m_new, jnp.max(s, axis=0, keepdims=True))
        alpha = jnp.exp2(m_prev - m_new)
        l_new = alpha * l_scr[:, lanes]
        pv = None
        for s, vt in pieces:
            p = jnp.exp2(s - m_new)
            l_new = l_new + jnp.sum(p, axis=0, keepdims=True)
            term = jnp.dot(vt, p.astype(BF16), preferred_element_type=F32)
            pv = term if pv is None else pv + term
        l_scr[:, lanes] = l_new
        acc_scr[:, lanes] = alpha * acc_scr[:, lanes] + pv
        m_scr[:, lanes] = m_new

    def scores(kc, qs):
        return lax.dot_general(kc, qs, (((1,), (1,)), ((), ())), preferred_element_type=F32)

    far_chunks = t // tq
    groups = [slice(g * far_group * sw, (g + 1) * far_group * sw)
              for g in range(n_strips // far_group)]

    def fold_far(row0, chunk0, n_chunks):
        kc = keys(row0, n_chunks * tq)
        vt = values_t(chunk0, n_chunks)
        pending = None
        for lanes in groups:
            s = scores(kc, qs_all[lanes])
            if pending is not None:
                fold(*pending)
            pending = (lanes, [(s, vt)])
        fold(*pending)

    def far_body(j, carry):
        fold_far(tq + j * t, 1 + j * far_chunks, far_chunks)
        return carry

    lax.fori_loop(0, qi - 1, far_body, 0)

    @pl.when(qi > 0)
    def _():
        fold_far(tq + (qi - 1) * t, 1 + (qi - 1) * far_chunks, far_chunks - 1)

    n_absent = jnp.where(qi == 0, tq, 0)
    pending = []
    near_ahead = 1
    for u in range(n_strips):
        lanes = slice(u * sw, (u + 1) * sw)
        qs = strips[u]
        pieces = []
        if u > 0:
            s_a = scores(keys(qi * t, u * tq), qs)
            row = lax.broadcasted_iota(jnp.int32, s_a.shape, 0)
            s_a = jnp.where(row < n_absent, NEG, s_a)
            pieces.append((s_a, values_t(qi * far_chunks, u)))
        bias = bias_ref[...]
        if u == 0:
            row = lax.broadcasted_iota(jnp.int32, bias.shape, 0)
            bias = jnp.where(row < n_absent, NEG, bias)
        s_b = scores(keys(qi * t + u * tq, 2 * tq), qs) + bias
        pieces.append((s_b, values_t(qi * far_chunks + u, 2)))
        pending.append((lanes, pieces))
        if len(pending) > near_ahead:
            fold(*pending.pop(0))
    for item in pending:
        fold(*item)

    o = acc_scr[...] / l_scr[...]
    lam = _lambda(lq1_ref, lk1_ref, lq2_ref, lk2_ref, lam_init)
    for u in range(n_strips):
        w = o[:, u * sw:u * sw + tq] - lam * o[:, u * sw + tq:(u + 1) * sw]
        y = (w * lax.rsqrt(jnp.mean(w * w, axis=0, keepdims=True) + SUBLN_EPS)
             * g_ref[...] * (1.0 - lam_init))
        o_ref[u * tq:(u + 1) * tq, :] = y.T.astype(o_ref.dtype)


def _attn_prompt(q, k, v, bias, lams, subln_g, *, batch, seq, n_heads, t, tq, lam_init):
    dv = q.shape[1] // n_heads
    nq = seq // t
    assert t % tq == 0 and tq >= MAX_DISTANCE
    g_cols = jnp.broadcast_to(subln_g.reshape(dv, 1), (dv, tq))
    lam_spec = pl.BlockSpec((1, dv // 2), lambda h, b, i: (0, 0))
    kv_spec = pl.BlockSpec((seq, dv), lambda h, b, i: (b, h))
    vmem = _vmem_limit(2 * 2 * seq * dv * 4, 2 * (seq + tq) * dv * 2, 2 * 2 * tq * 2 * tq * 4,
                       4 * t * dv * 2, dv * 2 * t * 4, 4 * t * 2 * t * 4)
    return pl.pallas_call(
        functools.partial(_attn_kernel, t=t, tq=tq, far_group=t // tq, dk=dv // 2,
                          lam_init=lam_init),
        grid=(n_heads, batch, nq),
        in_specs=[lam_spec, lam_spec, lam_spec, lam_spec,
                  pl.BlockSpec((dv, tq), lambda h, b, i: (0, 0)),
                  pl.BlockSpec((t, dv), lambda h, b, i: (b * nq + i, h)),
                  kv_spec, kv_spec,
                  pl.BlockSpec((None, 2 * tq, 2 * tq), lambda h, b, i: (h, 0, 0))],
        out_specs=pl.BlockSpec((t, dv), lambda h, b, i: (b * nq + i, h)),
        out_shape=jax.ShapeDtypeStruct(q.shape, BF16),
        scratch_shapes=[pltpu.VMEM((seq + tq, dv), BF16),
                        pltpu.VMEM((seq // tq + 1, dv, tq), BF16),
                        pltpu.VMEM((1, 2 * t), F32), pltpu.VMEM((1, 2 * t), F32),
                        pltpu.VMEM((dv, 2 * t), F32)],
        compiler_params=pltpu.CompilerParams(
            dimension_semantics=("arbitrary", "arbitrary", "arbitrary"), vmem_limit_bytes=vmem),
        name="attn_prompt",
    )(*lams, g_cols, q, k, v, bias)


def _decode_kernel(pt_ref, lq1_ref, lk1_ref, lq2_ref, lk2_ref, g_ref, tab_ref, q_ref, kn_ref,
                   vn_ref, *rest, n_pages_step, n_heads, dv, lam_init):
    del pt_ref
    p_n = n_pages_step
    k_refs = rest[:p_n]
    v_refs = rest[p_n:2 * p_n]
    o_ref, qm_scr, bias_scr, m_scr, l_scr, acc_scr = rest[2 * p_n:]
    c = pl.program_id(1)
    n_c = pl.num_programs(1)
    rows = 2 * n_heads
    page_cols = PAGE_SIZE * n_heads

    @pl.when(c == 0)
    def _():
        q = q_ref[...].astype(F32)
        lane = lax.broadcasted_iota(jnp.int32, q.shape, 1)
        qm_scr[...] = jnp.concatenate(
            [jnp.where(lane < dv // 2, q, 0.0), jnp.where(lane >= dv // 2, q, 0.0)],
            axis=0).astype(BF16)
        key_row = lax.broadcasted_iota(jnp.int32, (1, page_cols), 1) // n_heads
        bucket = _bucket(PAGE_SIZE - key_row)
        val = jnp.zeros((rows, page_cols), F32)
        for b in range(NUM_BUCKETS - 1):
            val = jnp.where(bucket == b, tab_ref[:, b:b + 1], val)
        bias_scr[...] = val
        m_scr[...] = jnp.full(m_scr.shape, NEG, F32)
        l_scr[...] = jnp.zeros(l_scr.shape, F32)
        acc_scr[...] = jnp.zeros(acc_scr.shape, F32)

    qm = qm_scr[...]
    row = lax.broadcasted_iota(jnp.int32, (rows, page_cols), 0)
    col = lax.broadcasted_iota(jnp.int32, (rows, page_cols), 1)
    own = (col % n_heads) == (row % n_heads)
    near = jnp.where(c == n_c - 1, bias_scr[...], 0.0)
    pieces = []
    for j in range(p_n):
        kf = k_refs[j][...].reshape(page_cols, dv).astype(BF16)
        sj = lax.dot_general(qm, kf, (((1,), (1,)), ((), ())), preferred_element_type=F32)
        if j == p_n - 1:
            sj = sj + near
        pieces.append(jnp.where(own, sj, NEG))
    s = jnp.concatenate(pieces, axis=1)

    def update(s, pv_fn):
        m_prev = m_scr[...]
        m_new = jnp.maximum(m_prev, jnp.max(s, axis=-1, keepdims=True))
        p = jnp.exp2(s - m_new)
        alpha = jnp.exp2(m_prev - m_new)
        l_scr[...] = alpha * l_scr[...] + jnp.sum(p, axis=-1, keepdims=True)
        acc_scr[...] = alpha * acc_scr[...] + pv_fn(p)
        m_scr[...] = m_new

    def pv_pages(p):
        pb = p.astype(BF16)
        out = None
        for j in range(p_n):
            vf = v_refs[j][...].reshape(page_cols, dv).astype(BF16)
            term = jnp.dot(pb[:, j * page_cols:(j + 1) * page_cols], vf,
                           preferred_element_type=F32)
            out = term if out is None else out + term
        return out

    update(s, pv_pages)

    @pl.when(c == n_c - 1)
    def _():
        kn = jnp.concatenate([kn_ref[...]] * 2, axis=0)
        vn = jnp.concatenate([vn_ref[...]] * 2, axis=0)
        s_self = jnp.sum(qm.astype(F32) * kn, axis=-1, keepdims=True) + tab_ref[:, 0:1]
        update(s_self, lambda p: p * vn)
        o = acc_scr[...] / l_scr[...]
        lam = _lambda(lq1_ref, lk1_ref, lq2_ref, lk2_ref, lam_init)
        w = o[0:n_heads] - lam * o[n_heads:rows]
        o_ref[...] = _rms(w, g_ref[...], SUBLN_EPS) * (1.0 - lam_init)


def _attn_decode(q, k_new, v_new, cache_k, cache_v, page_table, tab, lams, subln_g, *,
                 n_pages_step, lam_init):
    db, n_heads, dv = q.shape
    n_pages = page_table.shape[1]
    p_n = n_pages_step
    rows = 2 * n_heads
    page_cols = PAGE_SIZE * n_heads
    assert n_pages % p_n == 0 and PAGE_SIZE >= MAX_DISTANCE
    small = lambda shape: pl.BlockSpec(shape, lambda b, c, pt: (0,) * len(shape))
    row_spec = pl.BlockSpec((None, n_heads, dv), lambda b, c, pt: (b, 0, 0))

    def page_spec(j):
        return pl.BlockSpec((None, PAGE_SIZE, n_heads, dv),
                            lambda b, c, pt: (pt[b, c * p_n + j], 0, 0, 0))

    vmem = _vmem_limit(2 * 2 * p_n * page_cols * dv * 4,
                       2 * p_n * page_cols * dv * 2,
                       6 * rows * p_n * page_cols * 4)
    grid_spec = pltpu.PrefetchScalarGridSpec(
        num_scalar_prefetch=1,
        grid=(db, n_pages // p_n),
        in_specs=[small((1, dv // 2))] * 4 + [small((1, dv)), small((rows, NUM_BUCKETS)),
                                              row_spec, row_spec, row_spec]
        + [page_spec(j) for j in range(p_n)] * 2,
        out_specs=pl.BlockSpec((None, n_heads, dv), lambda b, c, pt: (b, 0, 0)),
        scratch_shapes=[pltpu.VMEM((rows, dv), BF16), pltpu.VMEM((rows, page_cols), F32),
                        pltpu.VMEM((rows, 1), F32), pltpu.VMEM((rows, 1), F32),
                        pltpu.VMEM((rows, dv), F32)],
    )
    return pl.pallas_call(
        functools.partial(_decode_kernel, n_pages_step=p_n, n_heads=n_heads, dv=dv,
                          lam_init=lam_init),
        grid_spec=grid_spec,
        out_shape=jax.ShapeDtypeStruct((db, n_heads, dv), F32),
        compiler_params=pltpu.CompilerParams(
            dimension_semantics=("arbitrary", "arbitrary"), vmem_limit_bytes=vmem),
        name="attn_decode",
    )(page_table, *lams, subln_g, tab, q, k_new, v_new,
      *([cache_k] * p_n), *([cache_v] * p_n))


def _pool_project(diffs, pw_ref, ps_ref):
    outs = [jnp.dot(d.astype(BF16), pw_ref[g], preferred_element_type=F32)
            for g, d in enumerate(diffs)]
    return (jnp.concatenate(outs, axis=1) * ps_ref[...]).astype(BF16)


def _mix_kernel(att_ref, xp_ref, halo_ref, pw_ref, ps_ref, wo_ref, y_ref, o_ref, full_scr,
                *, tm, tiles_per_seq):
    t_in_seq = lax.rem(pl.program_id(0), tiles_per_seq)
    att_w = att_ref.shape[1]
    gc = xp_ref.shape[1] // len(POOL_WINDOWS)
    full_scr[0:POOL_HALO, :] = jnp.where(t_in_seq == 0, 0.0, halo_ref[...])
    full_scr[POOL_HALO:, :] = xp_ref[...]
    pos = t_in_seq * tm + lax.broadcasted_iota(jnp.int32, (tm, 1), 0)
    diffs = []
    for g, win in enumerate(POOL_WINDOWS):
        cols = slice(g * gc, (g + 1) * gc)
        x = xp_ref[:, cols]
        acc = x
        for d in range(1, win):
            acc = acc + full_scr[POOL_HALO - d:POOL_HALO - d + tm, cols]
        cnt = jnp.minimum(pos + 1, win).astype(F32)
        diffs.append(acc / cnt - x)
    pool = _pool_project(diffs, pw_ref, ps_ref)
    o_ref[...] = (y_ref[...]
                  + jnp.dot(att_ref[...], wo_ref[0:att_w, :], preferred_element_type=F32)
                  + jnp.dot(pool, wo_ref[att_w:, :], preferred_element_type=F32))


def _mix_prompt(att, xp, pool_w, pool_scale, w_out, y, *, seq, tm):
    m, d = y.shape
    aw, pw = att.shape[1], xp.shape[1]
    n_g, gc = pool_w.shape[0], pool_w.shape[1]
    tiles_per_seq = seq // tm
    halo_blocks = tm // POOL_HALO
    vmem = _vmem_limit(2 * tm * (aw * 2 + pw * 4 + 2 * d * 4), 2 * POOL_HALO * pw * 4,
                       2 * (aw + pw) * d * 2, 2 * n_g * gc * gc * 2,
                       (tm + POOL_HALO) * pw * 4, 4 * tm * pw * 4)
    return pl.pallas_call(
        functools.partial(_mix_kernel, tm=tm, tiles_per_seq=tiles_per_seq),
        grid=(m // tm,),
        in_specs=[
            pl.BlockSpec((tm, aw), lambda i: (i, 0)),
            pl.BlockSpec((tm, pw), lambda i: (i, 0)),
            pl.BlockSpec((POOL_HALO, pw), lambda i: (jnp.maximum(i * halo_blocks - 1, 0), 0)),
            pl.BlockSpec((n_g, gc, gc), lambda i: (0, 0, 0)),
            pl.BlockSpec((1, pw), lambda i: (0, 0)),
            pl.BlockSpec((aw + pw, d), lambda i: (0, 0)),
            pl.BlockSpec((tm, d), lambda i: (i, 0)),
        ],
        out_specs=pl.BlockSpec((tm, d), lambda i: (i, 0)),
        out_shape=jax.ShapeDtypeStruct((m, d), F32),
        scratch_shapes=[pltpu.VMEM((tm + POOL_HALO, pw), F32)],
        compiler_params=pltpu.CompilerParams(
            dimension_semantics=("arbitrary",), vmem_limit_bytes=vmem),
        name="mix_prompt",
    )(att, xp, xp, pool_w, pool_scale, w_out, y)


def _mix_sample_kernel(att_ref, xp_ref, st_ref, pw_ref, ps_ref, wo_ref, y_ref, o_ref, *, pos0):
    att_w = att_ref.shape[1]
    gc = xp_ref.shape[1] // len(POOL_WINDOWS)
    row = lax.broadcasted_iota(jnp.int32, (1, POOL_HALO, 1), 1)
    diffs = []
    for g, win in enumerate(POOL_WINDOWS):
        cols = slice(g * gc, (g + 1) * gc)
        x = xp_ref[:, cols]
        prev = jnp.where(row >= POOL_HALO - (win - 1), st_ref[:, :, cols], 0.0)
        acc = x + jnp.sum(prev, axis=1)
        diffs.append(acc / float(min(pos0 + 1, win)) - x)
    pool = _pool_project(diffs, pw_ref, ps_ref)
    o_ref[...] = (y_ref[...]
                  + jnp.dot(att_ref[...].astype(BF16), wo_ref[0:att_w, :],
                            preferred_element_type=F32)
                  + jnp.dot(pool, wo_ref[att_w:, :], preferred_element_type=F32))


def _mix_sample(att, xp, state16, pool_w, pool_scale, w_out, y, *, pos0):
    whole = lambda a: pl.BlockSpec(a.shape, lambda: (0,) * a.ndim)
    args = (att, xp, state16, pool_w, pool_scale, w_out, y)
    return pl.pallas_call(
        functools.partial(_mix_sample_kernel, pos0=pos0),
        in_specs=[whole(a) for a in args],
        out_specs=whole(y),
        out_shape=jax.ShapeDtypeStruct(y.shape, F32),
        compiler_params=pltpu.CompilerParams(
            vmem_limit_bytes=_vmem_limit(*(2 * a.size * a.dtype.itemsize for a in args),
                                         4 * state16.size * 4)),
        name="mix_sample",
    )(*args)


def kernel(x_prompt, x_sample, cache_k, cache_v, state_pool, page_table, w_in, w_out, ffn1_norm,
           ffn1_gate, ffn1_up, ffn1_down, mix_norm, ffn2_norm, ffn2_gate, ffn2_up, ffn2_down,
           lambda_q1, lambda_k1, lambda_q2, lambda_k2, subln_g, pool_w, pool_scale, rel_bias,
           final_norm):
    batch, seq, d = x_prompt.shape
    db, sd, _ = x_sample.shape
    depth, n_phys, page, n_heads, dv = cache_v.shape
    dk = dv // 2
    width = n_heads * dv
    pool_width = state_pool.shape[-1]
    assert sd == 1 and page == PAGE_SIZE and cache_k.shape[-1] == 2 * dk
    assert w_in.shape[-1] == 4 * width and pool_width == width
    assert state_pool.shape[2] == POOL_HALO - 1 and rel_bias.shape[0] == NUM_BUCKETS
    n_past = page_table.shape[1] * PAGE_SIZE
    scale = dk ** -0.5 * LOG2E

    tm_p = min(512, batch * seq)
    tf = 512
    t_att = min(512, seq)
    tq_att = 128
    assert seq % tm_p == 0 and seq % t_att == 0

    row2 = lambda a: a.reshape(1, -1)
    yp = x_prompt.reshape(batch * seq, d)
    ys = x_sample.reshape(db * sd, d)
    bias = _bias_tiles(rel_bias, tq=tq_att)
    far = rel_bias[NUM_BUCKETS - 1]
    tab_dec = jnp.tile((rel_bias - far[None, :]).T * LOG2E, (2, 1))
    outs = {k: [] for k in ("kp", "vp", "pp", "ks", "vs", "ps")}

    for l in range(depth):
        lam_init = 0.8 - 0.6 * math.exp(-0.3 * l)
        lams = tuple(row2(a[l]) for a in (lambda_q1, lambda_k1, lambda_q2, lambda_k2))
        ffn_w = []
        for gate, up, down in ((ffn1_gate, ffn1_up, ffn1_down), (ffn2_gate, ffn2_up, ffn2_down)):
            ffn_w.append((gate[l].astype(BF16), up[l].astype(BF16), down[l].astype(BF16)))
        w_in_b = w_in[l].astype(BF16)
        w_out_b = w_out[l].astype(BF16)
        pool_w_b = pool_w[l].astype(BF16)
        pool_s = row2(pool_scale[l])
        last = l == depth - 1
        fin = row2(final_norm)

        yp = _ffn(yp, row2(ffn1_norm[l]), *ffn_w[0], fin, tm=tm_p, tf=tf, final=False,
                  name="ffn1_prompt")
        q, k, v, xp = _proj(yp, row2(mix_norm[l]), w_in_b, tm=tm_p, scale=scale,
                            name="proj_prompt")
        att = _attn_prompt(q, k, v, bias, lams, row2(subln_g[l]), batch=batch, seq=seq,
                           n_heads=n_heads, t=t_att, tq=tq_att, lam_init=lam_init)
        yp = _mix_prompt(att, xp, pool_w_b, pool_s, w_out_b, yp, seq=seq, tm=tm_p)
        yp = _ffn(yp, row2(ffn2_norm[l]), *ffn_w[1], fin, tm=tm_p, tf=tf, final=last,
                  name="ffn2_prompt")
        outs["kp"].append(k.reshape(batch, seq, n_heads, dv))
        outs["vp"].append(v.reshape(batch, seq, n_heads, dv))
        outs["pp"].append(xp.reshape(batch, seq, pool_width)[:, seq - (POOL_HALO - 1):])

        ys = _ffn(ys, row2(ffn1_norm[l]), *ffn_w[0], fin, tm=db, tf=tf, final=False,
                  name="ffn1_sample")
        qs, ks, vs, xps = _proj(ys, row2(mix_norm[l]), w_in_b, tm=db, scale=scale,
                                name="proj_sample")
        att_s = _attn_decode(qs.reshape(db, n_heads, dv), ks.reshape(db, n_heads, dv),
                             vs.reshape(db, n_heads, dv), cache_k[l], cache_v[l], page_table,
                             tab_dec, lams, row2(subln_g[l]), n_pages_step=8, lam_init=lam_init)
        state16 = jnp.pad(state_pool[l], ((0, 0), (1, 0), (0, 0)))
        ys = _mix_sample(att_s.reshape(db, width), xps, state16, pool_w_b, pool_s, w_out_b, ys,
                         pos0=n_past)
        ys = _ffn(ys, row2(ffn2_norm[l]), *ffn_w[1], fin, tm=db, tf=tf, final=last,
                  name="ffn2_sample")
        outs["ks"].append(ks.reshape(db, sd, n_heads, dv))
        outs["vs"].append(vs.reshape(db, sd, n_heads, dv))
        outs["ps"].append(jnp.concatenate([state_pool[l][:, 1:], xps[:, None, :]], axis=1))

    st = lambda name: jnp.stack(outs[name], axis=0)
    return (yp.reshape(batch, seq, d), ys.reshape(db, sd, d), st("kp"), st("vp"), st("pp"),
            st("ks"), st("vs"), st("ps"))
```

```python
import functools
import math

import jax
import jax.numpy as jnp
from jax import lax
from jax.experimental import pallas as pl
from jax.experimental.pallas import tpu as pltpu

F32 = jnp.float32
BF16 = jnp.bfloat16

V7X_LANES = 128
V7X_VMEM_BYTES = 64 * 1024 * 1024
V7X_VMEM_CAP = V7X_VMEM_BYTES - 8 * 1024 * 1024

EPS = 1e-6
SUBLN_EPS = 1e-5
NEG = -1e30
POOL_WINDOWS = (2, 4, 8, 16)
POOL_HALO = 16
NUM_BUCKETS = 32
MAX_EXACT = NUM_BUCKETS // 2
MAX_DISTANCE = 128
LOG2E = math.log2(math.e)
PAGE_SIZE = 128


def _vmem_limit(*nbytes):
    est = int(sum(nbytes))
    return min(max(est, 16 * 1024 * 1024), V7X_VMEM_CAP)


def _rms(x, g, eps):
    return x * lax.rsqrt(jnp.mean(x * x, axis=-1, keepdims=True) + eps) * g


def _bucket(n):
    n = jnp.maximum(n, 0)
    nf = jnp.maximum(n, 1).astype(F32)
    large = MAX_EXACT + (jnp.log(nf / MAX_EXACT) / math.log(MAX_DISTANCE / MAX_EXACT)
                         * (NUM_BUCKETS - MAX_EXACT)).astype(jnp.int32)
    large = jnp.minimum(large, NUM_BUCKETS - 1)
    return jnp.where(n < MAX_EXACT, n, large)


def _ffn_kernel(x_ref, g_ref, wg_ref, wu_ref, wd_ref, fg_ref, o_ref, h_scr, *, n_f, last_tf,
                final):
    f = pl.program_id(1)
    tf = wg_ref.shape[1]

    @pl.when(f == 0)
    def _():
        h_scr[...] = _rms(x_ref[...], g_ref[...], EPS).astype(BF16)
        o_ref[...] = jnp.zeros(o_ref.shape, F32)

    def accumulate(cols):
        h = h_scr[...]
        gate = jnp.dot(h, wg_ref[:, 0:cols], preferred_element_type=F32)
        up = jnp.dot(h, wu_ref[:, 0:cols], preferred_element_type=F32)
        act = (gate / (1.0 + jnp.exp(-gate)) * up).astype(BF16)
        o_ref[...] += jnp.dot(act, wd_ref[0:cols, :], preferred_element_type=F32)

    if last_tf == tf:
        accumulate(tf)
    else:
        @pl.when(f < n_f - 1)
        def _():
            accumulate(tf)

        @pl.when(f == n_f - 1)
        def _():
            accumulate(last_tf)

    @pl.when(f == n_f - 1)
    def _():
        y = x_ref[...] + 0.5 * o_ref[...]
        if final:
            y = _rms(y, fg_ref[...], EPS)
        o_ref[...] = y


def _ffn(x, norm_g, wg, wu, wd, final_g, *, tm, tf, final, name):
    m, d = x.shape
    f_dim = wg.shape[1]
    n_f = pl.cdiv(f_dim, tf)
    last_tf = f_dim - (n_f - 1) * tf
    assert last_tf % V7X_LANES == 0
    vmem = _vmem_limit(2 * 2 * tm * d * 4,
                       2 * 3 * d * tf * 2,
                       tm * d * 2,
                       4 * tm * tf * 4 + tm * d * 4)
    return pl.pallas_call(
        functools.partial(_ffn_kernel, n_f=n_f, last_tf=last_tf, final=final),
        grid=(m // tm, n_f),
        in_specs=[
            pl.BlockSpec((tm, d), lambda i, f: (i, 0)),
            pl.BlockSpec((1, d), lambda i, f: (0, 0)),
            pl.BlockSpec((d, tf), lambda i, f: (0, f)),
            pl.BlockSpec((d, tf), lambda i, f: (0, f)),
            pl.BlockSpec((tf, d), lambda i, f: (f, 0)),
            pl.BlockSpec((1, d), lambda i, f: (0, 0)),
        ],
        out_specs=pl.BlockSpec((tm, d), lambda i, f: (i, 0)),
        out_shape=jax.ShapeDtypeStruct((m, d), F32),
        scratch_shapes=[pltpu.VMEM((tm, d), BF16)],
        compiler_params=pltpu.CompilerParams(
            dimension_semantics=("arbitrary", "arbitrary"), vmem_limit_bytes=vmem),
        name=name,
    )(x, norm_g, wg, wu, wd, final_g)


def _proj_kernel(x_ref, g_ref, w_ref, q_ref, k_ref, v_ref, p_ref, h_scr, *, scale):
    j = pl.program_id(1)

    @pl.when(j == 0)
    def _():
        h_scr[...] = _rms(x_ref[...], g_ref[...], EPS).astype(BF16)

    p = jnp.dot(h_scr[...], w_ref[...], preferred_element_type=F32)

    @pl.when(j == 0)
    def _():
        q_ref[...] = (p * scale).astype(BF16)

    @pl.when(j == 1)
    def _():
        k_ref[...] = p

    @pl.when(j == 2)
    def _():
        v_ref[...] = p

    @pl.when(j == 3)
    def _():
        p_ref[...] = p


def _proj(x, norm_g, w_in, *, tm, scale, name):
    m, d = x.shape
    w = w_in.shape[1] // 4
    out_spec = pl.BlockSpec((tm, w), lambda i, j: (i, 0))
    vmem = _vmem_limit(2 * tm * d * 4, 2 * d * w * 2, 2 * tm * w * (2 + 4 + 4 + 4),
                       tm * d * 2, 2 * tm * w * 4)
    return pl.pallas_call(
        functools.partial(_proj_kernel, scale=scale),
        grid=(m // tm, 4),
        in_specs=[
            pl.BlockSpec((tm, d), lambda i, j: (i, 0)),
            pl.BlockSpec((1, d), lambda i, j: (0, 0)),
            pl.BlockSpec((d, w), lambda i, j: (0, j)),
        ],
        out_specs=[out_spec, out_spec, out_spec, out_spec],
        out_shape=[jax.ShapeDtypeStruct((m, w), BF16)] + [jax.ShapeDtypeStruct((m, w), F32)] * 3,
        scratch_shapes=[pltpu.VMEM((tm, d), BF16)],
        compiler_params=pltpu.CompilerParams(
            dimension_semantics=("arbitrary", "arbitrary"), vmem_limit_bytes=vmem),
        name=name,
    )(x, norm_g, w_in)


def _bias_tile_kernel(tab_ref, o_ref, *, tq):
    h = pl.program_id(0)
    r = lax.broadcasted_iota(jnp.int32, (2 * tq, 2 * tq), 0)
    c = lax.broadcasted_iota(jnp.int32, (2 * tq, 2 * tq), 1)
    key = r - tq
    qry = c % tq
    far = tab_ref[h * NUM_BUCKETS + NUM_BUCKETS - 1]
    bucket = _bucket(qry - key)
    val = jnp.zeros(r.shape, F32)
    for b in range(NUM_BUCKETS - 1):
        val = jnp.where(bucket == b, (tab_ref[h * NUM_BUCKETS + b] - far) * LOG2E, val)
    o_ref[...] = jnp.where(key <= qry, val, NEG)


def _bias_tiles(rel_bias, *, tq):
    n_heads = rel_bias.shape[1]
    tab = rel_bias.T.reshape(-1)
    return pl.pallas_call(
        functools.partial(_bias_tile_kernel, tq=tq),
        grid=(n_heads,),
        in_specs=[pl.BlockSpec(memory_space=pltpu.SMEM)],
        out_specs=pl.BlockSpec((None, 2 * tq, 2 * tq), lambda h: (h, 0, 0)),
        out_shape=jax.ShapeDtypeStruct((n_heads, 2 * tq, 2 * tq), F32),
        compiler_params=pltpu.CompilerParams(
            dimension_semantics=("arbitrary",),
            vmem_limit_bytes=_vmem_limit(16 * 2 * tq * 2 * tq * 4)),
        name="rel_bias_tiles",
    )(tab)


def _lambda(lq1_ref, lk1_ref, lq2_ref, lk2_ref, lam_init):
    a = jnp.sum(lq1_ref[...] * lk1_ref[...], axis=-1, keepdims=True)
    b = jnp.sum(lq2_ref[...] * lk2_ref[...], axis=-1, keepdims=True)
    return jnp.exp(a) - jnp.exp(b) + lam_init


def _attn_kernel(lq1_ref, lk1_ref, lq2_ref, lk2_ref, g_ref, q_ref, k_ref, v_ref, bias_ref,
                 o_ref, kb_scr, vt_scr, m_scr, l_scr, acc_scr, *, t, tq, dk, lam_init):
    qi = pl.program_id(2)
    n_strips = t // tq
    seq = k_ref.shape[0]
    sw = 2 * tq

    @pl.when(qi == 0)
    def _():
        kb_scr[0:tq, :] = jnp.zeros((tq, kb_scr.shape[1]), BF16)
        vt_scr[0] = jnp.zeros(vt_scr.shape[1:], BF16)
        kb_scr[tq:, :] = k_ref[...].astype(BF16)
        for c in range(seq // tq):
            vt_scr[c + 1] = v_ref[c * tq:(c + 1) * tq, :].T.astype(BF16)

    qa = q_ref[...].astype(F32)
    lane = lax.broadcasted_iota(jnp.int32, qa.shape, 1)
    q1 = jnp.where(lane < dk, qa, 0.0).astype(BF16)
    q2 = jnp.where(lane >= dk, qa, 0.0).astype(BF16)
    strips = [jnp.concatenate([q1[u * tq:(u + 1) * tq], q2[u * tq:(u + 1) * tq]], axis=0)
              for u in range(n_strips)]
    qs_all = jnp.concatenate(strips, axis=0)

    m_scr[...] = jnp.full(m_scr.shape, NEG, F32)
    l_scr[...] = jnp.zeros(l_scr.shape, F32)
    acc_scr[...] = jnp.zeros(acc_scr.shape, F32)

    def keys(row0, n):
        return kb_scr[pl.ds(pl.multiple_of(row0, tq), n), :]

    def values_t(chunk0, n_chunks):
        return jnp.concatenate([vt_scr[chunk0 + i] for i in range(n_chunks)], axis=1)

    def fold(lanes, pieces):
        m_prev = m_scr[:, lanes]
        m_new = m_prev
        for s, _ in pieces:
            m_new = jnp.maximum(m_new, jnp.max(s, axis=0, keepdims=True))
        alpha = jnp.exp2(m_prev - m_new)
        l_new = alpha * l_scr[:, lanes]
        pv = None
        for s, vt in pieces:
            p = jnp.exp2(s - m_new)
            l_new = l_new + jnp.sum(p, axis=0, keepdims=True)
            term = jnp.dot(vt, p.astype(BF16), preferred_element_type=F32)
            pv = term if pv is None else pv + term
        l_scr[:, lanes] = l_new
        acc_scr[:, lanes] = alpha * acc_scr[:, lanes] + pv
        m_scr[:, lanes] = m_new

    def scores(kc, qs):
        return lax.dot_general(kc, qs, (((1,), (1,)), ((), ())), preferred_element_type=F32)

    far_chunks = t // tq
    every = slice(0, n_strips * sw)

    def fold_far(blocks):
        scored = []
        for chunk0, n_chunks in blocks:
            kc = keys(chunk0 * tq, n_chunks * tq)
            scored.append((scores(kc, qs_all), values_t(chunk0, n_chunks)))
        for piece in scored:
            fold(every, [piece])

    n_far_blocks = jnp.maximum(qi - 1, 0)
    n_pairs = n_far_blocks // 2

    def far_body(j, carry):
        c0 = 1 + j * 2 * far_chunks
        fold_far([(c0, far_chunks), (c0 + far_chunks, far_chunks)])
        return carry

    lax.fori_loop(0, n_pairs, far_body, 0)
    tail0 = 1 + n_pairs * 2 * far_chunks

    @pl.when((qi > 0) & (n_far_blocks % 2 == 0))
    def _():
        fold_far([(tail0, far_chunks - 1)])

    @pl.when(n_far_blocks % 2 == 1)
    def _():
        fold_far([(tail0, far_chunks), (tail0 + far_chunks, far_chunks - 1)])

    n_absent = jnp.where(qi == 0, tq, 0)
    pending = []
    near_ahead = 1
    for u in range(n_strips):
        lanes = slice(u * sw, (u + 1) * sw)
        qs = strips[u]
        pieces = []
        if u > 0:
            s_a = scores(keys(qi * t, u * tq), qs)
            row = lax.broadcasted_iota(jnp.int32, s_a.shape, 0)
            s_a = jnp.where(row < n_absent, NEG, s_a)
            pieces.append((s_a, values_t(qi * far_chunks, u)))
        bias = bias_ref[...]
        if u == 0:
            row = lax.broadcasted_iota(jnp.int32, bias.shape, 0)
            bias = jnp.where(row < n_absent, NEG, bias)
        s_b = scores(keys(qi * t + u * tq, 2 * tq), qs) + bias
        pieces.append((s_b, values_t(qi * far_chunks + u, 2)))
        pending.append((lanes, pieces))
        if len(pending) > near_ahead:
            fold(*pending.pop(0))
    for item in pending:
        fold(*item)

    o = acc_scr[...] / l_scr[...]
    lam = _lambda(lq1_ref, lk1_ref, lq2_ref, lk2_ref, lam_init)
    for u in range(n_strips):
        w = o[:, u * sw:u * sw + tq] - lam * o[:, u * sw + tq:(u + 1) * sw]
        y = (w * lax.rsqrt(jnp.mean(w * w, axis=0, keepdims=True) + SUBLN_EPS)
             * g_ref[...] * (1.0 - lam_init))
        o_ref[u * tq:(u + 1) * tq, :] = y.T.astype(o_ref.dtype)


def _attn_prompt(q, k, v, bias, lams, subln_g, *, batch, seq, n_heads, t, tq, lam_init):
    dv = q.shape[1] // n_heads
    nq = seq // t
    assert t % tq == 0 and tq >= MAX_DISTANCE
    g_cols = jnp.broadcast_to(subln_g.reshape(dv, 1), (dv, tq))
    lam_spec = pl.BlockSpec((1, dv // 2), lambda h, b, i: (0, 0))
    kv_spec = pl.BlockSpec((seq, dv), lambda h, b, i: (b, h))
    vmem = _vmem_limit(2 * 2 * seq * dv * 4, 2 * (seq + tq) * dv * 2, 2 * 2 * tq * 2 * tq * 4,
                       4 * t * dv * 2, dv * 2 * t * 4, 4 * t * 2 * t * 4)
    return pl.pallas_call(
        functools.partial(_attn_kernel, t=t, tq=tq, dk=dv // 2, lam_init=lam_init),
        grid=(n_heads, batch, nq),
        in_specs=[lam_spec, lam_spec, lam_spec, lam_spec,
                  pl.BlockSpec((dv, tq), lambda h, b, i: (0, 0)),
                  pl.BlockSpec((t, dv), lambda h, b, i: (b * nq + i, h)),
                  kv_spec, kv_spec,
                  pl.BlockSpec((None, 2 * tq, 2 * tq), lambda h, b, i: (h, 0, 0))],
        out_specs=pl.BlockSpec((t, dv), lambda h, b, i: (b * nq + i, h)),
        out_shape=jax.ShapeDtypeStruct(q.shape, BF16),
        scratch_shapes=[pltpu.VMEM((seq + tq, dv), BF16),
                        pltpu.VMEM((seq // tq + 1, dv, tq), BF16),
                        pltpu.VMEM((1, 2 * t), F32), pltpu.VMEM((1, 2 * t), F32),
                        pltpu.VMEM((dv, 2 * t), F32)],
        compiler_params=pltpu.CompilerParams(
            dimension_semantics=("arbitrary", "arbitrary", "arbitrary"), vmem_limit_bytes=vmem),
        name="attn_prompt",
    )(*lams, g_cols, q, k, v, bias)


def _decode_kernel(pt_ref, lq1_ref, lk1_ref, lq2_ref, lk2_ref, g_ref, tab_ref, q_ref, kn_ref,
                   vn_ref, *rest, n_pages_step, n_heads, dv, lam_init):
    del pt_ref
    p_n = n_pages_step
    k_refs = rest[:p_n]
    v_refs = rest[p_n:2 * p_n]
    o_ref, qm_scr, bias_scr, m_scr, l_scr, acc_scr = rest[2 * p_n:]
    c = pl.program_id(1)
    n_c = pl.num_programs(1)
    rows = 2 * n_heads
    page_cols = PAGE_SIZE * n_heads

    @pl.when(c == 0)
    def _():
        q = q_ref[...].astype(F32)
        lane = lax.broadcasted_iota(jnp.int32, q.shape, 1)
        qm_scr[...] = jnp.concatenate(
            [jnp.where(lane < dv // 2, q, 0.0), jnp.where(lane >= dv // 2, q, 0.0)],
            axis=0).astype(BF16)
        key_row = lax.broadcasted_iota(jnp.int32, (1, page_cols), 1) // n_heads
        bucket = _bucket(PAGE_SIZE - key_row)
        val = jnp.zeros((rows, page_cols), F32)
        for b in range(NUM_BUCKETS - 1):
            val = jnp.where(bucket == b, tab_ref[:, b:b + 1], val)
        bias_scr[...] = val
        m_scr[...] = jnp.full(m_scr.shape, NEG, F32)
        l_scr[...] = jnp.zeros(l_scr.shape, F32)
        acc_scr[...] = jnp.zeros(acc_scr.shape, F32)

    qm = qm_scr[...]
    row = lax.broadcasted_iota(jnp.int32, (rows, page_cols), 0)
    col = lax.broadcasted_iota(jnp.int32, (rows, page_cols), 1)
    own = (col % n_heads) == (row % n_heads)
    near = jnp.where(c == n_c - 1, bias_scr[...], 0.0)
    pieces = []
    for j in range(p_n):
        kf = k_refs[j][...].reshape(page_cols, dv).astype(BF16)
        sj = lax.dot_general(qm, kf, (((1,), (1,)), ((), ())), preferred_element_type=F32)
        if j == p_n - 1:
            sj = sj + near
        pieces.append(jnp.where(own, sj, NEG))
    s = jnp.concatenate(pieces, axis=1)

    def update(s, pv_fn):
        m_prev = m_scr[...]
        m_new = jnp.maximum(m_prev, jnp.max(s, axis=-1, keepdims=True))
        p = jnp.exp2(s - m_new)
        alpha = jnp.exp2(m_prev - m_new)
        l_scr[...] = alpha * l_scr[...] + jnp.sum(p, axis=-1, keepdims=True)
        acc_scr[...] = alpha * acc_scr[...] + pv_fn(p)
        m_scr[...] = m_new

    def pv_pages(p):
        pb = p.astype(BF16)
        out = None
        for j in range(p_n):
            vf = v_refs[j][...].reshape(page_cols, dv).astype(BF16)
            term = jnp.dot(pb[:, j * page_cols:(j + 1) * page_cols], vf,
                           preferred_element_type=F32)
            out = term if out is None else out + term
        return out

    update(s, pv_pages)

    @pl.when(c == n_c - 1)
    def _():
        kn = jnp.concatenate([kn_ref[...]] * 2, axis=0)
        vn = jnp.concatenate([vn_ref[...]] * 2, axis=0)
        s_self = jnp.sum(qm.astype(F32) * kn, axis=-1, keepdims=True) + tab_ref[:, 0:1]
        update(s_self, lambda p: p * vn)
        o = acc_scr[...] / l_scr[...]
        lam = _lambda(lq1_ref, lk1_ref, lq2_ref, lk2_ref, lam_init)
        w = o[0:n_heads] - lam * o[n_heads:rows]
        o_ref[...] = _rms(w, g_ref[...], SUBLN_EPS) * (1.0 - lam_init)


def _attn_decode(q, k_new, v_new, cache_k, cache_v, page_table, tab, lams, subln_g, *,
                 n_pages_step, lam_init):
    db, n_heads, dv = q.shape
    n_pages = page_table.shape[1]
    p_n = n_pages_step
    rows = 2 * n_heads
    page_cols = PAGE_SIZE * n_heads
    assert n_pages % p_n == 0 and PAGE_SIZE >= MAX_DISTANCE
    small = lambda shape: pl.BlockSpec(shape, lambda b, c, pt: (0,) * len(shape))
    row_spec = pl.BlockSpec((None, n_heads, dv), lambda b, c, pt: (b, 0, 0))

    def page_spec(j):
        return pl.BlockSpec((None, PAGE_SIZE, n_heads, dv),
                            lambda b, c, pt: (pt[b, c * p_n + j], 0, 0, 0))

    vmem = _vmem_limit(2 * 2 * p_n * page_cols * dv * 4,
                       2 * p_n * page_cols * dv * 2,
                       6 * rows * p_n * page_cols * 4)
    grid_spec = pltpu.PrefetchScalarGridSpec(
        num_scalar_prefetch=1,
        grid=(db, n_pages // p_n),
        in_specs=[small((1, dv // 2))] * 4 + [small((1, dv)), small((rows, NUM_BUCKETS)),
                                              row_spec, row_spec, row_spec]
        + [page_spec(j) for j in range(p_n)] * 2,
        out_specs=pl.BlockSpec((None, n_heads, dv), lambda b, c, pt: (b, 0, 0)),
        scratch_shapes=[pltpu.VMEM((rows, dv), BF16), pltpu.VMEM((rows, page_cols), F32),
                        pltpu.VMEM((rows, 1), F32), pltpu.VMEM((rows, 1), F32),
                        pltpu.VMEM((rows, dv), F32)],
    )
    return pl.pallas_call(
        functools.partial(_decode_kernel, n_pages_step=p_n, n_heads=n_heads, dv=dv,
                          lam_init=lam_init),
        grid_spec=grid_spec,
        out_shape=jax.ShapeDtypeStruct((db, n_heads, dv), F32),
        compiler_params=pltpu.CompilerParams(
            dimension_semantics=("arbitrary", "arbitrary"), vmem_limit_bytes=vmem),
        name="attn_decode",
    )(page_table, *lams, subln_g, tab, q, k_new, v_new,
      *([cache_k] * p_n), *([cache_v] * p_n))


def _pool_project(diffs, pw_ref, ps_ref):
    outs = [jnp.dot(d.astype(BF16), pw_ref[g], preferred_element_type=F32)
            for g, d in enumerate(diffs)]
    return (jnp.concatenate(outs, axis=1) * ps_ref[...]).astype(BF16)


def _mix_kernel(att_ref, xp_ref, halo_ref, pw_ref, ps_ref, wo_ref, y_ref, o_ref, full_scr,
                *, tm, tiles_per_seq):
    t_in_seq = lax.rem(pl.program_id(0), tiles_per_seq)
    att_w = att_ref.shape[1]
    gc = xp_ref.shape[1] // len(POOL_WINDOWS)
    full_scr[0:POOL_HALO, :] = jnp.where(t_in_seq == 0, 0.0, halo_ref[...])
    full_scr[POOL_HALO:, :] = xp_ref[...]
    pos = t_in_seq * tm + lax.broadcasted_iota(jnp.int32, (tm, 1), 0)
    diffs = []
    for g, win in enumerate(POOL_WINDOWS):
        cols = slice(g * gc, (g + 1) * gc)
        x = xp_ref[:, cols]
        acc = x
        for d in range(1, win):
            acc = acc + full_scr[POOL_HALO - d:POOL_HALO - d + tm, cols]
        cnt = jnp.minimum(pos + 1, win).astype(F32)
        diffs.append(acc / cnt - x)
    pool = _pool_project(diffs, pw_ref, ps_ref)
    o_ref[...] = (y_ref[...]
                  + jnp.dot(att_ref[...], wo_ref[0:att_w, :], preferred_element_type=F32)
                  + jnp.dot(pool, wo_ref[att_w:, :], preferred_element_type=F32))


def _mix_prompt(att, xp, pool_w, pool_scale, w_out, y, *, seq, tm):
    m, d = y.shape
    aw, pw = att.shape[1], xp.shape[1]
    n_g, gc = pool_w.shape[0], pool_w.shape[1]
    tiles_per_seq = seq // tm
    halo_blocks = tm // POOL_HALO
    vmem = _vmem_limit(2 * tm * (aw * 2 + pw * 4 + 2 * d * 4), 2 * POOL_HALO * pw * 4,
                       2 * (aw + pw) * d * 2, 2 * n_g * gc * gc * 2,
                       (tm + POOL_HALO) * pw * 4, 4 * tm * pw * 4)
    return pl.pallas_call(
        functools.partial(_mix_kernel, tm=tm, tiles_per_seq=tiles_per_seq),
        grid=(m // tm,),
        in_specs=[
            pl.BlockSpec((tm, aw), lambda i: (i, 0)),
            pl.BlockSpec((tm, pw), lambda i: (i, 0)),
            pl.BlockSpec((POOL_HALO, pw), lambda i: (jnp.maximum(i * halo_blocks - 1, 0), 0)),
            pl.BlockSpec((n_g, gc, gc), lambda i: (0, 0, 0)),
            pl.BlockSpec((1, pw), lambda i: (0, 0)),
            pl.BlockSpec((aw + pw, d), lambda i: (0, 0)),
            pl.BlockSpec((tm, d), lambda i: (i, 0)),
        ],
        out_specs=pl.BlockSpec((tm, d), lambda i: (i, 0)),
        out_shape=jax.ShapeDtypeStruct((m, d), F32),
        scratch_shapes=[pltpu.VMEM((tm + POOL_HALO, pw), F32)],
        compiler_params=pltpu.CompilerParams(
            dimension_semantics=("arbitrary",), vmem_limit_bytes=vmem),
        name="mix_prompt",
    )(att, xp, xp, pool_w, pool_scale, w_out, y)


def _mix_sample_kernel(att_ref, xp_ref, st_ref, pw_ref, ps_ref, wo_ref, y_ref, o_ref, *, pos0):
    att_w = att_ref.shape[1]
    gc = xp_ref.shape[1] // len(POOL_WINDOWS)
    row = lax.broadcasted_iota(jnp.int32, (1, POOL_HALO, 1), 1)
    diffs = []
    for g, win in enumerate(POOL_WINDOWS):
        cols = slice(g * gc, (g + 1) * gc)
        x = xp_ref[:, cols]
        prev = jnp.where(row >= POOL_HALO - (win - 1), st_ref[:, :, cols], 0.0)
        acc = x + jnp.sum(prev, axis=1)
        diffs.append(acc / float(min(pos0 + 1, win)) - x)
    pool = _pool_project(diffs, pw_ref, ps_ref)
    o_ref[...] = (y_ref[...]
                  + jnp.dot(att_ref[...].astype(BF16), wo_ref[0:att_w, :],
                            preferred_element_type=F32)
                  + jnp.dot(pool, wo_ref[att_w:, :], preferred_element_type=F32))


def _mix_sample(att, xp, state16, pool_w, pool_scale, w_out, y, *, pos0):
    whole = lambda a: pl.BlockSpec(a.shape, lambda: (0,) * a.ndim)
    args = (att, xp, state16, pool_w, pool_scale, w_out, y)
    return pl.pallas_call(
        functools.partial(_mix_sample_kernel, pos0=pos0),
        in_specs=[whole(a) for a in args],
        out_specs=whole(y),
        out_shape=jax.ShapeDtypeStruct(y.shape, F32),
        compiler_params=pltpu.CompilerParams(
            vmem_limit_bytes=_vmem_limit(*(2 * a.size * a.dtype.itemsize for a in args),
                                         4 * state16.size * 4)),
        name="mix_sample",
    )(*args)


def kernel(x_prompt, x_sample, cache_k, cache_v, state_pool, page_table, w_in, w_out, ffn1_norm,
           ffn1_gate, ffn1_up, ffn1_down, mix_norm, ffn2_norm, ffn2_gate, ffn2_up, ffn2_down,
           lambda_q1, lambda_k1, lambda_q2, lambda_k2, subln_g, pool_w, pool_scale, rel_bias,
           final_norm):
    batch, seq, d = x_prompt.shape
    db, sd, _ = x_sample.shape
    depth, n_phys, page, n_heads, dv = cache_v.shape
    dk = dv // 2
    width = n_heads * dv
    pool_width = state_pool.shape[-1]
    assert sd == 1 and page == PAGE_SIZE and cache_k.shape[-1] == 2 * dk
    assert w_in.shape[-1] == 4 * width and pool_width == width
    assert state_pool.shape[2] == POOL_HALO - 1 and rel_bias.shape[0] == NUM_BUCKETS
    n_past = page_table.shape[1] * PAGE_SIZE
    scale = dk ** -0.5 * LOG2E

    tm_p = min(512, batch * seq)
    tm_ffn = min(1024, batch * seq)
    tf = 512
    t_att = min(512, seq)
    tq_att = 128
    assert seq % tm_p == 0 and seq % t_att == 0

    row2 = lambda a: a.reshape(1, -1)
    yp = x_prompt.reshape(batch * seq, d)
    ys = x_sample.reshape(db * sd, d)
    bias = _bias_tiles(rel_bias, tq=tq_att)
    far = rel_bias[NUM_BUCKETS - 1]
    tab_dec = jnp.tile((rel_bias - far[None, :]).T * LOG2E, (2, 1))
    outs = {k: [] for k in ("kp", "vp", "pp", "ks", "vs", "ps")}

    for l in range(depth):
        lam_init = 0.8 - 0.6 * math.exp(-0.3 * l)
        lams = tuple(row2(a[l]) for a in (lambda_q1, lambda_k1, lambda_q2, lambda_k2))
        ffn_w = []
        for gate, up, down in ((ffn1_gate, ffn1_up, ffn1_down), (ffn2_gate, ffn2_up, ffn2_down)):
            ffn_w.append((gate[l].astype(BF16), up[l].astype(BF16), down[l].astype(BF16)))
        w_in_b = w_in[l].astype(BF16)
        w_out_b = w_out[l].astype(BF16)
        pool_w_b = pool_w[l].astype(BF16)
        pool_s = row2(pool_scale[l])
        last = l == depth - 1
        fin = row2(final_norm)

        yp = _ffn(yp, row2(ffn1_norm[l]), *ffn_w[0], fin, tm=tm_ffn, tf=tf, final=False,
                  name="ffn1_prompt")
        q, k, v, xp = _proj(yp, row2(mix_norm[l]), w_in_b, tm=tm_p, scale=scale,
                            name="proj_prompt")
        att = _attn_prompt(q, k, v, bias, lams, row2(subln_g[l]), batch=batch, seq=seq,
                           n_heads=n_heads, t=t_att, tq=tq_att, lam_init=lam_init)
        yp = _mix_prompt(att, xp, pool_w_b, pool_s, w_out_b, yp, seq=seq, tm=tm_p)
        yp = _ffn(yp, row2(ffn2_norm[l]), *ffn_w[1], fin, tm=tm_ffn, tf=tf, final=last,
                  name="ffn2_prompt")
        outs["kp"].append(k.reshape(batch, seq, n_heads, dv))
        outs["vp"].append(v.reshape(batch, seq, n_heads, dv))
        outs["pp"].append(xp.reshape(batch, seq, pool_width)[:, seq - (POOL_HALO - 1):])

        ys = _ffn(ys, row2(ffn1_norm[l]), *ffn_w[0], fin, tm=db, tf=tf, final=False,
                  name="ffn1_sample")
        qs, ks, vs, xps = _proj(ys, row2(mix_norm[l]), w_in_b, tm=db, scale=scale,
                                name="proj_sample")
        att_s = _attn_decode(qs.reshape(db, n_heads, dv), ks.reshape(db, n_heads, dv),
                             vs.reshape(db, n_heads, dv), cache_k[l], cache_v[l], page_table,
                             tab_dec, lams, row2(subln_g[l]), n_pages_step=16, lam_init=lam_init)
        state16 = jnp.pad(state_pool[l], ((0, 0), (1, 0), (0, 0)))
        ys = _mix_sample(att_s.reshape(db, width), xps, state16, pool_w_b, pool_s, w_out_b, ys,
                         pos0=n_past)
        ys = _ffn(ys, row2(ffn2_norm[l]), *ffn_w[1], fin, tm=db, tf=tf, final=last,
                  name="ffn2_sample")
        outs["ks"].append(ks.reshape(db, sd, n_heads, dv))
        outs["vs"].append(vs.reshape(db, sd, n_heads, dv))
        outs["ps"].append(jnp.concatenate([state_pool[l][:, 1:], xps[:, None, :]], axis=1))

    st = lambda name: jnp.stack(outs[name], axis=0)
    return (yp.reshape(batch, seq, d), ys.reshape(db, sd, d), st("kp"), st("vp"), st("pp"),
            st("ks"), st("vs"), st("ps"))
```

```python
import functools
import math
from typing import NamedTuple

import jax
import jax.numpy as jnp
from jax import lax
from jax.experimental import pallas as pl
from jax.experimental.pallas import tpu as pltpu

F32 = jnp.float32
BF16 = jnp.bfloat16

V7X_LANES = 128
V7X_VMEM_BYTES = 64 * 1024 * 1024
V7X_VMEM_CAP = V7X_VMEM_BYTES - 8 * 1024 * 1024

EPS = 1e-6
SUBLN_EPS = 1e-5
NEG = -1e30
POOL_WINDOWS = (2, 4, 8, 16)
POOL_HALO = 16
NUM_BUCKETS = 32
MAX_EXACT = NUM_BUCKETS // 2
MAX_DISTANCE = 128
LOG2E = math.log2(math.e)
PAGE_SIZE = 128


def _vmem_limit(*nbytes):
    est = int(sum(nbytes))
    return min(max(est, 16 * 1024 * 1024), V7X_VMEM_CAP)


def _rms(x, g, eps):
    return x * lax.rsqrt(jnp.mean(x * x, axis=-1, keepdims=True) + eps) * g


def _bucket(n):
    n = jnp.maximum(n, 0)
    nf = jnp.maximum(n, 1).astype(F32)
    large = MAX_EXACT + (jnp.log(nf / MAX_EXACT) / math.log(MAX_DISTANCE / MAX_EXACT)
                         * (NUM_BUCKETS - MAX_EXACT)).astype(jnp.int32)
    large = jnp.minimum(large, NUM_BUCKETS - 1)
    return jnp.where(n < MAX_EXACT, n, large)


def _ffn_kernel(x_ref, g_ref, wg_ref, wu_ref, wd_ref, fg_ref, o_ref, h_scr, *, n_f, last_tf,
                final):
    f = pl.program_id(1)
    tf = wg_ref.shape[1]

    @pl.when(f == 0)
    def _():
        h_scr[...] = _rms(x_ref[...], g_ref[...], EPS).astype(BF16)
        o_ref[...] = jnp.zeros(o_ref.shape, F32)

    def accumulate(cols):
        h = h_scr[...]
        gate = jnp.dot(h, wg_ref[:, 0:cols], preferred_element_type=F32)
        up = jnp.dot(h, wu_ref[:, 0:cols], preferred_element_type=F32)
        act = (gate / (1.0 + jnp.exp(-gate)) * up).astype(BF16)
        o_ref[...] += jnp.dot(act, wd_ref[0:cols, :], preferred_element_type=F32)

    if last_tf == tf:
        accumulate(tf)
    else:
        @pl.when(f < n_f - 1)
        def _():
            accumulate(tf)

        @pl.when(f == n_f - 1)
        def _():
            accumulate(last_tf)

    @pl.when(f == n_f - 1)
    def _():
        y = x_ref[...] + 0.5 * o_ref[...]
        if final:
            y = _rms(y, fg_ref[...], EPS)
        o_ref[...] = y


def _ffn(x, norm_g, wg, wu, wd, final_g, *, tm, tf, final, name):
    m, d = x.shape
    f_dim = wg.shape[1]
    n_f = pl.cdiv(f_dim, tf)
    last_tf = f_dim - (n_f - 1) * tf
    assert last_tf % V7X_LANES == 0
    vmem = _vmem_limit(2 * 2 * tm * d * 4,
                       2 * 3 * d * tf * 2,
                       tm * d * 2,
                       4 * tm * tf * 4 + tm * d * 4)
    return pl.pallas_call(
        functools.partial(_ffn_kernel, n_f=n_f, last_tf=last_tf, final=final),
        grid=(m // tm, n_f),
        in_specs=[
            pl.BlockSpec((tm, d), lambda i, f: (i, 0)),
            pl.BlockSpec((1, d), lambda i, f: (0, 0)),
            pl.BlockSpec((d, tf), lambda i, f: (0, f)),
            pl.BlockSpec((d, tf), lambda i, f: (0, f)),
            pl.BlockSpec((tf, d), lambda i, f: (f, 0)),
            pl.BlockSpec((1, d), lambda i, f: (0, 0)),
        ],
        out_specs=pl.BlockSpec((tm, d), lambda i, f: (i, 0)),
        out_shape=jax.ShapeDtypeStruct((m, d), F32),
        scratch_shapes=[pltpu.VMEM((tm, d), BF16)],
        compiler_params=pltpu.CompilerParams(
            dimension_semantics=("arbitrary", "arbitrary"), vmem_limit_bytes=vmem),
        name=name,
    )(x, norm_g, wg, wu, wd, final_g)


def _proj_kernel(x_ref, g_ref, w_ref, q_ref, k_ref, v_ref, p_ref, h_scr, *, scale):
    j = pl.program_id(1)

    @pl.when(j == 0)
    def _():
        h_scr[...] = _rms(x_ref[...], g_ref[...], EPS).astype(BF16)

    p = jnp.dot(h_scr[...], w_ref[...], preferred_element_type=F32)

    @pl.when(j == 0)
    def _():
        q_ref[...] = (p * scale).astype(BF16)

    @pl.when(j == 1)
    def _():
        k_ref[...] = p

    @pl.when(j == 2)
    def _():
        v_ref[...] = p

    @pl.when(j == 3)
    def _():
        p_ref[...] = p


def _proj(x, norm_g, w_in, *, tm, scale, name):
    m, d = x.shape
    w = w_in.shape[1] // 4
    out_spec = pl.BlockSpec((tm, w), lambda i, j: (i, 0))
    vmem = _vmem_limit(2 * tm * d * 4, 2 * d * w * 2, 2 * tm * w * (2 + 4 + 4 + 4),
                       tm * d * 2, 2 * tm * w * 4)
    return pl.pallas_call(
        functools.partial(_proj_kernel, scale=scale),
        grid=(m // tm, 4),
        in_specs=[
            pl.BlockSpec((tm, d), lambda i, j: (i, 0)),
            pl.BlockSpec((1, d), lambda i, j: (0, 0)),
            pl.BlockSpec((d, w), lambda i, j: (0, j)),
        ],
        out_specs=[out_spec, out_spec, out_spec, out_spec],
        out_shape=[jax.ShapeDtypeStruct((m, w), BF16)] + [jax.ShapeDtypeStruct((m, w), F32)] * 3,
        scratch_shapes=[pltpu.VMEM((tm, d), BF16)],
        compiler_params=pltpu.CompilerParams(
            dimension_semantics=("arbitrary", "arbitrary"), vmem_limit_bytes=vmem),
        name=name,
    )(x, norm_g, w_in)


def _bias_tile_kernel(tab_ref, o_ref, *, tq):
    h = pl.program_id(0)
    r = lax.broadcasted_iota(jnp.int32, (2 * tq, 2 * tq), 0)
    c = lax.broadcasted_iota(jnp.int32, (2 * tq, 2 * tq), 1)
    key = r - tq
    qry = c % tq
    far = tab_ref[h * NUM_BUCKETS + NUM_BUCKETS - 1]
    bucket = _bucket(qry - key)
    val = jnp.zeros(r.shape, F32)
    for b in range(NUM_BUCKETS - 1):
        val = jnp.where(bucket == b, (tab_ref[h * NUM_BUCKETS + b] - far) * LOG2E, val)
    o_ref[...] = jnp.where(key <= qry, val, NEG)


def _bias_tiles(rel_bias, *, tq):
    n_heads = rel_bias.shape[1]
    tab = rel_bias.T.reshape(-1)
    return pl.pallas_call(
        functools.partial(_bias_tile_kernel, tq=tq),
        grid=(n_heads,),
        in_specs=[pl.BlockSpec(memory_space=pltpu.SMEM)],
        out_specs=pl.BlockSpec((None, 2 * tq, 2 * tq), lambda h: (h, 0, 0)),
        out_shape=jax.ShapeDtypeStruct((n_heads, 2 * tq, 2 * tq), F32),
        compiler_params=pltpu.CompilerParams(
            dimension_semantics=("arbitrary",),
            vmem_limit_bytes=_vmem_limit(16 * 2 * tq * 2 * tq * 4)),
        name="rel_bias_tiles",
    )(tab)


def _lambda(lq1_ref, lk1_ref, lq2_ref, lk2_ref, lam_init):
    a = jnp.sum(lq1_ref[...] * lk1_ref[...], axis=-1, keepdims=True)
    b = jnp.sum(lq2_ref[...] * lk2_ref[...], axis=-1, keepdims=True)
    return jnp.exp(a) - jnp.exp(b) + lam_init


def _attn_body(qi, lq1_ref, lk1_ref, lq2_ref, lk2_ref, g_ref, q_ref, k_ref, v_ref, bias_ref,
               o_ref, kb_scr, vt_scr, m_scr, l_scr, acc_scr, *, t, tq, dk, lam_init):
    n_strips = t // tq
    seq = k_ref.shape[0]
    sw = 2 * tq

    @pl.when(qi == 0)
    def _():
        kb_scr[0:tq, :] = jnp.zeros((tq, kb_scr.shape[1]), BF16)
        vt_scr[0] = jnp.zeros(vt_scr.shape[1:], BF16)
        kb_scr[tq:, :] = k_ref[...].astype(BF16)
        for c in range(seq // tq):
            vt_scr[c + 1] = v_ref[c * tq:(c + 1) * tq, :].T.astype(BF16)

    qa = q_ref[...].astype(F32)
    lane = lax.broadcasted_iota(jnp.int32, qa.shape, 1)
    q1 = jnp.where(lane < dk, qa, 0.0).astype(BF16)
    q2 = jnp.where(lane >= dk, qa, 0.0).astype(BF16)
    strips = [jnp.concatenate([q1[u * tq:(u + 1) * tq], q2[u * tq:(u + 1) * tq]], axis=0)
              for u in range(n_strips)]
    qs_all = jnp.concatenate(strips, axis=0)

    m_scr[...] = jnp.full(m_scr.shape, NEG, F32)
    l_scr[...] = jnp.zeros(l_scr.shape, F32)
    acc_scr[...] = jnp.zeros(acc_scr.shape, F32)

    def keys(row0, n):
        return kb_scr[pl.ds(pl.multiple_of(row0, tq), n), :]

    def values_t(chunk0, n_chunks):
        return jnp.concatenate([vt_scr[chunk0 + i] for i in range(n_chunks)], axis=1)

    def fold(lanes, pieces):
        m_prev = m_scr[:, lanes]
        m_new = m_prev
        for s, _ in pieces:
            m_new = jnp.maximum(m_new, jnp.max(s, axis=0, keepdims=True))
        alpha = jnp.exp2(m_prev - m_new)
        l_new = alpha * l_scr[:, lanes]
        pv = None
        for s, vt in pieces:
            p = jnp.exp2(s - m_new)
            l_new = l_new + jnp.sum(p, axis=0, keepdims=True)
            term = jnp.dot(vt, p.astype(BF16), preferred_element_type=F32)
            pv = term if pv is None else pv + term
        l_scr[:, lanes] = l_new
        acc_scr[:, lanes] = alpha * acc_scr[:, lanes] + pv
        m_scr[:, lanes] = m_new

    def scores(kc, qs):
        return lax.dot_general(kc, qs, (((1,), (1,)), ((), ())), preferred_element_type=F32)

    far_chunks = t // tq
    every = slice(0, n_strips * sw)

    def fold_far(blocks):
        scored = []
        for chunk0, n_chunks in blocks:
            kc = keys(chunk0 * tq, n_chunks * tq)
            scored.append((scores(kc, qs_all), values_t(chunk0, n_chunks)))
        for piece in scored:
            fold(every, [piece])

    n_far_blocks = jnp.maximum(qi - 1, 0)
    n_pairs = n_far_blocks // 2

    def far_body(j, carry):
        c0 = 1 + j * 2 * far_chunks
        fold_far([(c0, far_chunks), (c0 + far_chunks, far_chunks)])
        return carry

    lax.fori_loop(0, n_pairs, far_body, 0)
    tail0 = 1 + n_pairs * 2 * far_chunks

    @pl.when((qi > 0) & (n_far_blocks % 2 == 0))
    def _():
        fold_far([(tail0, far_chunks - 1)])

    @pl.when(n_far_blocks % 2 == 1)
    def _():
        fold_far([(tail0, far_chunks), (tail0 + far_chunks, far_chunks - 1)])

    n_absent = jnp.where(qi == 0, tq, 0)
    pending = []
    near_ahead = 1
    for u in range(n_strips):
        lanes = slice(u * sw, (u + 1) * sw)
        qs = strips[u]
        pieces = []
        if u > 0:
            s_a = scores(keys(qi * t, u * tq), qs)
            row = lax.broadcasted_iota(jnp.int32, s_a.shape, 0)
            s_a = jnp.where(row < n_absent, NEG, s_a)
            pieces.append((s_a, values_t(qi * far_chunks, u)))
        bias = bias_ref[...]
        if u == 0:
            row = lax.broadcasted_iota(jnp.int32, bias.shape, 0)
            bias = jnp.where(row < n_absent, NEG, bias)
        s_b = scores(keys(qi * t + u * tq, 2 * tq), qs) + bias
        pieces.append((s_b, values_t(qi * far_chunks + u, 2)))
        pending.append((lanes, pieces))
        if len(pending) > near_ahead:
            fold(*pending.pop(0))
    for item in pending:
        fold(*item)

    o = acc_scr[...] / l_scr[...]
    lam = _lambda(lq1_ref, lk1_ref, lq2_ref, lk2_ref, lam_init)
    for u in range(n_strips):
        w = o[:, u * sw:u * sw + tq] - lam * o[:, u * sw + tq:(u + 1) * sw]
        y = (w * lax.rsqrt(jnp.mean(w * w, axis=0, keepdims=True) + SUBLN_EPS)
             * g_ref[...] * (1.0 - lam_init))
        o_ref[u * tq:(u + 1) * tq, :] = y.T.astype(o_ref.dtype)


class _Operands(NamedTuple):
    args: tuple
    in_specs: list
    out_spec: pl.BlockSpec
    out_shape: jax.ShapeDtypeStruct
    scratch: list
    vmem_bytes: int


def _attn_prompt_operands(q, k, v, bias, lams, subln_g, *, seq, n_heads, t, tq, index):
    dv = q.shape[1] // n_heads
    nq = seq // t
    assert t % tq == 0 and tq >= MAX_DISTANCE
    at = lambda f: (lambda *g: f(*index(*g)))
    g_cols = jnp.broadcast_to(subln_g.reshape(dv, 1), (dv, tq))
    lam_spec = pl.BlockSpec((1, dv // 2), at(lambda h, b, i: (0, 0)))
    kv_spec = pl.BlockSpec((seq, dv), at(lambda h, b, i: (b, h)))
    tile_spec = pl.BlockSpec((t, dv), at(lambda h, b, i: (b * nq + i, h)))
    return _Operands(
        args=(*lams, g_cols, q, k, v, bias),
        in_specs=[lam_spec, lam_spec, lam_spec, lam_spec,
                  pl.BlockSpec((dv, tq), at(lambda h, b, i: (0, 0))),
                  tile_spec, kv_spec, kv_spec,
                  pl.BlockSpec((None, 2 * tq, 2 * tq), at(lambda h, b, i: (h, 0, 0)))],
        out_spec=tile_spec,
        out_shape=jax.ShapeDtypeStruct(q.shape, BF16),
        scratch=[pltpu.VMEM((seq + tq, dv), BF16), pltpu.VMEM((seq // tq + 1, dv, tq), BF16),
                 pltpu.VMEM((1, 2 * t), F32), pltpu.VMEM((1, 2 * t), F32),
                 pltpu.VMEM((dv, 2 * t), F32)],
        vmem_bytes=(2 * 2 * seq * dv * 4 + 2 * (seq + tq) * dv * 2 + 2 * 2 * tq * 2 * tq * 4
                    + 4 * t * dv * 2 + dv * 2 * t * 4 + 6 * t * 2 * t * 4))


def _decode_body(c, n_c, lq1_ref, lk1_ref, lq2_ref, lk2_ref, g_ref, tab_ref, q_ref, kn_ref,
                 vn_ref, *rest, n_pages_step, n_heads, dv, lam_init):
    p_n = n_pages_step
    k_refs = rest[:p_n]
    v_refs = rest[p_n:2 * p_n]
    o_ref, qm_scr, bias_scr, m_scr, l_scr, acc_scr = rest[2 * p_n:]
    rows = 2 * n_heads
    page_cols = PAGE_SIZE * n_heads

    @pl.when(c == 0)
    def _():
        q = q_ref[...].astype(F32)
        lane = lax.broadcasted_iota(jnp.int32, q.shape, 1)
        qm_scr[...] = jnp.concatenate(
            [jnp.where(lane < dv // 2, q, 0.0), jnp.where(lane >= dv // 2, q, 0.0)],
            axis=0).astype(BF16)
        key_row = lax.broadcasted_iota(jnp.int32, (1, page_cols), 1) // n_heads
        bucket = _bucket(PAGE_SIZE - key_row)
        val = jnp.zeros((rows, page_cols), F32)
        for b in range(NUM_BUCKETS - 1):
            val = jnp.where(bucket == b, tab_ref[:, b:b + 1], val)
        bias_scr[...] = val
        m_scr[...] = jnp.full(m_scr.shape, NEG, F32)
        l_scr[...] = jnp.zeros(l_scr.shape, F32)
        acc_scr[...] = jnp.zeros(acc_scr.shape, F32)

    qm = qm_scr[...]
    row = lax.broadcasted_iota(jnp.int32, (rows, page_cols), 0)
    col = lax.broadcasted_iota(jnp.int32, (rows, page_cols), 1)
    own = (col % n_heads) == (row % n_heads)
    near = jnp.where(c == n_c - 1, bias_scr[...], 0.0)
    pieces = []
    for j in range(p_n):
        kf = k_refs[j][...].reshape(page_cols, dv).astype(BF16)
        sj = lax.dot_general(qm, kf, (((1,), (1,)), ((), ())), preferred_element_type=F32)
        if j == p_n - 1:
            sj = sj + near
        pieces.append(jnp.where(own, sj, NEG))
    s = jnp.concatenate(pieces, axis=1)

    def update(s, pv_fn):
        m_prev = m_scr[...]
        m_new = jnp.maximum(m_prev, jnp.max(s, axis=-1, keepdims=True))
        p = jnp.exp2(s - m_new)
        alpha = jnp.exp2(m_prev - m_new)
        l_scr[...] = alpha * l_scr[...] + jnp.sum(p, axis=-1, keepdims=True)
        acc_scr[...] = alpha * acc_scr[...] + pv_fn(p)
        m_scr[...] = m_new

    def pv_pages(p):
        pb = p.astype(BF16)
        out = None
        for j in range(p_n):
            vf = v_refs[j][...].reshape(page_cols, dv).astype(BF16)
            term = jnp.dot(pb[:, j * page_cols:(j + 1) * page_cols], vf,
                           preferred_element_type=F32)
            out = term if out is None else out + term
        return out

    update(s, pv_pages)

    @pl.when(c == n_c - 1)
    def _():
        kn = jnp.concatenate([kn_ref[...]] * 2, axis=0)
        vn = jnp.concatenate([vn_ref[...]] * 2, axis=0)
        s_self = jnp.sum(qm.astype(F32) * kn, axis=-1, keepdims=True) + tab_ref[:, 0:1]
        update(s_self, lambda p: p * vn)
        o = acc_scr[...] / l_scr[...]
        lam = _lambda(lq1_ref, lk1_ref, lq2_ref, lk2_ref, lam_init)
        w = o[0:n_heads] - lam * o[n_heads:rows]
        o_ref[...] = _rms(w, g_ref[...], SUBLN_EPS) * (1.0 - lam_init)


def _attn_decode_operands(q, k_new, v_new, cache_k, cache_v, tab, lams, subln_g, *,
                          n_pages_step, index):
    db, n_heads, dv = q.shape
    p_n = n_pages_step
    rows = 2 * n_heads
    page_cols = PAGE_SIZE * n_heads
    assert PAGE_SIZE >= MAX_DISTANCE
    at = lambda f: (lambda *g: f(*index(*g)))
    small = lambda shape: pl.BlockSpec(shape, at(lambda s, c, pt: (0,) * len(shape)))
    row_spec = pl.BlockSpec((None, n_heads, dv), at(lambda s, c, pt: (s, 0, 0)))

    def page_spec(j):
        return pl.BlockSpec((None, PAGE_SIZE, n_heads, dv),
                            at(lambda s, c, pt: (pt[s, c * p_n + j], 0, 0, 0)))

    return _Operands(
        args=(*lams, subln_g, tab, q, k_new, v_new, *([cache_k] * p_n), *([cache_v] * p_n)),
        in_specs=[small((1, dv // 2))] * 4 + [small((1, dv)), small((rows, NUM_BUCKETS)),
                                              row_spec, row_spec, row_spec]
        + [page_spec(j) for j in range(p_n)] * 2,
        out_spec=row_spec,
        out_shape=jax.ShapeDtypeStruct((db, n_heads, dv), F32),
        scratch=[pltpu.VMEM((rows, dv), BF16), pltpu.VMEM((rows, page_cols), F32),
                 pltpu.VMEM((rows, 1), F32), pltpu.VMEM((rows, 1), F32),
                 pltpu.VMEM((rows, dv), F32)],
        vmem_bytes=(2 * 2 * p_n * page_cols * dv * 4
                    + 2 * 2 * page_cols * dv * 2
                    + 6 * rows * p_n * page_cols * 4))


def _attention(prompt, decode, page_table, *, grid_prompt, n_chunks, attn_kw, decode_kw):
    n_heads, batch, nq = grid_prompt

    def flat(h, b, i):
        return (h * batch + b) * nq + i

    n_seq = page_table.shape[0]
    if n_heads * batch * nq == n_seq * n_chunks:
        pr = prompt(index=lambda h, b, i, pt: (h, b, i))
        de = decode(index=lambda h, b, i, pt: (flat(h, b, i) // n_chunks,
                                               flat(h, b, i) % n_chunks, pt))
        n_pr, n_de = len(pr.args), len(de.args)

        def fused(pt_ref, *refs):
            del pt_ref
            ins, outs, scr = refs[:n_pr + n_de], refs[n_pr + n_de:n_pr + n_de + 2], \
                refs[n_pr + n_de + 2:]
            h, b, i = pl.program_id(0), pl.program_id(1), pl.program_id(2)
            _decode_body(flat(h, b, i) % n_chunks, n_chunks, *ins[n_pr:], outs[1],
                         *scr[len(pr.scratch):], **decode_kw)
            _attn_body(i, *ins[:n_pr], outs[0], *scr[:len(pr.scratch)], **attn_kw)

        return pl.pallas_call(
            fused,
            grid_spec=pltpu.PrefetchScalarGridSpec(
                num_scalar_prefetch=1, grid=grid_prompt,
                in_specs=pr.in_specs + de.in_specs, out_specs=[pr.out_spec, de.out_spec],
                scratch_shapes=pr.scratch + de.scratch),
            out_shape=[pr.out_shape, de.out_shape],
            compiler_params=pltpu.CompilerParams(
                dimension_semantics=("arbitrary",) * 3,
                vmem_limit_bytes=_vmem_limit(pr.vmem_bytes, de.vmem_bytes)),
            name="attn_prompt_decode",
        )(page_table, *pr.args, *de.args)

    pr = prompt(index=lambda h, b, i: (h, b, i))
    att = pl.pallas_call(
        lambda *refs: _attn_body(pl.program_id(2), *refs, **attn_kw),
        grid=grid_prompt, in_specs=pr.in_specs, out_specs=pr.out_spec, out_shape=pr.out_shape,
        scratch_shapes=pr.scratch,
        compiler_params=pltpu.CompilerParams(
            dimension_semantics=("arbitrary",) * 3, vmem_limit_bytes=_vmem_limit(pr.vmem_bytes)),
        name="attn_prompt",
    )(*pr.args)
    de = decode(index=lambda s, c, pt: (s, c, pt))
    att_s = pl.pallas_call(
        lambda pt_ref, *refs: _decode_body(pl.program_id(1), n_chunks, *refs, **decode_kw),
        grid_spec=pltpu.PrefetchScalarGridSpec(
            num_scalar_prefetch=1, grid=(n_seq, n_chunks), in_specs=de.in_specs,
            out_specs=de.out_spec, scratch_shapes=de.scratch),
        out_shape=de.out_shape,
        compiler_params=pltpu.CompilerParams(
            dimension_semantics=("arbitrary",) * 2, vmem_limit_bytes=_vmem_limit(de.vmem_bytes)),
        name="attn_decode",
    )(page_table, *de.args)
    return att, att_s


def _pool_project(diffs, pw_ref, ps_ref):
    outs = [jnp.dot(d.astype(BF16), pw_ref[g], preferred_element_type=F32)
            for g, d in enumerate(diffs)]
    return (jnp.concatenate(outs, axis=1) * ps_ref[...]).astype(BF16)


def _mix_kernel(att_ref, xp_ref, halo_ref, pw_ref, ps_ref, wo_ref, y_ref, o_ref, full_scr,
                *, tm, tiles_per_seq):
    t_in_seq = lax.rem(pl.program_id(0), tiles_per_seq)
    att_w = att_ref.shape[1]
    gc = xp_ref.shape[1] // len(POOL_WINDOWS)
    full_scr[0:POOL_HALO, :] = jnp.where(t_in_seq == 0, 0.0, halo_ref[...])
    full_scr[POOL_HALO:, :] = xp_ref[...]
    pos = t_in_seq * tm + lax.broadcasted_iota(jnp.int32, (tm, 1), 0)
    diffs = []
    for g, win in enumerate(POOL_WINDOWS):
        cols = slice(g * gc, (g + 1) * gc)
        x = xp_ref[:, cols]
        acc = x
        for d in range(1, win):
            acc = acc + full_scr[POOL_HALO - d:POOL_HALO - d + tm, cols]
        cnt = jnp.minimum(pos + 1, win).astype(F32)
        diffs.append(acc / cnt - x)
    pool = _pool_project(diffs, pw_ref, ps_ref)
    o_ref[...] = (y_ref[...]
                  + jnp.dot(att_ref[...], wo_ref[0:att_w, :], preferred_element_type=F32)
                  + jnp.dot(pool, wo_ref[att_w:, :], preferred_element_type=F32))


def _mix_prompt(att, xp, pool_w, pool_scale, w_out, y, *, seq, tm):
    m, d = y.shape
    aw, pw = att.shape[1], xp.shape[1]
    n_g, gc = pool_w.shape[0], pool_w.shape[1]
    tiles_per_seq = seq // tm
    halo_blocks = tm // POOL_HALO
    vmem = _vmem_limit(2 * tm * (aw * 2 + pw * 4 + 2 * d * 4), 2 * POOL_HALO * pw * 4,
                       2 * (aw + pw) * d * 2, 2 * n_g * gc * gc * 2,
                       (tm + POOL_HALO) * pw * 4, 4 * tm * pw * 4)
    return pl.pallas_call(
        functools.partial(_mix_kernel, tm=tm, tiles_per_seq=tiles_per_seq),
        grid=(m // tm,),
        in_specs=[
            pl.BlockSpec((tm, aw), lambda i: (i, 0)),
            pl.BlockSpec((tm, pw), lambda i: (i, 0)),
            pl.BlockSpec((POOL_HALO, pw), lambda i: (jnp.maximum(i * halo_blocks - 1, 0), 0)),
            pl.BlockSpec((n_g, gc, gc), lambda i: (0, 0, 0)),
            pl.BlockSpec((1, pw), lambda i: (0, 0)),
            pl.BlockSpec((aw + pw, d), lambda i: (0, 0)),
            pl.BlockSpec((tm, d), lambda i: (i, 0)),
        ],
        out_specs=pl.BlockSpec((tm, d), lambda i: (i, 0)),
        out_shape=jax.ShapeDtypeStruct((m, d), F32),
        scratch_shapes=[pltpu.VMEM((tm + POOL_HALO, pw), F32)],
        compiler_params=pltpu.CompilerParams(
            dimension_semantics=("arbitrary",), vmem_limit_bytes=vmem),
        name="mix_prompt",
    )(att, xp, xp, pool_w, pool_scale, w_out, y)


def _mix_sample_kernel(att_ref, xp_ref, st_ref, pw_ref, ps_ref, wo_ref, y_ref, o_ref, *, pos0):
    att_w = att_ref.shape[1]
    gc = xp_ref.shape[1] // len(POOL_WINDOWS)
    row = lax.broadcasted_iota(jnp.int32, (1, POOL_HALO, 1), 1)
    diffs = []
    for g, win in enumerate(POOL_WINDOWS):
        cols = slice(g * gc, (g + 1) * gc)
        x = xp_ref[:, cols]
        prev = jnp.where(row >= POOL_HALO - (win - 1), st_ref[:, :, cols], 0.0)
        acc = x + jnp.sum(prev, axis=1)
        diffs.append(acc / float(min(pos0 + 1, win)) - x)
    pool = _pool_project(diffs, pw_ref, ps_ref)
    o_ref[...] = (y_ref[...]
                  + jnp.dot(att_ref[...].astype(BF16), wo_ref[0:att_w, :],
                            preferred_element_type=F32)
                  + jnp.dot(pool, wo_ref[att_w:, :], preferred_element_type=F32))


def _mix_sample(att, xp, state16, pool_w, pool_scale, w_out, y, *, pos0):
    whole = lambda a: pl.BlockSpec(a.shape, lambda: (0,) * a.ndim)
    args = (att, xp, state16, pool_w, pool_scale, w_out, y)
    return pl.pallas_call(
        functools.partial(_mix_sample_kernel, pos0=pos0),
        in_specs=[whole(a) for a in args],
        out_specs=whole(y),
        out_shape=jax.ShapeDtypeStruct(y.shape, F32),
        compiler_params=pltpu.CompilerParams(
            vmem_limit_bytes=_vmem_limit(*(2 * a.size * a.dtype.itemsize for a in args),
                                         4 * state16.size * 4)),
        name="mix_sample",
    )(*args)


def kernel(x_prompt, x_sample, cache_k, cache_v, state_pool, page_table, w_in, w_out, ffn1_norm,
           ffn1_gate, ffn1_up, ffn1_down, mix_norm, ffn2_norm, ffn2_gate, ffn2_up, ffn2_down,
           lambda_q1, lambda_k1, lambda_q2, lambda_k2, subln_g, pool_w, pool_scale, rel_bias,
           final_norm):
    batch, seq, d = x_prompt.shape
    db, sd, _ = x_sample.shape
    depth, n_phys, page, n_heads, dv = cache_v.shape
    dk = dv // 2
    width = n_heads * dv
    pool_width = state_pool.shape[-1]
    assert sd == 1 and page == PAGE_SIZE and cache_k.shape[-1] == 2 * dk
    assert w_in.shape[-1] == 4 * width and pool_width == width
    assert state_pool.shape[2] == POOL_HALO - 1 and rel_bias.shape[0] == NUM_BUCKETS
    n_past = page_table.shape[1] * PAGE_SIZE
    scale = dk ** -0.5 * LOG2E

    tm_p = min(512, batch * seq)
    tm_ffn = min(1024, batch * seq)
    tf = 512
    t_att = min(512, seq)
    tq_att = 128
    pages_step = min(16, page_table.shape[1])
    assert seq % tm_p == 0 and seq % t_att == 0 and page_table.shape[1] % pages_step == 0

    row2 = lambda a: a.reshape(1, -1)
    yp = x_prompt.reshape(batch * seq, d)
    ys = x_sample.reshape(db * sd, d)
    bias = _bias_tiles(rel_bias, tq=tq_att)
    far = rel_bias[NUM_BUCKETS - 1]
    tab_dec = jnp.tile((rel_bias - far[None, :]).T * LOG2E, (2, 1))
    outs = {k: [] for k in ("kp", "vp", "pp", "ks", "vs", "ps")}

    for l in range(depth):
        lam_init = 0.8 - 0.6 * math.exp(-0.3 * l)
        lams = tuple(row2(a[l]) for a in (lambda_q1, lambda_k1, lambda_q2, lambda_k2))
        ffn_w = []
        for gate, up, down in ((ffn1_gate, ffn1_up, ffn1_down), (ffn2_gate, ffn2_up, ffn2_down)):
            ffn_w.append((gate[l].astype(BF16), up[l].astype(BF16), down[l].astype(BF16)))
        w_in_b = w_in[l].astype(BF16)
        w_out_b = w_out[l].astype(BF16)
        pool_w_b = pool_w[l].astype(BF16)
        pool_s = row2(pool_scale[l])
        last = l == depth - 1
        fin = row2(final_norm)

        yp = _ffn(yp, row2(ffn1_norm[l]), *ffn_w[0], fin, tm=tm_ffn, tf=tf, final=False,
                  name="ffn1_prompt")
        q, k, v, xp = _proj(yp, row2(mix_norm[l]), w_in_b, tm=tm_p, scale=scale,
                            name="proj_prompt")
        ys = _ffn(ys, row2(ffn1_norm[l]), *ffn_w[0], fin, tm=db, tf=tf, final=False,
                  name="ffn1_sample")
        qs, ks, vs, xps = _proj(ys, row2(mix_norm[l]), w_in_b, tm=db, scale=scale,
                                name="proj_sample")

        att, att_s = _attention(
            functools.partial(_attn_prompt_operands, q, k, v, bias, lams, row2(subln_g[l]),
                              seq=seq, n_heads=n_heads, t=t_att, tq=tq_att),
            functools.partial(_attn_decode_operands, qs.reshape(db, n_heads, dv),
                              ks.reshape(db, n_heads, dv), vs.reshape(db, n_heads, dv),
                              cache_k[l], cache_v[l], tab_dec, lams, row2(subln_g[l]),
                              n_pages_step=pages_step),
            page_table, grid_prompt=(n_heads, batch, seq // t_att),
            n_chunks=page_table.shape[1] // pages_step,
            attn_kw=dict(t=t_att, tq=tq_att, dk=dk, lam_init=lam_init),
            decode_kw=dict(n_pages_step=pages_step, n_heads=n_heads, dv=dv, lam_init=lam_init))

        yp = _mix_prompt(att, xp, pool_w_b, pool_s, w_out_b, yp, seq=seq, tm=tm_p)
        yp = _ffn(yp, row2(ffn2_norm[l]), *ffn_w[1], fin, tm=tm_ffn, tf=tf, final=last,
                  name="ffn2_prompt")
        outs["kp"].append(k.reshape(batch, seq, n_heads, dv))
        outs["vp"].append(v.reshape(batch, seq, n_heads, dv))
        outs["pp"].append(xp.reshape(batch, seq, pool_width)[:, seq - (POOL_HALO - 1):])

        state16 = jnp.pad(state_pool[l], ((0, 0), (1, 0), (0, 0)))
        ys = _mix_sample(att_s.reshape(db, width), xps, state16, pool_w_b, pool_s, w_out_b, ys,
                         pos0=n_past)
        ys = _ffn(ys, row2(ffn2_norm[l]), *ffn_w[1], fin, tm=db, tf=tf, final=last,
                  name="ffn2_sample")
        outs["ks"].append(ks.reshape(db, sd, n_heads, dv))
        outs["vs"].append(vs.reshape(db, sd, n_heads, dv))
        outs["ps"].append(jnp.concatenate([state_pool[l][:, 1:], xps[:, None, :]], axis=1))

    st = lambda name: jnp.stack(outs[name], axis=0)
    return (yp.reshape(batch, seq, d), ys.reshape(db, sd, d), st("kp"), st("vp"), st("pp"),
            st("ks"), st("vs"), st("ps"))
```

```python
import functools
import math
from typing import NamedTuple

import jax
import jax.numpy as jnp
from jax import lax
from jax.experimental import pallas as pl
from jax.experimental.pallas import tpu as pltpu

F32 = jnp.float32
BF16 = jnp.bfloat16

V7X_LANES = 128
V7X_VMEM_BYTES = 64 * 1024 * 1024
V7X_VMEM_CAP = V7X_VMEM_BYTES - 8 * 1024 * 1024

EPS = 1e-6
SUBLN_EPS = 1e-5
NEG = -1e30
POOL_WINDOWS = (2, 4, 8, 16)
POOL_HALO = 16
NUM_BUCKETS = 32
MAX_EXACT = NUM_BUCKETS // 2
MAX_DISTANCE = 128
LOG2E = math.log2(math.e)
PAGE_SIZE = 128
DECODE_GROUP = 4


def _vmem_limit(*nbytes):
    est = int(sum(nbytes))
    return min(max(est, 16 * 1024 * 1024), V7X_VMEM_CAP)


def _rms(x, g, eps):
    return x * lax.rsqrt(jnp.mean(x * x, axis=-1, keepdims=True) + eps) * g


def _bucket(n):
    n = jnp.maximum(n, 0)
    nf = jnp.maximum(n, 1).astype(F32)
    large = MAX_EXACT + (jnp.log(nf / MAX_EXACT) / math.log(MAX_DISTANCE / MAX_EXACT)
                         * (NUM_BUCKETS - MAX_EXACT)).astype(jnp.int32)
    large = jnp.minimum(large, NUM_BUCKETS - 1)
    return jnp.where(n < MAX_EXACT, n, large)


def _ffn_kernel(x_ref, g_ref, wg_ref, wu_ref, wd_ref, fg_ref, o_ref, h_scr, *, n_f, last_tf,
                final):
    f = pl.program_id(1)
    tf = wg_ref.shape[1]

    @pl.when(f == 0)
    def _():
        h_scr[...] = _rms(x_ref[...], g_ref[...], EPS).astype(BF16)
        o_ref[...] = jnp.zeros(o_ref.shape, F32)

    def accumulate(cols):
        h = h_scr[...]
        gate = jnp.dot(h, wg_ref[:, 0:cols], preferred_element_type=F32)
        up = jnp.dot(h, wu_ref[:, 0:cols], preferred_element_type=F32)
        act = (gate / (1.0 + jnp.exp(-gate)) * up).astype(BF16)
        o_ref[...] += jnp.dot(act, wd_ref[0:cols, :], preferred_element_type=F32)

    if last_tf == tf:
        accumulate(tf)
    else:
        @pl.when(f < n_f - 1)
        def _():
            accumulate(tf)

        @pl.when(f == n_f - 1)
        def _():
            accumulate(last_tf)

    @pl.when(f == n_f - 1)
    def _():
        y = x_ref[...] + 0.5 * o_ref[...]
        if final:
            y = _rms(y, fg_ref[...], EPS)
        o_ref[...] = y


def _ffn(x, norm_g, wg, wu, wd, final_g, *, tm, tf, final, name):
    m, d = x.shape
    f_dim = wg.shape[1]
    n_f = pl.cdiv(f_dim, tf)
    last_tf = f_dim - (n_f - 1) * tf
    assert last_tf % V7X_LANES == 0
    vmem = _vmem_limit(2 * 2 * tm * d * 4,
                       2 * 3 * d * tf * 2,
                       tm * d * 2,
                       4 * tm * tf * 4 + tm * d * 4)
    return pl.pallas_call(
        functools.partial(_ffn_kernel, n_f=n_f, last_tf=last_tf, final=final),
        grid=(m // tm, n_f),
        in_specs=[
            pl.BlockSpec((tm, d), lambda i, f: (i, 0)),
            pl.BlockSpec((1, d), lambda i, f: (0, 0)),
            pl.BlockSpec((d, tf), lambda i, f: (0, f)),
            pl.BlockSpec((d, tf), lambda i, f: (0, f)),
            pl.BlockSpec((tf, d), lambda i, f: (f, 0)),
            pl.BlockSpec((1, d), lambda i, f: (0, 0)),
        ],
        out_specs=pl.BlockSpec((tm, d), lambda i, f: (i, 0)),
        out_shape=jax.ShapeDtypeStruct((m, d), F32),
        scratch_shapes=[pltpu.VMEM((tm, d), BF16)],
        compiler_params=pltpu.CompilerParams(
            dimension_semantics=("arbitrary", "arbitrary"), vmem_limit_bytes=vmem),
        name=name,
    )(x, norm_g, wg, wu, wd, final_g)


def _proj_kernel(x_ref, g_ref, w_ref, q_ref, k_ref, v_ref, p_ref, h_scr, *, scale):
    j = pl.program_id(1)

    @pl.when(j == 0)
    def _():
        h_scr[...] = _rms(x_ref[...], g_ref[...], EPS).astype(BF16)

    p = jnp.dot(h_scr[...], w_ref[...], preferred_element_type=F32)

    @pl.when(j == 0)
    def _():
        q_ref[...] = (p * scale).astype(BF16)

    @pl.when(j == 1)
    def _():
        k_ref[...] = p

    @pl.when(j == 2)
    def _():
        v_ref[...] = p

    @pl.when(j == 3)
    def _():
        p_ref[...] = p


def _proj(x, norm_g, w_in, *, tm, scale, name):
    m, d = x.shape
    w = w_in.shape[1] // 4
    out_spec = pl.BlockSpec((tm, w), lambda i, j: (i, 0))
    vmem = _vmem_limit(2 * tm * d * 4, 2 * d * w * 2, 2 * tm * w * (2 + 4 + 4 + 4),
                       tm * d * 2, 2 * tm * w * 4)
    return pl.pallas_call(
        functools.partial(_proj_kernel, scale=scale),
        grid=(m // tm, 4),
        in_specs=[
            pl.BlockSpec((tm, d), lambda i, j: (i, 0)),
            pl.BlockSpec((1, d), lambda i, j: (0, 0)),
            pl.BlockSpec((d, w), lambda i, j: (0, j)),
        ],
        out_specs=[out_spec, out_spec, out_spec, out_spec],
        out_shape=[jax.ShapeDtypeStruct((m, w), BF16)] + [jax.ShapeDtypeStruct((m, w), F32)] * 3,
        scratch_shapes=[pltpu.VMEM((tm, d), BF16)],
        compiler_params=pltpu.CompilerParams(
            dimension_semantics=("arbitrary", "arbitrary"), vmem_limit_bytes=vmem),
        name=name,
    )(x, norm_g, w_in)


def _bias_tile_kernel(tab_ref, o_ref, *, tq):
    h = pl.program_id(0)
    r = lax.broadcasted_iota(jnp.int32, (2 * tq, 2 * tq), 0)
    c = lax.broadcasted_iota(jnp.int32, (2 * tq, 2 * tq), 1)
    key = r - tq
    qry = c % tq
    far = tab_ref[h * NUM_BUCKETS + NUM_BUCKETS - 1]
    bucket = _bucket(qry - key)
    val = jnp.zeros(r.shape, F32)
    for b in range(NUM_BUCKETS - 1):
        val = jnp.where(bucket == b, (tab_ref[h * NUM_BUCKETS + b] - far) * LOG2E, val)
    o_ref[...] = jnp.where(key <= qry, val, NEG)


def _bias_tiles(rel_bias, *, tq):
    n_heads = rel_bias.shape[1]
    tab = rel_bias.T.reshape(-1)
    return pl.pallas_call(
        functools.partial(_bias_tile_kernel, tq=tq),
        grid=(n_heads,),
        in_specs=[pl.BlockSpec(memory_space=pltpu.SMEM)],
        out_specs=pl.BlockSpec((None, 2 * tq, 2 * tq), lambda h: (h, 0, 0)),
        out_shape=jax.ShapeDtypeStruct((n_heads, 2 * tq, 2 * tq), F32),
        compiler_params=pltpu.CompilerParams(
            dimension_semantics=("arbitrary",),
            vmem_limit_bytes=_vmem_limit(16 * 2 * tq * 2 * tq * 4)),
        name="rel_bias_tiles",
    )(tab)


def _lambda(lq1_ref, lk1_ref, lq2_ref, lk2_ref, lam_init):
    a = jnp.sum(lq1_ref[...] * lk1_ref[...], axis=-1, keepdims=True)
    b = jnp.sum(lq2_ref[...] * lk2_ref[...], axis=-1, keepdims=True)
    return jnp.exp(a) - jnp.exp(b) + lam_init


def _attn_body(qi, lq1_ref, lk1_ref, lq2_ref, lk2_ref, g_ref, q_ref, k_ref, v_ref, bias_ref,
               o_ref, kb_scr, vt_scr, m_scr, l_scr, acc_scr, *, t, tq, dk, lam_init):
    n_strips = t // tq
    seq = k_ref.shape[0]
    sw = 2 * tq

    @pl.when(qi == 0)
    def _():
        kb_scr[0:tq, :] = jnp.zeros((tq, kb_scr.shape[1]), BF16)
        vt_scr[0] = jnp.zeros(vt_scr.shape[1:], BF16)
        kb_scr[tq:, :] = k_ref[...].astype(BF16)
        for c in range(seq // tq):
            vt_scr[c + 1] = v_ref[c * tq:(c + 1) * tq, :].T.astype(BF16)

    qa = q_ref[...].astype(F32)
    lane = lax.broadcasted_iota(jnp.int32, qa.shape, 1)
    q1 = jnp.where(lane < dk, qa, 0.0).astype(BF16)
    q2 = jnp.where(lane >= dk, qa, 0.0).astype(BF16)
    strips = [jnp.concatenate([q1[u * tq:(u + 1) * tq], q2[u * tq:(u + 1) * tq]], axis=0)
              for u in range(n_strips)]
    qs_all = jnp.concatenate(strips, axis=0)

    m_scr[...] = jnp.full(m_scr.shape, NEG, F32)
    l_scr[...] = jnp.zeros(l_scr.shape, F32)
    acc_scr[...] = jnp.zeros(acc_scr.shape, F32)

    def keys(row0, n):
        return kb_scr[pl.ds(pl.multiple_of(row0, tq), n), :]

    def values_t(chunk0, n_chunks):
        return jnp.concatenate([vt_scr[chunk0 + i] for i in range(n_chunks)], axis=1)

    def fold(lanes, pieces):
        m_prev = m_scr[:, lanes]
        m_new = m_prev
        for s, _ in pieces:
            m_new = jnp.maximum(m_new, jnp.max(s, axis=0, keepdims=True))
        alpha = jnp.exp2(m_prev - m_new)
        l_new = alpha * l_scr[:, lanes]
        pv = None
        for s, vt in pieces:
            p = jnp.exp2(s - m_new)
            l_new = l_new + jnp.sum(p, axis=0, keepdims=True)
            term = jnp.dot(vt, p.astype(BF16), preferred_element_type=F32)
            pv = term if pv is None else pv + term
        l_scr[:, lanes] = l_new
        acc_scr[:, lanes] = alpha * acc_scr[:, lanes] + pv
        m_scr[:, lanes] = m_new

    def scores(kc, qs):
        return lax.dot_general(kc, qs, (((1,), (1,)), ((), ())), preferred_element_type=F32)

    far_chunks = t // tq
    every = slice(0, n_strips * sw)

    def fold_far(blocks):
        scored = []
        for chunk0, n_chunks in blocks:
            kc = keys(chunk0 * tq, n_chunks * tq)
            scored.append((scores(kc, qs_all), values_t(chunk0, n_chunks)))
        for piece in scored:
            fold(every, [piece])

    n_far_blocks = jnp.maximum(qi - 1, 0)
    n_pairs = n_far_blocks // 2

    def far_body(j, carry):
        c0 = 1 + j * 2 * far_chunks
        fold_far([(c0, far_chunks), (c0 + far_chunks, far_chunks)])
        return carry

    lax.fori_loop(0, n_pairs, far_body, 0)
    tail0 = 1 + n_pairs * 2 * far_chunks

    @pl.when((qi > 0) & (n_far_blocks % 2 == 0))
    def _():
        fold_far([(tail0, far_chunks - 1)])

    @pl.when(n_far_blocks % 2 == 1)
    def _():
        fold_far([(tail0, far_chunks), (tail0 + far_chunks, far_chunks - 1)])

    n_absent = jnp.where(qi == 0, tq, 0)
    pending = []
    near_ahead = 1
    for u in range(n_strips):
        lanes = slice(u * sw, (u + 1) * sw)
        qs = strips[u]
        pieces = []
        if u > 0:
            s_a = scores(keys(qi * t, u * tq), qs)
            row = lax.broadcasted_iota(jnp.int32, s_a.shape, 0)
            s_a = jnp.where(row < n_absent, NEG, s_a)
            pieces.append((s_a, values_t(qi * far_chunks, u)))
        bias = bias_ref[...]
        if u == 0:
            row = lax.broadcasted_iota(jnp.int32, bias.shape, 0)
            bias = jnp.where(row < n_absent, NEG, bias)
        s_b = scores(keys(qi * t + u * tq, 2 * tq), qs) + bias
        pieces.append((s_b, values_t(qi * far_chunks + u, 2)))
        pending.append((lanes, pieces))
        if len(pending) > near_ahead:
            fold(*pending.pop(0))
    for item in pending:
        fold(*item)

    o = acc_scr[...] / l_scr[...]
    lam = _lambda(lq1_ref, lk1_ref, lq2_ref, lk2_ref, lam_init)
    for u in range(n_strips):
        w = o[:, u * sw:u * sw + tq] - lam * o[:, u * sw + tq:(u + 1) * sw]
        y = (w * lax.rsqrt(jnp.mean(w * w, axis=0, keepdims=True) + SUBLN_EPS)
             * g_ref[...] * (1.0 - lam_init))
        o_ref[u * tq:(u + 1) * tq, :] = y.T.astype(o_ref.dtype)


class _Operands(NamedTuple):
    args: tuple
    in_specs: list
    out_spec: pl.BlockSpec
    out_shape: jax.ShapeDtypeStruct
    scratch: list
    vmem_bytes: int


def _attn_prompt_operands(q, k, v, bias, lams, subln_g, *, seq, n_heads, t, tq, index):
    dv = q.shape[1] // n_heads
    nq = seq // t
    assert t % tq == 0 and tq >= MAX_DISTANCE
    at = lambda f: (lambda *g: f(*index(*g)))
    g_cols = jnp.broadcast_to(subln_g.reshape(dv, 1), (dv, tq))
    lam_spec = pl.BlockSpec((1, dv // 2), at(lambda h, b, i: (0, 0)))
    kv_spec = pl.BlockSpec((seq, dv), at(lambda h, b, i: (b, h)))
    tile_spec = pl.BlockSpec((t, dv), at(lambda h, b, i: (b * nq + i, h)))
    return _Operands(
        args=(*lams, g_cols, q, k, v, bias),
        in_specs=[lam_spec, lam_spec, lam_spec, lam_spec,
                  pl.BlockSpec((dv, tq), at(lambda h, b, i: (0, 0))),
                  tile_spec, kv_spec, kv_spec,
                  pl.BlockSpec((None, 2 * tq, 2 * tq), at(lambda h, b, i: (h, 0, 0)))],
        out_spec=tile_spec,
        out_shape=jax.ShapeDtypeStruct(q.shape, BF16),
        scratch=[pltpu.VMEM((seq + tq, dv), BF16), pltpu.VMEM((seq // tq + 1, dv, tq), BF16),
                 pltpu.VMEM((1, 2 * t), F32), pltpu.VMEM((1, 2 * t), F32),
                 pltpu.VMEM((dv, 2 * t), F32)],
        vmem_bytes=(2 * 2 * seq * dv * 4 + 2 * (seq + tq) * dv * 2 + 2 * 2 * tq * 2 * tq * 4
                    + 4 * t * dv * 2 + dv * 2 * t * 4 + 6 * t * 2 * t * 4))


def _decode_body(c, n_c, lq1_ref, lk1_ref, lq2_ref, lk2_ref, g_ref, tab_ref, q_ref, kn_ref,
                 vn_ref, *rest, n_pages_step, n_heads, dv, lam_init):
    p_n = n_pages_step
    k_refs = rest[:p_n]
    v_refs = rest[p_n:2 * p_n]
    o_ref, qm_scr, bias_scr, m_scr, l_scr, acc_scr = rest[2 * p_n:]
    rows = 2 * n_heads
    page_cols = PAGE_SIZE * n_heads

    @pl.when(c == 0)
    def _():
        q = q_ref[...].astype(F32)
        lane = lax.broadcasted_iota(jnp.int32, q.shape, 1)
        qm_scr[...] = jnp.concatenate(
            [jnp.where(lane < dv // 2, q, 0.0), jnp.where(lane >= dv // 2, q, 0.0)],
            axis=0).astype(BF16)
        key_row = lax.broadcasted_iota(jnp.int32, (1, page_cols), 1) // n_heads
        bucket = _bucket(PAGE_SIZE - key_row)
        val = jnp.zeros((rows, page_cols), F32)
        for b in range(NUM_BUCKETS - 1):
            val = jnp.where(bucket == b, tab_ref[:, b:b + 1], val)
        bias_scr[...] = val
        m_scr[...] = jnp.full(m_scr.shape, NEG, F32)
        l_scr[...] = jnp.zeros(l_scr.shape, F32)
        acc_scr[...] = jnp.zeros(acc_scr.shape, F32)

    qm = qm_scr[...]
    row = lax.broadcasted_iota(jnp.int32, (rows, page_cols), 0)
    col = lax.broadcasted_iota(jnp.int32, (rows, page_cols), 1)
    own = (col % n_heads) == (row % n_heads)
    near = jnp.where(c == n_c - 1, bias_scr[...], 0.0)
    pieces = []
    for j in range(p_n):
        kf = k_refs[j][...].reshape(page_cols, dv).astype(BF16)
        sj = lax.dot_general(qm, kf, (((1,), (1,)), ((), ())), preferred_element_type=F32)
        if j == p_n - 1:
            sj = sj + near
        pieces.append(jnp.where(own, sj, NEG))

    def update(s, pv_fn):
        m_prev = m_scr[...]
        m_new = jnp.maximum(m_prev, jnp.max(s, axis=-1, keepdims=True))
        p = jnp.exp2(s - m_new)
        alpha = jnp.exp2(m_prev - m_new)
        l_scr[...] = alpha * l_scr[...] + jnp.sum(p, axis=-1, keepdims=True)
        acc_scr[...] = alpha * acc_scr[...] + pv_fn(p)
        m_scr[...] = m_new

    def pv_pages(first):
        def pv(p):
            pb = p.astype(BF16)
            out = None
            for j in range(DECODE_GROUP):
                vf = v_refs[first + j][...].reshape(page_cols, dv).astype(BF16)
                term = jnp.dot(pb[:, j * page_cols:(j + 1) * page_cols], vf,
                               preferred_element_type=F32)
                out = term if out is None else out + term
            return out
        return pv

    for first in range(0, p_n, DECODE_GROUP):
        update(jnp.concatenate(pieces[first:first + DECODE_GROUP], axis=1), pv_pages(first))

    @pl.when(c == n_c - 1)
    def _():
        kn = jnp.concatenate([kn_ref[...]] * 2, axis=0)
        vn = jnp.concatenate([vn_ref[...]] * 2, axis=0)
        s_self = jnp.sum(qm.astype(F32) * kn, axis=-1, keepdims=True) + tab_ref[:, 0:1]
        update(s_self, lambda p: p * vn)
        o = acc_scr[...] / l_scr[...]
        lam = _lambda(lq1_ref, lk1_ref, lq2_ref, lk2_ref, lam_init)
        w = o[0:n_heads] - lam * o[n_heads:rows]
        o_ref[...] = _rms(w, g_ref[...], SUBLN_EPS) * (1.0 - lam_init)


def _attn_decode_operands(q, k_new, v_new, cache_k, cache_v, tab, lams, subln_g, *,
                          n_pages_step, index):
    db, n_heads, dv = q.shape
    p_n = n_pages_step
    rows = 2 * n_heads
    page_cols = PAGE_SIZE * n_heads
    assert PAGE_SIZE >= MAX_DISTANCE and p_n % DECODE_GROUP == 0
    at = lambda f: (lambda *g: f(*index(*g)))
    small = lambda shape: pl.BlockSpec(shape, at(lambda s, c, pt: (0,) * len(shape)))
    row_spec = pl.BlockSpec((None, n_heads, dv), at(lambda s, c, pt: (s, 0, 0)))

    def page_spec(j):
        return pl.BlockSpec((None, PAGE_SIZE, n_heads, dv),
                            at(lambda s, c, pt: (pt[s, c * p_n + j], 0, 0, 0)))

    return _Operands(
        args=(*lams, subln_g, tab, q, k_new, v_new, *([cache_k] * p_n), *([cache_v] * p_n)),
        in_specs=[small((1, dv // 2))] * 4 + [small((1, dv)), small((rows, NUM_BUCKETS)),
                                              row_spec, row_spec, row_spec]
        + [page_spec(j) for j in range(p_n)] * 2,
        out_spec=row_spec,
        out_shape=jax.ShapeDtypeStruct((db, n_heads, dv), F32),
        scratch=[pltpu.VMEM((rows, dv), BF16), pltpu.VMEM((rows, page_cols), F32),
                 pltpu.VMEM((rows, 1), F32), pltpu.VMEM((rows, 1), F32),
                 pltpu.VMEM((rows, dv), F32)],
        vmem_bytes=(2 * 2 * p_n * page_cols * dv * 4
                    + 2 * 2 * page_cols * dv * 2
                    + 6 * rows * p_n * page_cols * 4))


def _attention(prompt, decode, page_table, *, grid_prompt, n_chunks, attn_kw, decode_kw):
    n_heads, batch, nq = grid_prompt

    def flat(h, b, i):
        return (h * batch + b) * nq + i

    n_seq = page_table.shape[0]
    if n_heads * batch * nq == n_seq * n_chunks:
        pr = prompt(index=lambda h, b, i, pt: (h, b, i))
        de = decode(index=lambda h, b, i, pt: (flat(h, b, i) // n_chunks,
                                               flat(h, b, i) % n_chunks, pt))
        n_pr, n_de = len(pr.args), len(de.args)

        def fused(pt_ref, *refs):
            del pt_ref
            ins, outs, scr = refs[:n_pr + n_de], refs[n_pr + n_de:n_pr + n_de + 2], \
                refs[n_pr + n_de + 2:]
            h, b, i = pl.program_id(0), pl.program_id(1), pl.program_id(2)
            _decode_body(flat(h, b, i) % n_chunks, n_chunks, *ins[n_pr:], outs[1],
                         *scr[len(pr.scratch):], **decode_kw)
            _attn_body(i, *ins[:n_pr], outs[0], *scr[:len(pr.scratch)], **attn_kw)

        return pl.pallas_call(
            fused,
            grid_spec=pltpu.PrefetchScalarGridSpec(
                num_scalar_prefetch=1, grid=grid_prompt,
                in_specs=pr.in_specs + de.in_specs, out_specs=[pr.out_spec, de.out_spec],
                scratch_shapes=pr.scratch + de.scratch),
            out_shape=[pr.out_shape, de.out_shape],
            compiler_params=pltpu.CompilerParams(
                dimension_semantics=("arbitrary",) * 3,
                vmem_limit_bytes=_vmem_limit(pr.vmem_bytes, de.vmem_bytes)),
            name="attn_prompt_decode",
        )(page_table, *pr.args, *de.args)

    pr = prompt(index=lambda h, b, i: (h, b, i))
    att = pl.pallas_call(
        lambda *refs: _attn_body(pl.program_id(2), *refs, **attn_kw),
        grid=grid_prompt, in_specs=pr.in_specs, out_specs=pr.out_spec, out_shape=pr.out_shape,
        scratch_shapes=pr.scratch,
        compiler_params=pltpu.CompilerParams(
            dimension_semantics=("arbitrary",) * 3, vmem_limit_bytes=_vmem_limit(pr.vmem_bytes)),
        name="attn_prompt",
    )(*pr.args)
    de = decode(index=lambda s, c, pt: (s, c, pt))
    att_s = pl.pallas_call(
        lambda pt_ref, *refs: _decode_body(pl.program_id(1), n_chunks, *refs, **decode_kw),
        grid_spec=pltpu.PrefetchScalarGridSpec(
            num_scalar_prefetch=1, grid=(n_seq, n_chunks), in_specs=de.in_specs,
            out_specs=de.out_spec, scratch_shapes=de.scratch),
        out_shape=de.out_shape,
        compiler_params=pltpu.CompilerParams(
            dimension_semantics=("arbitrary",) * 2, vmem_limit_bytes=_vmem_limit(de.vmem_bytes)),
        name="attn_decode",
    )(page_table, *de.args)
    return att, att_s


def _pool_project(diffs, pw_ref, ps_ref):
    outs = [jnp.dot(d.astype(BF16), pw_ref[g], preferred_element_type=F32)
            for g, d in enumerate(diffs)]
    return (jnp.concatenate(outs, axis=1) * ps_ref[...]).astype(BF16)


def _mix_kernel(att_ref, xp_ref, halo_ref, pw_ref, ps_ref, wo_ref, y_ref, o_ref, full_scr,
                *, tm, tiles_per_seq):
    t_in_seq = lax.rem(pl.program_id(0), tiles_per_seq)
    att_w = att_ref.shape[1]
    gc = xp_ref.shape[1] // len(POOL_WINDOWS)
    out = y_ref[...] + jnp.dot(att_ref[...], wo_ref[0:att_w, :], preferred_element_type=F32)
    full_scr[0:POOL_HALO, :] = jnp.where(t_in_seq == 0, 0.0, halo_ref[...])
    full_scr[POOL_HALO:, :] = xp_ref[...]
    pos = t_in_seq * tm + lax.broadcasted_iota(jnp.int32, (tm, 1), 0)
    diffs = []
    for g, win in enumerate(POOL_WINDOWS):
        cols = slice(g * gc, (g + 1) * gc)
        x = xp_ref[:, cols]
        acc = x
        for d in range(1, win):
            acc = acc + full_scr[POOL_HALO - d:POOL_HALO - d + tm, cols]
        cnt = jnp.minimum(pos + 1, win).astype(F32)
        diffs.append(acc / cnt - x)
    pool = _pool_project(diffs, pw_ref, ps_ref)
    o_ref[...] = out + jnp.dot(pool, wo_ref[att_w:, :], preferred_element_type=F32)


def _mix_prompt(att, xp, pool_w, pool_scale, w_out, y, *, seq, tm):
    m, d = y.shape
    aw, pw = att.shape[1], xp.shape[1]
    n_g, gc = pool_w.shape[0], pool_w.shape[1]
    tiles_per_seq = seq // tm
    halo_blocks = tm // POOL_HALO
    vmem = _vmem_limit(2 * tm * (aw * 2 + pw * 4 + 2 * d * 4), 2 * POOL_HALO * pw * 4,
                       2 * (aw + pw) * d * 2, 2 * n_g * gc * gc * 2,
                       (tm + POOL_HALO) * pw * 4, 4 * tm * pw * 4)
    return pl.pallas_call(
        functools.partial(_mix_kernel, tm=tm, tiles_per_seq=tiles_per_seq),
        grid=(m // tm,),
        in_specs=[
            pl.BlockSpec((tm, aw), lambda i: (i, 0)),
            pl.BlockSpec((tm, pw), lambda i: (i, 0)),
            pl.BlockSpec((POOL_HALO, pw), lambda i: (jnp.maximum(i * halo_blocks - 1, 0), 0)),
            pl.BlockSpec((n_g, gc, gc), lambda i: (0, 0, 0)),
            pl.BlockSpec((1, pw), lambda i: (0, 0)),
            pl.BlockSpec((aw + pw, d), lambda i: (0, 0)),
            pl.BlockSpec((tm, d), lambda i: (i, 0)),
        ],
        out_specs=pl.BlockSpec((tm, d), lambda i: (i, 0)),
        out_shape=jax.ShapeDtypeStruct((m, d), F32),
        scratch_shapes=[pltpu.VMEM((tm + POOL_HALO, pw), F32)],
        compiler_params=pltpu.CompilerParams(
            dimension_semantics=("arbitrary",), vmem_limit_bytes=vmem),
        name="mix_prompt",
    )(att, xp, xp, pool_w, pool_scale, w_out, y)


def _mix_sample_kernel(att_ref, xp_ref, st_ref, pw_ref, ps_ref, wo_ref, y_ref, o_ref, *, pos0):
    att_w = att_ref.shape[1]
    gc = xp_ref.shape[1] // len(POOL_WINDOWS)
    row = lax.broadcasted_iota(jnp.int32, (1, POOL_HALO, 1), 1)
    diffs = []
    for g, win in enumerate(POOL_WINDOWS):
        cols = slice(g * gc, (g + 1) * gc)
        x = xp_ref[:, cols]
        prev = jnp.where(row >= POOL_HALO - (win - 1), st_ref[:, :, cols], 0.0)
        acc = x + jnp.sum(prev, axis=1)
        diffs.append(acc / float(min(pos0 + 1, win)) - x)
    pool = _pool_project(diffs, pw_ref, ps_ref)
    o_ref[...] = (y_ref[...]
                  + jnp.dot(att_ref[...].astype(BF16), wo_ref[0:att_w, :],
                            preferred_element_type=F32)
                  + jnp.dot(pool, wo_ref[att_w:, :], preferred_element_type=F32))


def _mix_sample(att, xp, state16, pool_w, pool_scale, w_out, y, *, pos0):
    whole = lambda a: pl.BlockSpec(a.shape, lambda: (0,) * a.ndim)
    args = (att, xp, state16, pool_w, pool_scale, w_out, y)
    return pl.pallas_call(
        functools.partial(_mix_sample_kernel, pos0=pos0),
        in_specs=[whole(a) for a in args],
        out_specs=whole(y),
        out_shape=jax.ShapeDtypeStruct(y.shape, F32),
        compiler_params=pltpu.CompilerParams(
            vmem_limit_bytes=_vmem_limit(*(2 * a.size * a.dtype.itemsize for a in args),
                                         4 * state16.size * 4)),
        name="mix_sample",
    )(*args)


def kernel(x_prompt, x_sample, cache_k, cache_v, state_pool, page_table, w_in, w_out, ffn1_norm,
           ffn1_gate, ffn1_up, ffn1_down, mix_norm, ffn2_norm, ffn2_gate, ffn2_up, ffn2_down,
           lambda_q1, lambda_k1, lambda_q2, lambda_k2, subln_g, pool_w, pool_scale, rel_bias,
           final_norm):
    batch, seq, d = x_prompt.shape
    db, sd, _ = x_sample.shape
    depth, n_phys, page, n_heads, dv = cache_v.shape
    dk = dv // 2
    width = n_heads * dv
    pool_width = state_pool.shape[-1]
    assert sd == 1 and page == PAGE_SIZE and cache_k.shape[-1] == 2 * dk
    assert w_in.shape[-1] == 4 * width and pool_width == width
    assert state_pool.shape[2] == POOL_HALO - 1 and rel_bias.shape[0] == NUM_BUCKETS
    n_past = page_table.shape[1] * PAGE_SIZE
    scale = dk ** -0.5 * LOG2E

    tm_p = min(512, batch * seq)
    tm_ffn = min(1024, batch * seq)
    tf = 512
    t_att = min(512, seq)
    tq_att = 128
    pages_step = min(16, page_table.shape[1])
    assert seq % tm_p == 0 and seq % t_att == 0 and page_table.shape[1] % pages_step == 0

    row2 = lambda a: a.reshape(1, -1)
    yp = x_prompt.reshape(batch * seq, d)
    ys = x_sample.reshape(db * sd, d)
    bias = _bias_tiles(rel_bias, tq=tq_att)
    far = rel_bias[NUM_BUCKETS - 1]
    tab_dec = jnp.tile((rel_bias - far[None, :]).T * LOG2E, (2, 1))
    outs = {k: [] for k in ("kp", "vp", "pp", "ks", "vs", "ps")}

    for l in range(depth):
        lam_init = 0.8 - 0.6 * math.exp(-0.3 * l)
        lams = tuple(row2(a[l]) for a in (lambda_q1, lambda_k1, lambda_q2, lambda_k2))
        ffn_w = []
        for gate, up, down in ((ffn1_gate, ffn1_up, ffn1_down), (ffn2_gate, ffn2_up, ffn2_down)):
            ffn_w.append((gate[l].astype(BF16), up[l].astype(BF16), down[l].astype(BF16)))
        w_in_b = w_in[l].astype(BF16)
        w_out_b = w_out[l].astype(BF16)
        pool_w_b = pool_w[l].astype(BF16)
        pool_s = row2(pool_scale[l])
        last = l == depth - 1
        fin = row2(final_norm)

        yp = _ffn(yp, row2(ffn1_norm[l]), *ffn_w[0], fin, tm=tm_ffn, tf=tf, final=False,
                  name="ffn1_prompt")
        q, k, v, xp = _proj(yp, row2(mix_norm[l]), w_in_b, tm=tm_p, scale=scale,
                            name="proj_prompt")
        ys = _ffn(ys, row2(ffn1_norm[l]), *ffn_w[0], fin, tm=db, tf=tf, final=False,
                  name="ffn1_sample")
        qs, ks, vs, xps = _proj(ys, row2(mix_norm[l]), w_in_b, tm=db, scale=scale,
                                name="proj_sample")

        att, att_s = _attention(
            functools.partial(_attn_prompt_operands, q, k, v, bias, lams, row2(subln_g[l]),
                              seq=seq, n_heads=n_heads, t=t_att, tq=tq_att),
            functools.partial(_attn_decode_operands, qs.reshape(db, n_heads, dv),
                              ks.reshape(db, n_heads, dv), vs.reshape(db, n_heads, dv),
                              cache_k[l], cache_v[l], tab_dec, lams, row2(subln_g[l]),
                              n_pages_step=pages_step),
            page_table, grid_prompt=(n_heads, batch, seq // t_att),
            n_chunks=page_table.shape[1] // pages_step,
            attn_kw=dict(t=t_att, tq=tq_att, dk=dk, lam_init=lam_init),
            decode_kw=dict(n_pages_step=pages_step, n_heads=n_heads, dv=dv, lam_init=lam_init))

        yp = _mix_prompt(att, xp, pool_w_b, pool_s, w_out_b, yp, seq=seq, tm=tm_p)
        yp = _ffn(yp, row2(ffn2_norm[l]), *ffn_w[1], fin, tm=tm_ffn, tf=tf, final=last,
                  name="ffn2_prompt")
        outs["kp"].append(k.reshape(batch, seq, n_heads, dv))
        outs["vp"].append(v.reshape(batch, seq, n_heads, dv))
        outs["pp"].append(xp.reshape(batch, seq, pool_width)[:, seq - (POOL_HALO - 1):])

        state16 = jnp.pad(state_pool[l], ((0, 0), (1, 0), (0, 0)))
        ys = _mix_sample(att_s.reshape(db, width), xps, state16, pool_w_b, pool_s, w_out_b, ys,
                         pos0=n_past)
        ys = _ffn(ys, row2(ffn2_norm[l]), *ffn_w[1], fin, tm=db, tf=tf, final=last,
                  name="ffn2_sample")
        outs["ks"].append(ks.reshape(db, sd, n_heads, dv))
        outs["vs"].append(vs.reshape(db, sd, n_heads, dv))
        outs["ps"].append(jnp.concatenate([state_pool[l][:, 1:], xps[:, None, :]], axis=1))

    st = lambda name: jnp.stack(outs[name], axis=0)
    return (yp.reshape(batch, seq, d), ys.reshape(db, sd, d), st("kp"), st("vp"), st("pp"),
            st("ks"), st("vs"), st("ps"))
```

```python
import functools
import math
from typing import NamedTuple

import jax
import jax.numpy as jnp
from jax import lax
from jax.experimental import pallas as pl
from jax.experimental.pallas import tpu as pltpu

F32 = jnp.float32
BF16 = jnp.bfloat16

V7X_LANES = 128
V7X_VMEM_BYTES = 64 * 1024 * 1024
V7X_VMEM_CAP = V7X_VMEM_BYTES - 8 * 1024 * 1024

EPS = 1e-6
SUBLN_EPS = 1e-5
NEG = -1e30
POOL_WINDOWS = (2, 4, 8, 16)
POOL_HALO = 16
NUM_BUCKETS = 32
MAX_EXACT = NUM_BUCKETS // 2
MAX_DISTANCE = 128
LOG2E = math.log2(math.e)
PAGE_SIZE = 128
DECODE_GROUP = 4


def _vmem_limit(*nbytes):
    est = int(sum(nbytes))
    return min(max(est, 16 * 1024 * 1024), V7X_VMEM_CAP)


def _rms(x, g, eps):
    return x * lax.rsqrt(jnp.mean(x * x, axis=-1, keepdims=True) + eps) * g


def _bucket(n):
    n = jnp.maximum(n, 0)
    nf = jnp.maximum(n, 1).astype(F32)
    large = MAX_EXACT + (jnp.log(nf / MAX_EXACT) / math.log(MAX_DISTANCE / MAX_EXACT)
                         * (NUM_BUCKETS - MAX_EXACT)).astype(jnp.int32)
    large = jnp.minimum(large, NUM_BUCKETS - 1)
    return jnp.where(n < MAX_EXACT, n, large)


def _ffn_kernel(x_ref, g_ref, wg_ref, wu_ref, wd_ref, fg_ref, o_ref, h_scr, *, n_f, last_tf,
                final):
    f = pl.program_id(1)
    tf = wg_ref.shape[1]

    @pl.when(f == 0)
    def _():
        h_scr[...] = _rms(x_ref[...], g_ref[...], EPS).astype(BF16)
        o_ref[...] = jnp.zeros(o_ref.shape, F32)

    def accumulate(cols):
        h = h_scr[...]
        gate = jnp.dot(h, wg_ref[:, 0:cols], preferred_element_type=F32)
        up = jnp.dot(h, wu_ref[:, 0:cols], preferred_element_type=F32)
        act = (gate / (1.0 + jnp.exp(-gate)) * up).astype(BF16)
        o_ref[...] += jnp.dot(act, wd_ref[0:cols, :], preferred_element_type=F32)

    if last_tf == tf:
        accumulate(tf)
    else:
        @pl.when(f < n_f - 1)
        def _():
            accumulate(tf)

        @pl.when(f == n_f - 1)
        def _():
            accumulate(last_tf)

    @pl.when(f == n_f - 1)
    def _():
        y = x_ref[...] + 0.5 * o_ref[...]
        if final:
            y = _rms(y, fg_ref[...], EPS)
        o_ref[...] = y


def _ffn(x, norm_g, wg, wu, wd, final_g, *, tm, tf, final, name):
    m, d = x.shape
    f_dim = wg.shape[1]
    n_f = pl.cdiv(f_dim, tf)
    last_tf = f_dim - (n_f - 1) * tf
    assert last_tf % V7X_LANES == 0
    vmem = _vmem_limit(2 * 2 * tm * d * 4,
                       2 * 3 * d * tf * 2,
                       tm * d * 2,
                       4 * tm * tf * 4 + tm * d * 4)
    return pl.pallas_call(
        functools.partial(_ffn_kernel, n_f=n_f, last_tf=last_tf, final=final),
        grid=(m // tm, n_f),
        in_specs=[
            pl.BlockSpec((tm, d), lambda i, f: (i, 0)),
            pl.BlockSpec((1, d), lambda i, f: (0, 0)),
            pl.BlockSpec((d, tf), lambda i, f: (0, f)),
            pl.BlockSpec((d, tf), lambda i, f: (0, f)),
            pl.BlockSpec((tf, d), lambda i, f: (f, 0)),
            pl.BlockSpec((1, d), lambda i, f: (0, 0)),
        ],
        out_specs=pl.BlockSpec((tm, d), lambda i, f: (i, 0)),
        out_shape=jax.ShapeDtypeStruct((m, d), F32),
        scratch_shapes=[pltpu.VMEM((tm, d), BF16)],
        compiler_params=pltpu.CompilerParams(
            dimension_semantics=("arbitrary", "arbitrary"), vmem_limit_bytes=vmem),
        name=name,
    )(x, norm_g, wg, wu, wd, final_g)


PROJ_Q, PROJ_K, PROJ_V, PROJ_POOL = range(4)


def _proj_kernel(x_ref, g_ref, w_ref, o_ref, h_scr, *, scale):
    j = pl.program_id(1)

    @pl.when(j == 0)
    def _():
        h_scr[...] = _rms(x_ref[...], g_ref[...], EPS).astype(BF16)

    p = jnp.dot(h_scr[...], w_ref[...], preferred_element_type=F32)
    o_ref[...] = p * jnp.where(j == PROJ_Q, scale, 1.0)


def _proj(x, norm_g, w_in, *, tm, scale, name):
    m, d = x.shape
    w = w_in.shape[1] // 4
    vmem = _vmem_limit(2 * tm * d * 4, 2 * d * w * 2, 2 * tm * w * 4, tm * d * 2, 2 * tm * w * 4)
    return pl.pallas_call(
        functools.partial(_proj_kernel, scale=scale),
        grid=(m // tm, 4),
        in_specs=[
            pl.BlockSpec((tm, d), lambda i, j: (i, 0)),
            pl.BlockSpec((1, d), lambda i, j: (0, 0)),
            pl.BlockSpec((d, w), lambda i, j: (0, j)),
        ],
        out_specs=pl.BlockSpec((None, tm, w), lambda i, j: (j, i, 0)),
        out_shape=jax.ShapeDtypeStruct((4, m, w), F32),
        scratch_shapes=[pltpu.VMEM((tm, d), BF16)],
        compiler_params=pltpu.CompilerParams(
            dimension_semantics=("arbitrary", "arbitrary"), vmem_limit_bytes=vmem),
        name=name,
    )(x, norm_g, w_in)


def _bias_tile_kernel(tab_ref, o_ref, *, tq):
    h = pl.program_id(0)
    r = lax.broadcasted_iota(jnp.int32, (2 * tq, 2 * tq), 0)
    c = lax.broadcasted_iota(jnp.int32, (2 * tq, 2 * tq), 1)
    key = r - tq
    qry = c % tq
    far = tab_ref[h * NUM_BUCKETS + NUM_BUCKETS - 1]
    bucket = _bucket(qry - key)
    val = jnp.zeros(r.shape, F32)
    for b in range(NUM_BUCKETS - 1):
        val = jnp.where(bucket == b, (tab_ref[h * NUM_BUCKETS + b] - far) * LOG2E, val)
    o_ref[...] = jnp.where(key <= qry, val, NEG)


def _bias_tiles(rel_bias, *, tq):
    n_heads = rel_bias.shape[1]
    tab = rel_bias.T.reshape(-1)
    return pl.pallas_call(
        functools.partial(_bias_tile_kernel, tq=tq),
        grid=(n_heads,),
        in_specs=[pl.BlockSpec(memory_space=pltpu.SMEM)],
        out_specs=pl.BlockSpec((None, 2 * tq, 2 * tq), lambda h: (h, 0, 0)),
        out_shape=jax.ShapeDtypeStruct((n_heads, 2 * tq, 2 * tq), F32),
        compiler_params=pltpu.CompilerParams(
            dimension_semantics=("arbitrary",),
            vmem_limit_bytes=_vmem_limit(16 * 2 * tq * 2 * tq * 4)),
        name="rel_bias_tiles",
    )(tab)


def _lambda(lq1_ref, lk1_ref, lq2_ref, lk2_ref, lam_init):
    a = jnp.sum(lq1_ref[...] * lk1_ref[...], axis=-1, keepdims=True)
    b = jnp.sum(lq2_ref[...] * lk2_ref[...], axis=-1, keepdims=True)
    return jnp.exp(a) - jnp.exp(b) + lam_init


def _attn_body(qi, lq1_ref, lk1_ref, lq2_ref, lk2_ref, g_ref, q_ref, k_ref, v_ref, bias_ref,
               o_ref, kb_scr, vt_scr, m_scr, l_scr, acc_scr, *, t, tq, dk, lam_init):
    n_strips = t // tq
    seq = k_ref.shape[0]
    sw = 2 * tq

    @pl.when(qi == 0)
    def _():
        kb_scr[0:tq, :] = jnp.zeros((tq, kb_scr.shape[1]), BF16)
        vt_scr[0] = jnp.zeros(vt_scr.shape[1:], BF16)
        kb_scr[tq:, :] = k_ref[...].astype(BF16)
        for c in range(seq // tq):
            vt_scr[c + 1] = v_ref[c * tq:(c + 1) * tq, :].T.astype(BF16)

    qa = q_ref[...].astype(F32)
    lane = lax.broadcasted_iota(jnp.int32, qa.shape, 1)
    q1 = jnp.where(lane < dk, qa, 0.0).astype(BF16)
    q2 = jnp.where(lane >= dk, qa, 0.0).astype(BF16)
    strips = [jnp.concatenate([q1[u * tq:(u + 1) * tq], q2[u * tq:(u + 1) * tq]], axis=0)
              for u in range(n_strips)]
    qs_all = jnp.concatenate(strips, axis=0)

    m_scr[...] = jnp.full(m_scr.shape, NEG, F32)
    l_scr[...] = jnp.zeros(l_scr.shape, F32)
    acc_scr[...] = jnp.zeros(acc_scr.shape, F32)

    def keys(row0, n):
        return kb_scr[pl.ds(pl.multiple_of(row0, tq), n), :]

    def values_t(chunk0, n_chunks):
        return jnp.concatenate([vt_scr[chunk0 + i] for i in range(n_chunks)], axis=1)

    def fold(lanes, pieces):
        m_prev = m_scr[:, lanes]
        m_new = m_prev
        for s, _ in pieces:
            m_new = jnp.maximum(m_new, jnp.max(s, axis=0, keepdims=True))
        alpha = jnp.exp2(m_prev - m_new)
        l_new = alpha * l_scr[:, lanes]
        pv = None
        for s, vt in pieces:
            p = jnp.exp2(s - m_new)
            l_new = l_new + jnp.sum(p, axis=0, keepdims=True)
            term = jnp.dot(vt, p.astype(BF16), preferred_element_type=F32)
            pv = term if pv is None else pv + term
        l_scr[:, lanes] = l_new
        acc_scr[:, lanes] = alpha * acc_scr[:, lanes] + pv
        m_scr[:, lanes] = m_new

    def scores(kc, qs):
        return lax.dot_general(kc, qs, (((1,), (1,)), ((), ())), preferred_element_type=F32)

    far_chunks = t // tq
    every = slice(0, n_strips * sw)

    def fold_far(blocks):
        scored = []
        for chunk0, n_chunks in blocks:
            kc = keys(chunk0 * tq, n_chunks * tq)
            scored.append((scores(kc, qs_all), values_t(chunk0, n_chunks)))
        for piece in scored:
            fold(every, [piece])

    n_far_blocks = jnp.maximum(qi - 1, 0)
    n_pairs = n_far_blocks // 2

    def far_body(j, carry):
        c0 = 1 + j * 2 * far_chunks
        fold_far([(c0, far_chunks), (c0 + far_chunks, far_chunks)])
        return carry

    lax.fori_loop(0, n_pairs, far_body, 0)
    tail0 = 1 + n_pairs * 2 * far_chunks

    @pl.when((qi > 0) & (n_far_blocks % 2 == 0))
    def _():
        fold_far([(tail0, far_chunks - 1)])

    @pl.when(n_far_blocks % 2 == 1)
    def _():
        fold_far([(tail0, far_chunks), (tail0 + far_chunks, far_chunks - 1)])

    n_absent = jnp.where(qi == 0, tq, 0)
    pending = []
    near_ahead = 1
    for u in range(n_strips):
        lanes = slice(u * sw, (u + 1) * sw)
        qs = strips[u]
        pieces = []
        if u > 0:
            s_a = scores(keys(qi * t, u * tq), qs)
            row = lax.broadcasted_iota(jnp.int32, s_a.shape, 0)
            s_a = jnp.where(row < n_absent, NEG, s_a)
            pieces.append((s_a, values_t(qi * far_chunks, u)))
        bias = bias_ref[...]
        if u == 0:
            row = lax.broadcasted_iota(jnp.int32, bias.shape, 0)
            bias = jnp.where(row < n_absent, NEG, bias)
        s_b = scores(keys(qi * t + u * tq, 2 * tq), qs) + bias
        pieces.append((s_b, values_t(qi * far_chunks + u, 2)))
        pending.append((lanes, pieces))
        if len(pending) > near_ahead:
            fold(*pending.pop(0))
    for item in pending:
        fold(*item)

    o = acc_scr[...] / l_scr[...]
    lam = _lambda(lq1_ref, lk1_ref, lq2_ref, lk2_ref, lam_init)
    for u in range(n_strips):
        w = o[:, u * sw:u * sw + tq] - lam * o[:, u * sw + tq:(u + 1) * sw]
        y = (w * lax.rsqrt(jnp.mean(w * w, axis=0, keepdims=True) + SUBLN_EPS)
             * g_ref[...] * (1.0 - lam_init))
        o_ref[u * tq:(u + 1) * tq, :] = y.T.astype(o_ref.dtype)


class _Operands(NamedTuple):
    args: tuple
    in_specs: list
    out_spec: pl.BlockSpec
    out_shape: jax.ShapeDtypeStruct
    scratch: list
    vmem_bytes: int


def _attn_prompt_operands(qkv, bias, lams, subln_g, *, seq, n_heads, t, tq, index):
    dv = qkv.shape[2] // n_heads
    nq = seq // t
    assert t % tq == 0 and tq >= MAX_DISTANCE
    at = lambda f: (lambda *g: f(*index(*g)))
    g_cols = jnp.broadcast_to(subln_g.reshape(dv, 1), (dv, tq))
    lam_spec = pl.BlockSpec((1, dv // 2), at(lambda h, b, i: (0, 0)))
    kv_spec = lambda sec: pl.BlockSpec((None, seq, dv), at(lambda h, b, i: (sec, b, h)))
    return _Operands(
        args=(*lams, g_cols, qkv, qkv, qkv, bias),
        in_specs=[lam_spec, lam_spec, lam_spec, lam_spec,
                  pl.BlockSpec((dv, tq), at(lambda h, b, i: (0, 0))),
                  pl.BlockSpec((None, t, dv), at(lambda h, b, i: (PROJ_Q, b * nq + i, h))),
                  kv_spec(PROJ_K), kv_spec(PROJ_V),
                  pl.BlockSpec((None, 2 * tq, 2 * tq), at(lambda h, b, i: (h, 0, 0)))],
        out_spec=pl.BlockSpec((t, dv), at(lambda h, b, i: (b * nq + i, h))),
        out_shape=jax.ShapeDtypeStruct(qkv.shape[1:], BF16),
        scratch=[pltpu.VMEM((seq + tq, dv), BF16), pltpu.VMEM((seq // tq + 1, dv, tq), BF16),
                 pltpu.VMEM((1, 2 * t), F32), pltpu.VMEM((1, 2 * t), F32),
                 pltpu.VMEM((dv, 2 * t), F32)],
        vmem_bytes=(2 * 2 * seq * dv * 4 + 2 * (seq + tq) * dv * 2 + 2 * 2 * tq * 2 * tq * 4
                    + 4 * t * dv * 2 + dv * 2 * t * 4 + 6 * t * 2 * t * 4))


def _decode_body(c, n_c, lq1_ref, lk1_ref, lq2_ref, lk2_ref, g_ref, tab_ref, q_ref, kn_ref,
                 vn_ref, *rest, n_pages_step, n_heads, dv, lam_init):
    p_n = n_pages_step
    k_refs = rest[:p_n]
    v_refs = rest[p_n:2 * p_n]
    o_ref, qm_scr, bias_scr, m_scr, l_scr, acc_scr = rest[2 * p_n:]
    rows = 2 * n_heads
    page_cols = PAGE_SIZE * n_heads

    @pl.when(c == 0)
    def _():
        q = q_ref[...].astype(F32)
        lane = lax.broadcasted_iota(jnp.int32, q.shape, 1)
        qm_scr[...] = jnp.concatenate(
            [jnp.where(lane < dv // 2, q, 0.0), jnp.where(lane >= dv // 2, q, 0.0)],
            axis=0).astype(BF16)
        key_row = lax.broadcasted_iota(jnp.int32, (1, page_cols), 1) // n_heads
        bucket = _bucket(PAGE_SIZE - key_row)
        val = jnp.zeros((rows, page_cols), F32)
        for b in range(NUM_BUCKETS - 1):
            val = jnp.where(bucket == b, tab_ref[:, b:b + 1], val)
        bias_scr[...] = val
        m_scr[...] = jnp.full(m_scr.shape, NEG, F32)
        l_scr[...] = jnp.zeros(l_scr.shape, F32)
        acc_scr[...] = jnp.zeros(acc_scr.shape, F32)

    qm = qm_scr[...]
    row = lax.broadcasted_iota(jnp.int32, (rows, page_cols), 0)
    col = lax.broadcasted_iota(jnp.int32, (rows, page_cols), 1)
    own = (col % n_heads) == (row % n_heads)
    near = jnp.where(c == n_c - 1, bias_scr[...], 0.0)
    pieces = []
    for j in range(p_n):
        kf = k_refs[j][...].reshape(page_cols, dv).astype(BF16)
        sj = lax.dot_general(qm, kf, (((1,), (1,)), ((), ())), preferred_element_type=F32)
        if j == p_n - 1:
            sj = sj + near
        pieces.append(jnp.where(own, sj, NEG))

    def update(s, pv_fn):
        m_prev = m_scr[...]
        m_new = jnp.maximum(m_prev, jnp.max(s, axis=-1, keepdims=True))
        p = jnp.exp2(s - m_new)
        alpha = jnp.exp2(m_prev - m_new)
        l_scr[...] = alpha * l_scr[...] + jnp.sum(p, axis=-1, keepdims=True)
        acc_scr[...] = alpha * acc_scr[...] + pv_fn(p)
        m_scr[...] = m_new

    def pv_pages(first):
        def pv(p):
            pb = p.astype(BF16)
            out = None
            for j in range(DECODE_GROUP):
                vf = v_refs[first + j][...].reshape(page_cols, dv).astype(BF16)
                term = jnp.dot(pb[:, j * page_cols:(j + 1) * page_cols], vf,
                               preferred_element_type=F32)
                out = term if out is None else out + term
            return out
        return pv

    for first in range(0, p_n, DECODE_GROUP):
        update(jnp.concatenate(pieces[first:first + DECODE_GROUP], axis=1), pv_pages(first))

    @pl.when(c == n_c - 1)
    def _():
        kn = jnp.concatenate([kn_ref[...]] * 2, axis=0)
        vn = jnp.concatenate([vn_ref[...]] * 2, axis=0)
        s_self = jnp.sum(qm.astype(F32) * kn, axis=-1, keepdims=True) + tab_ref[:, 0:1]
        update(s_self, lambda p: p * vn)
        o = acc_scr[...] / l_scr[...]
        lam = _lambda(lq1_ref, lk1_ref, lq2_ref, lk2_ref, lam_init)
        w = o[0:n_heads] - lam * o[n_heads:rows]
        o_ref[...] = _rms(w, g_ref[...], SUBLN_EPS) * (1.0 - lam_init)


def _attn_decode_operands(q, k_new, v_new, cache_k, cache_v, tab, lams, subln_g, *,
                          n_pages_step, index):
    db, n_heads, dv = q.shape
    p_n = n_pages_step
    rows = 2 * n_heads
    page_cols = PAGE_SIZE * n_heads
    assert PAGE_SIZE >= MAX_DISTANCE and p_n % DECODE_GROUP == 0
    at = lambda f: (lambda *g: f(*index(*g)))
    small = lambda shape: pl.BlockSpec(shape, at(lambda s, c, pt: (0,) * len(shape)))
    row_spec = pl.BlockSpec((None, n_heads, dv), at(lambda s, c, pt: (s, 0, 0)))

    def page_spec(j):
        return pl.BlockSpec((None, PAGE_SIZE, n_heads, dv),
                            at(lambda s, c, pt: (pt[s, c * p_n + j], 0, 0, 0)))

    return _Operands(
        args=(*lams, subln_g, tab, q, k_new, v_new, *([cache_k] * p_n), *([cache_v] * p_n)),
        in_specs=[small((1, dv // 2))] * 4 + [small((1, dv)), small((rows, NUM_BUCKETS)),
                                              row_spec, row_spec, row_spec]
        + [page_spec(j) for j in range(p_n)] * 2,
        out_spec=row_spec,
        out_shape=jax.ShapeDtypeStruct((db, n_heads, dv), F32),
        scratch=[pltpu.VMEM((rows, dv), BF16), pltpu.VMEM((rows, page_cols), F32),
                 pltpu.VMEM((rows, 1), F32), pltpu.VMEM((rows, 1), F32),
                 pltpu.VMEM((rows, dv), F32)],
        vmem_bytes=(2 * 2 * p_n * page_cols * dv * 4
                    + 2 * 2 * page_cols * dv * 2
                    + 6 * rows * p_n * page_cols * 4))


def _attention(prompt, decode, page_table, *, grid_prompt, n_chunks, attn_kw, decode_kw):
    n_heads, batch, nq = grid_prompt

    def seq_chunk(h, b, i):
        step = (h * batch + b) * nq + i
        return lax.div(step, n_chunks), lax.rem(step, n_chunks)

    n_seq = page_table.shape[0]
    if n_heads * batch * nq == n_seq * n_chunks:
        pr = prompt(index=lambda h, b, i, pt: (h, b, i))
        de = decode(index=lambda h, b, i, pt: (*seq_chunk(h, b, i), pt))
        n_pr, n_de = len(pr.args), len(de.args)

        def fused(pt_ref, *refs):
            del pt_ref
            ins, outs, scr = refs[:n_pr + n_de], refs[n_pr + n_de:n_pr + n_de + 2], \
                refs[n_pr + n_de + 2:]
            h, b, i = pl.program_id(0), pl.program_id(1), pl.program_id(2)
            _decode_body(seq_chunk(h, b, i)[1], n_chunks, *ins[n_pr:], outs[1],
                         *scr[len(pr.scratch):], **decode_kw)
            _attn_body(i, *ins[:n_pr], outs[0], *scr[:len(pr.scratch)], **attn_kw)

        return pl.pallas_call(
            fused,
            grid_spec=pltpu.PrefetchScalarGridSpec(
                num_scalar_prefetch=1, grid=grid_prompt,
                in_specs=pr.in_specs + de.in_specs, out_specs=[pr.out_spec, de.out_spec],
                scratch_shapes=pr.scratch + de.scratch),
            out_shape=[pr.out_shape, de.out_shape],
            compiler_params=pltpu.CompilerParams(
                dimension_semantics=("arbitrary",) * 3,
                vmem_limit_bytes=_vmem_limit(pr.vmem_bytes, de.vmem_bytes)),
            name="attn_prompt_decode",
        )(page_table, *pr.args, *de.args)

    pr = prompt(index=lambda h, b, i: (h, b, i))
    att = pl.pallas_call(
        lambda *refs: _attn_body(pl.program_id(2), *refs, **attn_kw),
        grid=grid_prompt, in_specs=pr.in_specs, out_specs=pr.out_spec, out_shape=pr.out_shape,
        scratch_shapes=pr.scratch,
        compiler_params=pltpu.CompilerParams(
            dimension_semantics=("arbitrary",) * 3, vmem_limit_bytes=_vmem_limit(pr.vmem_bytes)),
        name="attn_prompt",
    )(*pr.args)
    de = decode(index=lambda s, c, pt: (s, c, pt))
    att_s = pl.pallas_call(
        lambda pt_ref, *refs: _decode_body(pl.program_id(1), n_chunks, *refs, **decode_kw),
        grid_spec=pltpu.PrefetchScalarGridSpec(
            num_scalar_prefetch=1, grid=(n_seq, n_chunks), in_specs=de.in_specs,
            out_specs=de.out_spec, scratch_shapes=de.scratch),
        out_shape=de.out_shape,
        compiler_params=pltpu.CompilerParams(
            dimension_semantics=("arbitrary",) * 2, vmem_limit_bytes=_vmem_limit(de.vmem_bytes)),
        name="attn_decode",
    )(page_table, *de.args)
    return att, att_s


def _pool_project(diffs, pw_ref, ps_ref):
    outs = [jnp.dot(d.astype(BF16), pw_ref[g], preferred_element_type=F32)
            for g, d in enumerate(diffs)]
    return (jnp.concatenate(outs, axis=1) * ps_ref[...]).astype(BF16)


def _mix_kernel(att_ref, xp_ref, halo_ref, pw_ref, ps_ref, wo_ref, y_ref, o_ref, full_scr,
                *, tm, tiles_per_seq):
    t_in_seq = lax.rem(pl.program_id(0), tiles_per_seq)
    att_w = att_ref.shape[1]
    gc = xp_ref.shape[1] // len(POOL_WINDOWS)
    out = y_ref[...] + jnp.dot(att_ref[...], wo_ref[0:att_w, :], preferred_element_type=F32)
    full_scr[0:POOL_HALO, :] = jnp.where(t_in_seq == 0, 0.0, halo_ref[...])
    full_scr[POOL_HALO:, :] = xp_ref[...]
    pos = t_in_seq * tm + lax.broadcasted_iota(jnp.int32, (tm, 1), 0)
    diffs = []
    for g, win in enumerate(POOL_WINDOWS):
        cols = slice(g * gc, (g + 1) * gc)
        x = xp_ref[:, cols]
        acc = x
        for d in range(1, win):
            acc = acc + full_scr[POOL_HALO - d:POOL_HALO - d + tm, cols]
        cnt = jnp.minimum(pos + 1, win).astype(F32)
        diffs.append(acc / cnt - x)
    pool = _pool_project(diffs, pw_ref, ps_ref)
    o_ref[...] = out + jnp.dot(pool, wo_ref[att_w:, :], preferred_element_type=F32)


def _mix_prompt(att, proj, pool_w, pool_scale, w_out, y, *, seq, tm):
    m, d = y.shape
    aw, pw = att.shape[1], proj.shape[2]
    n_g, gc = pool_w.shape[0], pool_w.shape[1]
    tiles_per_seq = seq // tm
    halo_blocks = tm // POOL_HALO
    vmem = _vmem_limit(2 * tm * (aw * 2 + pw * 4 + 2 * d * 4), 2 * POOL_HALO * pw * 4,
                       2 * (aw + pw) * d * 2, 2 * n_g * gc * gc * 2,
                       (tm + POOL_HALO) * pw * 4, 4 * tm * pw * 4)
    return pl.pallas_call(
        functools.partial(_mix_kernel, tm=tm, tiles_per_seq=tiles_per_seq),
        grid=(m // tm,),
        in_specs=[
            pl.BlockSpec((tm, aw), lambda i: (i, 0)),
            pl.BlockSpec((None, tm, pw), lambda i: (PROJ_POOL, i, 0)),
            pl.BlockSpec((None, POOL_HALO, pw),
                         lambda i: (PROJ_POOL, jnp.maximum(i * halo_blocks - 1, 0), 0)),
            pl.BlockSpec((n_g, gc, gc), lambda i: (0, 0, 0)),
            pl.BlockSpec((1, pw), lambda i: (0, 0)),
            pl.BlockSpec((aw + pw, d), lambda i: (0, 0)),
            pl.BlockSpec((tm, d), lambda i: (i, 0)),
        ],
        out_specs=pl.BlockSpec((tm, d), lambda i: (i, 0)),
        out_shape=jax.ShapeDtypeStruct((m, d), F32),
        scratch_shapes=[pltpu.VMEM((tm + POOL_HALO, pw), F32)],
        compiler_params=pltpu.CompilerParams(
            dimension_semantics=("arbitrary",), vmem_limit_bytes=vmem),
        name="mix_prompt",
    )(att, proj, proj, pool_w, pool_scale, w_out, y)


def _mix_sample_kernel(att_ref, xp_ref, st_ref, pw_ref, ps_ref, wo_ref, y_ref, o_ref, *, pos0):
    att_w = att_ref.shape[1]
    gc = xp_ref.shape[1] // len(POOL_WINDOWS)
    row = lax.broadcasted_iota(jnp.int32, (1, POOL_HALO, 1), 1)
    diffs = []
    for g, win in enumerate(POOL_WINDOWS):
        cols = slice(g * gc, (g + 1) * gc)
        x = xp_ref[:, cols]
        prev = jnp.where(row >= POOL_HALO - (win - 1), st_ref[:, :, cols], 0.0)
        acc = x + jnp.sum(prev, axis=1)
        diffs.append(acc / float(min(pos0 + 1, win)) - x)
    pool = _pool_project(diffs, pw_ref, ps_ref)
    o_ref[...] = (y_ref[...]
                  + jnp.dot(att_ref[...].astype(BF16), wo_ref[0:att_w, :],
                            preferred_element_type=F32)
                  + jnp.dot(pool, wo_ref[att_w:, :], preferred_element_type=F32))


def _mix_sample(att, xp, state16, pool_w, pool_scale, w_out, y, *, pos0):
    whole = lambda a: pl.BlockSpec(a.shape, lambda: (0,) * a.ndim)
    args = (att, xp, state16, pool_w, pool_scale, w_out, y)
    return pl.pallas_call(
        functools.partial(_mix_sample_kernel, pos0=pos0),
        in_specs=[whole(a) for a in args],
        out_specs=whole(y),
        out_shape=jax.ShapeDtypeStruct(y.shape, F32),
        compiler_params=pltpu.CompilerParams(
            vmem_limit_bytes=_vmem_limit(*(2 * a.size * a.dtype.itemsize for a in args),
                                         4 * state16.size * 4)),
        name="mix_sample",
    )(*args)


def kernel(x_prompt, x_sample, cache_k, cache_v, state_pool, page_table, w_in, w_out, ffn1_norm,
           ffn1_gate, ffn1_up, ffn1_down, mix_norm, ffn2_norm, ffn2_gate, ffn2_up, ffn2_down,
           lambda_q1, lambda_k1, lambda_q2, lambda_k2, subln_g, pool_w, pool_scale, rel_bias,
           final_norm):
    batch, seq, d = x_prompt.shape
    db, sd, _ = x_sample.shape
    depth, n_phys, page, n_heads, dv = cache_v.shape
    dk = dv // 2
    width = n_heads * dv
    pool_width = state_pool.shape[-1]
    assert sd == 1 and page == PAGE_SIZE and cache_k.shape[-1] == 2 * dk
    assert w_in.shape[-1] == 4 * width and pool_width == width
    assert state_pool.shape[2] == POOL_HALO - 1 and rel_bias.shape[0] == NUM_BUCKETS
    n_past = page_table.shape[1] * PAGE_SIZE
    scale = dk ** -0.5 * LOG2E

    tm_p = min(512, batch * seq)
    tm_ffn = min(1024, batch * seq)
    tf = 512
    t_att = min(512, seq)
    tq_att = 128
    pages_step = min(16, page_table.shape[1])
    assert seq % tm_p == 0 and seq % t_att == 0 and page_table.shape[1] % pages_step == 0

    row2 = lambda a: a.reshape(1, -1)
    yp = x_prompt.reshape(batch * seq, d)
    ys = x_sample.reshape(db * sd, d)
    bias = _bias_tiles(rel_bias, tq=tq_att)
    far = rel_bias[NUM_BUCKETS - 1]
    tab_dec = jnp.tile((rel_bias - far[None, :]).T * LOG2E, (2, 1))
    outs = {k: [] for k in ("kp", "vp", "pp", "ks", "vs", "ps")}

    for l in range(depth):
        lam_init = 0.8 - 0.6 * math.exp(-0.3 * l)
        lams = tuple(row2(a[l]) for a in (lambda_q1, lambda_k1, lambda_q2, lambda_k2))
        ffn_w = []
        for gate, up, down in ((ffn1_gate, ffn1_up, ffn1_down), (ffn2_gate, ffn2_up, ffn2_down)):
            ffn_w.append((gate[l].astype(BF16), up[l].astype(BF16), down[l].astype(BF16)))
        w_in_b = w_in[l].astype(BF16)
        w_out_b = w_out[l].astype(BF16)
        pool_w_b = pool_w[l].astype(BF16)
        pool_s = row2(pool_scale[l])
        last = l == depth - 1
        fin = row2(final_norm)

        yp = _ffn(yp, row2(ffn1_norm[l]), *ffn_w[0], fin, tm=tm_ffn, tf=tf, final=False,
                  name="ffn1_prompt")
        proj = _proj(yp, row2(mix_norm[l]), w_in_b, tm=tm_ffn, scale=scale, name="proj_prompt")
        ys = _ffn(ys, row2(ffn1_norm[l]), *ffn_w[0], fin, tm=db, tf=tf, final=False,
                  name="ffn1_sample")
        qs, ks, vs, xps = _proj(ys, row2(mix_norm[l]), w_in_b, tm=db, scale=scale,
                                name="proj_sample")

        att, att_s = _attention(
            functools.partial(_attn_prompt_operands, proj, bias, lams, row2(subln_g[l]),
                              seq=seq, n_heads=n_heads, t=t_att, tq=tq_att),
            functools.partial(_attn_decode_operands, qs.reshape(db, n_heads, dv),
                              ks.reshape(db, n_heads, dv), vs.reshape(db, n_heads, dv),
                              cache_k[l], cache_v[l], tab_dec, lams, row2(subln_g[l]),
                              n_pages_step=pages_step),
            page_table, grid_prompt=(n_heads, batch, seq // t_att),
            n_chunks=page_table.shape[1] // pages_step,
            attn_kw=dict(t=t_att, tq=tq_att, dk=dk, lam_init=lam_init),
            decode_kw=dict(n_pages_step=pages_step, n_heads=n_heads, dv=dv, lam_init=lam_init))

        yp = _mix_prompt(att, proj, pool_w_b, pool_s, w_out_b, yp, seq=seq, tm=tm_p)
        yp = _ffn(yp, row2(ffn2_norm[l]), *ffn_w[1], fin, tm=tm_ffn, tf=tf, final=last,
                  name="ffn2_prompt")
        outs["kp"].append(proj[PROJ_K].reshape(batch, seq, n_heads, dv))
        outs["vp"].append(proj[PROJ_V].reshape(batch, seq, n_heads, dv))
        outs["pp"].append(
            proj[PROJ_POOL].reshape(batch, seq, pool_width)[:, seq - (POOL_HALO - 1):])

        state16 = jnp.pad(state_pool[l], ((0, 0), (1, 0), (0, 0)))
        ys = _mix_sample(att_s.reshape(db, width), xps, state16, pool_w_b, pool_s, w_out_b, ys,
                         pos0=n_past)
        ys = _ffn(ys, row2(ffn2_norm[l]), *ffn_w[1], fin, tm=db, tf=tf, final=last,
                  name="ffn2_sample")
        outs["ks"].append(ks.reshape(db, sd, n_heads, dv))
        outs["vs"].append(vs.reshape(db, sd, n_heads, dv))
        outs["ps"].append(jnp.concatenate([state_pool[l][:, 1:], xps[:, None, :]], axis=1))

    st = lambda name: jnp.stack(outs[name], axis=0)
    return (yp.reshape(batch, seq, d), ys.reshape(db, sd, d), st("kp"), st("vp"), st("pp"),
            st("ks"), st("vs"), st("ps"))
```

```python
import functools
import math
from typing import NamedTuple

import jax
import jax.numpy as jnp
from jax import lax
from jax.experimental import pallas as pl
from jax.experimental.pallas import tpu as pltpu

F32 = jnp.float32
BF16 = jnp.bfloat16

V7X_LANES = 128
V7X_VMEM_BYTES = 64 * 1024 * 1024
V7X_VMEM_CAP = V7X_VMEM_BYTES - 8 * 1024 * 1024

EPS = 1e-6
SUBLN_EPS = 1e-5
NEG = -1e30
POOL_WINDOWS = (2, 4, 8, 16)
POOL_HALO = 16
NUM_BUCKETS = 32
MAX_EXACT = NUM_BUCKETS // 2
MAX_DISTANCE = 128
LOG2E = math.log2(math.e)
PAGE_SIZE = 128
DECODE_GROUP = 4


def _vmem_limit(*nbytes):
    est = int(sum(nbytes))
    return min(max(est, 16 * 1024 * 1024), V7X_VMEM_CAP)


def _rms(x, g, eps):
    return x * lax.rsqrt(jnp.mean(x * x, axis=-1, keepdims=True) + eps) * g


def _bucket(n):
    n = jnp.maximum(n, 0)
    nf = jnp.maximum(n, 1).astype(F32)
    large = MAX_EXACT + (jnp.log(nf / MAX_EXACT) / math.log(MAX_DISTANCE / MAX_EXACT)
                         * (NUM_BUCKETS - MAX_EXACT)).astype(jnp.int32)
    large = jnp.minimum(large, NUM_BUCKETS - 1)
    return jnp.where(n < MAX_EXACT, n, large)


def _ffn_kernel(x_ref, g_ref, wg_ref, wu_ref, wd_ref, fg_ref, o_ref, h_scr, *, n_f, last_tf,
                final):
    f = pl.program_id(1)
    tf = wg_ref.shape[1]

    @pl.when(f == 0)
    def _():
        h_scr[...] = _rms(x_ref[...], g_ref[...], EPS).astype(BF16)
        o_ref[...] = jnp.zeros(o_ref.shape, F32)

    def accumulate(cols):
        h = h_scr[...]
        gate = jnp.dot(h, wg_ref[:, 0:cols], preferred_element_type=F32)
        up = jnp.dot(h, wu_ref[:, 0:cols], preferred_element_type=F32)
        act = (gate / (1.0 + jnp.exp(-gate)) * up).astype(BF16)
        o_ref[...] += jnp.dot(act, wd_ref[0:cols, :], preferred_element_type=F32)

    if last_tf == tf:
        accumulate(tf)
    else:
        @pl.when(f < n_f - 1)
        def _():
            accumulate(tf)

        @pl.when(f == n_f - 1)
        def _():
            accumulate(last_tf)

    @pl.when(f == n_f - 1)
    def _():
        y = x_ref[...] + 0.5 * o_ref[...]
        if final:
            y = _rms(y, fg_ref[...], EPS)
        o_ref[...] = y


def _ffn(x, norm_g, wg, wu, wd, final_g, *, tm, tf, final, name):
    m, d = x.shape
    f_dim = wg.shape[1]
    n_f = pl.cdiv(f_dim, tf)
    last_tf = f_dim - (n_f - 1) * tf
    assert last_tf % V7X_LANES == 0
    vmem = _vmem_limit(2 * 2 * tm * d * 4,
                       2 * 3 * d * tf * 2,
                       tm * d * 2,
                       4 * tm * tf * 4 + tm * d * 4)
    return pl.pallas_call(
        functools.partial(_ffn_kernel, n_f=n_f, last_tf=last_tf, final=final),
        grid=(m // tm, n_f),
        in_specs=[
            pl.BlockSpec((tm, d), lambda i, f: (i, 0)),
            pl.BlockSpec((1, d), lambda i, f: (0, 0)),
            pl.BlockSpec((d, tf), lambda i, f: (0, f)),
            pl.BlockSpec((d, tf), lambda i, f: (0, f)),
            pl.BlockSpec((tf, d), lambda i, f: (f, 0)),
            pl.BlockSpec((1, d), lambda i, f: (0, 0)),
        ],
        out_specs=pl.BlockSpec((tm, d), lambda i, f: (i, 0)),
        out_shape=jax.ShapeDtypeStruct((m, d), F32),
        scratch_shapes=[pltpu.VMEM((tm, d), BF16)],
        compiler_params=pltpu.CompilerParams(
            dimension_semantics=("arbitrary", "arbitrary"), vmem_limit_bytes=vmem),
        name=name,
    )(x, norm_g, wg, wu, wd, final_g)


PROJ_Q, PROJ_K, PROJ_V, PROJ_POOL = range(4)
QX_Q, QX_POOL = range(2)


def _proj_kernel(x_ref, g_ref, w_ref, qx_ref, k_ref, v_ref, h_scr, *, scale):
    j = pl.program_id(1)

    @pl.when(j == 0)
    def _():
        h_scr[...] = _rms(x_ref[...], g_ref[...], EPS).astype(BF16)

    p = jnp.dot(h_scr[...], w_ref[...], preferred_element_type=F32)

    @pl.when(j == PROJ_K)
    def _():
        k_ref[...] = p

    @pl.when(j == PROJ_V)
    def _():
        v_ref[...] = p

    @pl.when((j == PROJ_Q) | (j == PROJ_POOL))
    def _():
        qx_ref[...] = p * jnp.where(j == PROJ_Q, scale, 1.0)


def _proj(x, norm_g, w_in, *, tm, scale, name):
    m, d = x.shape
    w = w_in.shape[1] // 4
    kv_spec = pl.BlockSpec((tm, w), lambda i, j: (i, 0))
    vmem = _vmem_limit(2 * tm * d * 4, 2 * d * w * 2, 3 * 2 * tm * w * 4, tm * d * 2,
                       2 * tm * w * 4)
    return pl.pallas_call(
        functools.partial(_proj_kernel, scale=scale),
        grid=(m // tm, 4),
        in_specs=[
            pl.BlockSpec((tm, d), lambda i, j: (i, 0)),
            pl.BlockSpec((1, d), lambda i, j: (0, 0)),
            pl.BlockSpec((d, w), lambda i, j: (0, j)),
        ],
        out_specs=[pl.BlockSpec((None, tm, w),
                                lambda i, j: (jnp.where(j == PROJ_POOL, QX_POOL, QX_Q), i, 0)),
                   kv_spec, kv_spec],
        out_shape=[jax.ShapeDtypeStruct((2, m, w), F32)] + [jax.ShapeDtypeStruct((m, w), F32)] * 2,
        scratch_shapes=[pltpu.VMEM((tm, d), BF16)],
        compiler_params=pltpu.CompilerParams(
            dimension_semantics=("arbitrary", "arbitrary"), vmem_limit_bytes=vmem),
        name=name,
    )(x, norm_g, w_in)


def _bias_tile_kernel(tab_ref, o_ref, *, tq):
    h = pl.program_id(0)
    r = lax.broadcasted_iota(jnp.int32, (2 * tq, 2 * tq), 0)
    c = lax.broadcasted_iota(jnp.int32, (2 * tq, 2 * tq), 1)
    key = r - tq
    qry = c % tq
    far = tab_ref[h * NUM_BUCKETS + NUM_BUCKETS - 1]
    bucket = _bucket(qry - key)
    val = jnp.zeros(r.shape, F32)
    for b in range(NUM_BUCKETS - 1):
        val = jnp.where(bucket == b, (tab_ref[h * NUM_BUCKETS + b] - far) * LOG2E, val)
    o_ref[...] = jnp.where(key <= qry, val, NEG)


def _bias_tiles(rel_bias, *, tq):
    n_heads = rel_bias.shape[1]
    tab = rel_bias.T.reshape(-1)
    return pl.pallas_call(
        functools.partial(_bias_tile_kernel, tq=tq),
        grid=(n_heads,),
        in_specs=[pl.BlockSpec(memory_space=pltpu.SMEM)],
        out_specs=pl.BlockSpec((None, 2 * tq, 2 * tq), lambda h: (h, 0, 0)),
        out_shape=jax.ShapeDtypeStruct((n_heads, 2 * tq, 2 * tq), F32),
        compiler_params=pltpu.CompilerParams(
            dimension_semantics=("arbitrary",),
            vmem_limit_bytes=_vmem_limit(16 * 2 * tq * 2 * tq * 4)),
        name="rel_bias_tiles",
    )(tab)


def _lambda(lq1_ref, lk1_ref, lq2_ref, lk2_ref, lam_init):
    a = jnp.sum(lq1_ref[...] * lk1_ref[...], axis=-1, keepdims=True)
    b = jnp.sum(lq2_ref[...] * lk2_ref[...], axis=-1, keepdims=True)
    return jnp.exp(a) - jnp.exp(b) + lam_init


def _attn_body(qi, lq1_ref, lk1_ref, lq2_ref, lk2_ref, g_ref, q_ref, k_ref, v_ref, bias_ref,
               o_ref, kb_scr, vt_scr, m_scr, l_scr, acc_scr, *, t, tq, dk, lam_init):
    n_strips = t // tq
    seq = k_ref.shape[0]
    sw = 2 * tq

    @pl.when(qi == 0)
    def _():
        kb_scr[0:tq, :] = jnp.zeros((tq, kb_scr.shape[1]), BF16)
        vt_scr[0] = jnp.zeros(vt_scr.shape[1:], BF16)
        kb_scr[tq:, :] = k_ref[...].astype(BF16)
        for c in range(seq // tq):
            vt_scr[c + 1] = v_ref[c * tq:(c + 1) * tq, :].T.astype(BF16)

    qa = q_ref[...].astype(F32)
    lane = lax.broadcasted_iota(jnp.int32, qa.shape, 1)
    q1 = jnp.where(lane < dk, qa, 0.0).astype(BF16)
    q2 = jnp.where(lane >= dk, qa, 0.0).astype(BF16)
    strips = [jnp.concatenate([q1[u * tq:(u + 1) * tq], q2[u * tq:(u + 1) * tq]], axis=0)
              for u in range(n_strips)]
    qs_all = jnp.concatenate(strips, axis=0)

    m_scr[...] = jnp.full(m_scr.shape, NEG, F32)
    l_scr[...] = jnp.zeros(l_scr.shape, F32)
    acc_scr[...] = jnp.zeros(acc_scr.shape, F32)

    def keys(row0, n):
        return kb_scr[pl.ds(pl.multiple_of(row0, tq), n), :]

    def values_t(chunk0, n_chunks):
        return jnp.concatenate([vt_scr[chunk0 + i] for i in range(n_chunks)], axis=1)

    def fold(lanes, pieces):
        m_prev = m_scr[:, lanes]
        m_new = m_prev
        for s, _ in pieces:
            m_new = jnp.maximum(m_new, jnp.max(s, axis=0, keepdims=True))
        alpha = jnp.exp2(m_prev - m_new)
        l_new = alpha * l_scr[:, lanes]
        pv = None
        for s, vt in pieces:
            p = jnp.exp2(s - m_new)
            l_new = l_new + jnp.sum(p, axis=0, keepdims=True)
            term = jnp.dot(vt, p.astype(BF16), preferred_element_type=F32)
            pv = term if pv is None else pv + term
        l_scr[:, lanes] = l_new
        acc_scr[:, lanes] = alpha * acc_scr[:, lanes] + pv
        m_scr[:, lanes] = m_new

    def scores(kc, qs):
        return lax.dot_general(kc, qs, (((1,), (1,)), ((), ())), preferred_element_type=F32)

    far_chunks = t // tq
    every = slice(0, n_strips * sw)

    def fold_far(blocks):
        scored = []
        for chunk0, n_chunks in blocks:
            kc = keys(chunk0 * tq, n_chunks * tq)
            scored.append((scores(kc, qs_all), values_t(chunk0, n_chunks)))
        for piece in scored:
            fold(every, [piece])

    n_far_blocks = jnp.maximum(qi - 1, 0)
    n_pairs = n_far_blocks // 2

    def far_body(j, carry):
        c0 = 1 + j * 2 * far_chunks
        fold_far([(c0, far_chunks), (c0 + far_chunks, far_chunks)])
        return carry

    lax.fori_loop(0, n_pairs, far_body, 0)
    tail0 = 1 + n_pairs * 2 * far_chunks

    @pl.when((qi > 0) & (n_far_blocks % 2 == 0))
    def _():
        fold_far([(tail0, far_chunks - 1)])

    @pl.when(n_far_blocks % 2 == 1)
    def _():
        fold_far([(tail0, far_chunks), (tail0 + far_chunks, far_chunks - 1)])

    n_absent = jnp.where(qi == 0, tq, 0)
    pending = []
    near_ahead = 1
    for u in range(n_strips):
        lanes = slice(u * sw, (u + 1) * sw)
        qs = strips[u]
        pieces = []
        if u > 0:
            s_a = scores(keys(qi * t, u * tq), qs)
            row = lax.broadcasted_iota(jnp.int32, s_a.shape, 0)
            s_a = jnp.where(row < n_absent, NEG, s_a)
            pieces.append((s_a, values_t(qi * far_chunks, u)))
        bias = bias_ref[...]
        if u == 0:
            row = lax.broadcasted_iota(jnp.int32, bias.shape, 0)
            bias = jnp.where(row < n_absent, NEG, bias)
        s_b = scores(keys(qi * t + u * tq, 2 * tq), qs) + bias
        pieces.append((s_b, values_t(qi * far_chunks + u, 2)))
        pending.append((lanes, pieces))
        if len(pending) > near_ahead:
            fold(*pending.pop(0))
    for item in pending:
        fold(*item)

    o = acc_scr[...] / l_scr[...]
    lam = _lambda(lq1_ref, lk1_ref, lq2_ref, lk2_ref, lam_init)
    for u in range(n_strips):
        w = o[:, u * sw:u * sw + tq] - lam * o[:, u * sw + tq:(u + 1) * sw]
        y = (w * lax.rsqrt(jnp.mean(w * w, axis=0, keepdims=True) + SUBLN_EPS)
             * g_ref[...] * (1.0 - lam_init))
        o_ref[u * tq:(u + 1) * tq, :] = y.T.astype(o_ref.dtype)


class _Operands(NamedTuple):
    args: tuple
    in_specs: list
    out_spec: pl.BlockSpec
    out_shape: jax.ShapeDtypeStruct
    scratch: list
    vmem_bytes: int


def _attn_prompt_operands(qx, k, v, bias, lams, subln_g, *, seq, n_heads, t, tq, index):
    dv = k.shape[1] // n_heads
    nq = seq // t
    assert t % tq == 0 and tq >= MAX_DISTANCE
    at = lambda f: (lambda *g: f(*index(*g)))
    g_cols = jnp.broadcast_to(subln_g.reshape(dv, 1), (dv, tq))
    lam_spec = pl.BlockSpec((1, dv // 2), at(lambda h, b, i: (0, 0)))
    kv_spec = pl.BlockSpec((seq, dv), at(lambda h, b, i: (b, h)))
    return _Operands(
        args=(*lams, g_cols, qx, k, v, bias),
        in_specs=[lam_spec, lam_spec, lam_spec, lam_spec,
                  pl.BlockSpec((dv, tq), at(lambda h, b, i: (0, 0))),
                  pl.BlockSpec((None, t, dv), at(lambda h, b, i: (QX_Q, b * nq + i, h))),
                  kv_spec, kv_spec,
                  pl.BlockSpec((None, 2 * tq, 2 * tq), at(lambda h, b, i: (h, 0, 0)))],
        out_spec=pl.BlockSpec((t, dv), at(lambda h, b, i: (b * nq + i, h))),
        out_shape=jax.ShapeDtypeStruct(k.shape, BF16),
        scratch=[pltpu.VMEM((seq + tq, dv), BF16), pltpu.VMEM((seq // tq + 1, dv, tq), BF16),
                 pltpu.VMEM((1, 2 * t), F32), pltpu.VMEM((1, 2 * t), F32),
                 pltpu.VMEM((dv, 2 * t), F32)],
        vmem_bytes=(2 * 2 * seq * dv * 4 + 2 * (seq + tq) * dv * 2 + 2 * 2 * tq * 2 * tq * 4
                    + 4 * t * dv * 2 + dv * 2 * t * 4 + 6 * t * 2 * t * 4))


def _page_copies(pt_ref, ck_ref, cv_ref, kbuf, vbuf, sems, step, *, n_c, p_n):
    seq, c, slot = lax.div(step, n_c), lax.rem(step, n_c), lax.rem(step, 2)
    copies = []
    for j in range(p_n):
        page = pt_ref[seq, c * p_n + j]
        copies.append(pltpu.make_async_copy(ck_ref.at[page], kbuf.at[slot, j], sems.at[0, slot]))
        copies.append(pltpu.make_async_copy(cv_ref.at[page], vbuf.at[slot, j], sems.at[1, slot]))
    return copies


def _decode_body(step, n_steps, n_c, pt_ref, lq1_ref, lk1_ref, lq2_ref, lk2_ref, g_ref, tab_ref,
                 q_ref, kn_ref, vn_ref, ck_ref, cv_ref, o_ref, qm_scr, bias_scr, m_scr, l_scr,
                 acc_scr, kbuf, vbuf, sems, *, n_pages_step, n_heads, dv, lam_init):
    p_n = n_pages_step
    c = lax.rem(step, n_c)
    slot = lax.rem(step, 2)
    rows = 2 * n_heads
    page_cols = PAGE_SIZE * n_heads
    copies = functools.partial(_page_copies, pt_ref, ck_ref, cv_ref, kbuf, vbuf, sems,
                               n_c=n_c, p_n=p_n)

    @pl.when(step == 0)
    def _():
        for cp in copies(step):
            cp.start()

    @pl.when(step + 1 < n_steps)
    def _():
        for cp in copies(step + 1):
            cp.start()

    for cp in copies(step):
        cp.wait()
    k_refs = [kbuf.at[slot, j] for j in range(p_n)]
    v_refs = [vbuf.at[slot, j] for j in range(p_n)]

    @pl.when(c == 0)
    def _():
        q = q_ref[...].astype(F32)
        lane = lax.broadcasted_iota(jnp.int32, q.shape, 1)
        qm_scr[...] = jnp.concatenate(
            [jnp.where(lane < dv // 2, q, 0.0), jnp.where(lane >= dv // 2, q, 0.0)],
            axis=0).astype(BF16)
        key_row = lax.broadcasted_iota(jnp.int32, (1, page_cols), 1) // n_heads
        bucket = _bucket(PAGE_SIZE - key_row)
        val = jnp.zeros((rows, page_cols), F32)
        for b in range(NUM_BUCKETS - 1):
            val = jnp.where(bucket == b, tab_ref[:, b:b + 1], val)
        bias_scr[...] = val
        m_scr[...] = jnp.full(m_scr.shape, NEG, F32)
        l_scr[...] = jnp.zeros(l_scr.shape, F32)
        acc_scr[...] = jnp.zeros(acc_scr.shape, F32)

    qm = qm_scr[...]
    row = lax.broadcasted_iota(jnp.int32, (rows, page_cols), 0)
    col = lax.broadcasted_iota(jnp.int32, (rows, page_cols), 1)
    own = (col % n_heads) == (row % n_heads)
    near = jnp.where(c == n_c - 1, bias_scr[...], 0.0)
    pieces = []
    for j in range(p_n):
        kf = k_refs[j][...].reshape(page_cols, dv).astype(BF16)
        sj = lax.dot_general(qm, kf, (((1,), (1,)), ((), ())), preferred_element_type=F32)
        if j == p_n - 1:
            sj = sj + near
        pieces.append(jnp.where(own, sj, NEG))

    def update(s, pv_fn):
        m_prev = m_scr[...]
        m_new = jnp.maximum(m_prev, jnp.max(s, axis=-1, keepdims=True))
        p = jnp.exp2(s - m_new)
        alpha = jnp.exp2(m_prev - m_new)
        l_scr[...] = alpha * l_scr[...] + jnp.sum(p, axis=-1, keepdims=True)
        acc_scr[...] = alpha * acc_scr[...] + pv_fn(p)
        m_scr[...] = m_new

    def pv_pages(first):
        def pv(p):
            pb = p.astype(BF16)
            out = None
            for j in range(DECODE_GROUP):
                vf = v_refs[first + j][...].reshape(page_cols, dv).astype(BF16)
                term = jnp.dot(pb[:, j * page_cols:(j + 1) * page_cols], vf,
                               preferred_element_type=F32)
                out = term if out is None else out + term
            return out
        return pv

    for first in range(0, p_n, DECODE_GROUP):
        update(jnp.concatenate(pieces[first:first + DECODE_GROUP], axis=1), pv_pages(first))

    @pl.when(c == n_c - 1)
    def _():
        kn = jnp.concatenate([kn_ref[...]] * 2, axis=0)
        vn = jnp.concatenate([vn_ref[...]] * 2, axis=0)
        s_self = jnp.sum(qm.astype(F32) * kn, axis=-1, keepdims=True) + tab_ref[:, 0:1]
        update(s_self, lambda p: p * vn)
        o = acc_scr[...] / l_scr[...]
        lam = _lambda(lq1_ref, lk1_ref, lq2_ref, lk2_ref, lam_init)
        w = o[0:n_heads] - lam * o[n_heads:rows]
        o_ref[...] = _rms(w, g_ref[...], SUBLN_EPS) * (1.0 - lam_init)


def _attn_decode_operands(q, k_new, v_new, cache_k, cache_v, tab, lams, subln_g, *,
                          n_pages_step, index):
    db, n_heads, dv = q.shape
    p_n = n_pages_step
    rows = 2 * n_heads
    page_cols = PAGE_SIZE * n_heads
    assert PAGE_SIZE >= MAX_DISTANCE and p_n % DECODE_GROUP == 0
    at = lambda f: (lambda *g: f(index(*g)))
    small = lambda shape: pl.BlockSpec(shape, at(lambda s: (0,) * len(shape)))
    row_spec = pl.BlockSpec((None, n_heads, dv), at(lambda s: (s, 0, 0)))
    hbm = pl.BlockSpec(memory_space=pl.ANY)
    slots = (2, p_n, PAGE_SIZE, n_heads, dv)
    return _Operands(
        args=(*lams, subln_g, tab, q, k_new, v_new, cache_k, cache_v),
        in_specs=[small((1, dv // 2))] * 4 + [small((1, dv)), small((rows, NUM_BUCKETS)),
                                              row_spec, row_spec, row_spec, hbm, hbm],
        out_spec=row_spec,
        out_shape=jax.ShapeDtypeStruct((db, n_heads, dv), F32),
        scratch=[pltpu.VMEM((rows, dv), BF16), pltpu.VMEM((rows, page_cols), F32),
                 pltpu.VMEM((rows, 1), F32), pltpu.VMEM((rows, 1), F32),
                 pltpu.VMEM((rows, dv), F32), pltpu.VMEM(slots, F32), pltpu.VMEM(slots, F32),
                 pltpu.SemaphoreType.DMA((2, 2))],
        vmem_bytes=(2 * 2 * p_n * page_cols * dv * 4
                    + 2 * 2 * page_cols * dv * 2
                    + 6 * rows * p_n * page_cols * 4))


def _attention(prompt, decode, page_table, *, grid_prompt, n_chunks, attn_kw, decode_kw):
    n_heads, batch, nq = grid_prompt

    def step_of(h, b, i):
        return (h * batch + b) * nq + i

    n_seq = page_table.shape[0]
    n_steps = n_seq * n_chunks
    if n_heads * batch * nq == n_steps:
        pr = prompt(index=lambda h, b, i, pt: (h, b, i))
        de = decode(index=lambda h, b, i, pt: lax.div(step_of(h, b, i), n_chunks))
        n_pr, n_de = len(pr.args), len(de.args)

        def fused(pt_ref, *refs):
            ins, outs, scr = refs[:n_pr + n_de], refs[n_pr + n_de:n_pr + n_de + 2], \
                refs[n_pr + n_de + 2:]
            h, b, i = pl.program_id(0), pl.program_id(1), pl.program_id(2)
            _decode_body(step_of(h, b, i), n_steps, n_chunks, pt_ref, *ins[n_pr:], outs[1],
                         *scr[len(pr.scratch):], **decode_kw)
            _attn_body(i, *ins[:n_pr], outs[0], *scr[:len(pr.scratch)], **attn_kw)

        return pl.pallas_call(
            fused,
            grid_spec=pltpu.PrefetchScalarGridSpec(
                num_scalar_prefetch=1, grid=grid_prompt,
                in_specs=pr.in_specs + de.in_specs, out_specs=[pr.out_spec, de.out_spec],
                scratch_shapes=pr.scratch + de.scratch),
            out_shape=[pr.out_shape, de.out_shape],
            compiler_params=pltpu.CompilerParams(
                dimension_semantics=("arbitrary",) * 3,
                vmem_limit_bytes=_vmem_limit(pr.vmem_bytes, de.vmem_bytes)),
            name="attn_prompt_decode",
        )(page_table, *pr.args, *de.args)

    pr = prompt(index=lambda h, b, i: (h, b, i))
    att = pl.pallas_call(
        lambda *refs: _attn_body(pl.program_id(2), *refs, **attn_kw),
        grid=grid_prompt, in_specs=pr.in_specs, out_specs=pr.out_spec, out_shape=pr.out_shape,
        scratch_shapes=pr.scratch,
        compiler_params=pltpu.CompilerParams(
            dimension_semantics=("arbitrary",) * 3, vmem_limit_bytes=_vmem_limit(pr.vmem_bytes)),
        name="attn_prompt",
    )(*pr.args)
    de = decode(index=lambda s, c, pt: s)
    att_s = pl.pallas_call(
        lambda pt_ref, *refs: _decode_body(pl.program_id(0) * n_chunks + pl.program_id(1),
                                           n_steps, n_chunks, pt_ref, *refs, **decode_kw),
        grid_spec=pltpu.PrefetchScalarGridSpec(
            num_scalar_prefetch=1, grid=(n_seq, n_chunks), in_specs=de.in_specs,
            out_specs=de.out_spec, scratch_shapes=de.scratch),
        out_shape=de.out_shape,
        compiler_params=pltpu.CompilerParams(
            dimension_semantics=("arbitrary",) * 2, vmem_limit_bytes=_vmem_limit(de.vmem_bytes)),
        name="attn_decode",
    )(page_table, *de.args)
    return att, att_s


def _pool_project(diffs, pw_ref, ps_ref):
    outs = [jnp.dot(d.astype(BF16), pw_ref[g], preferred_element_type=F32)
            for g, d in enumerate(diffs)]
    return (jnp.concatenate(outs, axis=1) * ps_ref[...]).astype(BF16)


def _mix_kernel(att_ref, xp_ref, halo_ref, pw_ref, ps_ref, wo_ref, y_ref, o_ref, full_scr,
                *, tm, tiles_per_seq):
    t_in_seq = lax.rem(pl.program_id(0), tiles_per_seq)
    att_w = att_ref.shape[1]
    gc = xp_ref.shape[1] // len(POOL_WINDOWS)
    out = y_ref[...] + jnp.dot(att_ref[...], wo_ref[0:att_w, :], preferred_element_type=F32)
    full_scr[0:POOL_HALO, :] = jnp.where(t_in_seq == 0, 0.0, halo_ref[...])
    full_scr[POOL_HALO:, :] = xp_ref[...]
    pos = t_in_seq * tm + lax.broadcasted_iota(jnp.int32, (tm, 1), 0)
    diffs = []
    for g, win in enumerate(POOL_WINDOWS):
        cols = slice(g * gc, (g + 1) * gc)
        x = xp_ref[:, cols]
        acc = x
        for d in range(1, win):
            acc = acc + full_scr[POOL_HALO - d:POOL_HALO - d + tm, cols]
        cnt = jnp.minimum(pos + 1, win).astype(F32)
        diffs.append(acc / cnt - x)
    pool = _pool_project(diffs, pw_ref, ps_ref)
    o_ref[...] = out + jnp.dot(pool, wo_ref[att_w:, :], preferred_element_type=F32)


def _mix_prompt(att, qx, pool_w, pool_scale, w_out, y, *, seq, tm):
    m, d = y.shape
    aw, pw = att.shape[1], qx.shape[2]
    n_g, gc = pool_w.shape[0], pool_w.shape[1]
    tiles_per_seq = seq // tm
    halo_blocks = tm // POOL_HALO
    vmem = _vmem_limit(2 * tm * (aw * 2 + pw * 4 + 2 * d * 4), 2 * POOL_HALO * pw * 4,
                       2 * (aw + pw) * d * 2, 2 * n_g * gc * gc * 2,
                       (tm + POOL_HALO) * pw * 4, 4 * tm * pw * 4)
    return pl.pallas_call(
        functools.partial(_mix_kernel, tm=tm, tiles_per_seq=tiles_per_seq),
        grid=(m // tm,),
        in_specs=[
            pl.BlockSpec((tm, aw), lambda i: (i, 0)),
            pl.BlockSpec((None, tm, pw), lambda i: (QX_POOL, i, 0)),
            pl.BlockSpec((None, POOL_HALO, pw),
                         lambda i: (QX_POOL, jnp.maximum(i * halo_blocks - 1, 0), 0)),
            pl.BlockSpec((n_g, gc, gc), lambda i: (0, 0, 0)),
            pl.BlockSpec((1, pw), lambda i: (0, 0)),
            pl.BlockSpec((aw + pw, d), lambda i: (0, 0)),
            pl.BlockSpec((tm, d), lambda i: (i, 0)),
        ],
        out_specs=pl.BlockSpec((tm, d), lambda i: (i, 0)),
        out_shape=jax.ShapeDtypeStruct((m, d), F32),
        scratch_shapes=[pltpu.VMEM((tm + POOL_HALO, pw), F32)],
        compiler_params=pltpu.CompilerParams(
            dimension_semantics=("arbitrary",), vmem_limit_bytes=vmem),
        name="mix_prompt",
    )(att, qx, qx, pool_w, pool_scale, w_out, y)


def _mix_sample_kernel(att_ref, xp_ref, st_ref, pw_ref, ps_ref, wo_ref, y_ref, o_ref, *, pos0):
    att_w = att_ref.shape[1]
    gc = xp_ref.shape[1] // len(POOL_WINDOWS)
    row = lax.broadcasted_iota(jnp.int32, (1, POOL_HALO, 1), 1)
    diffs = []
    for g, win in enumerate(POOL_WINDOWS):
        cols = slice(g * gc, (g + 1) * gc)
        x = xp_ref[:, cols]
        prev = jnp.where(row >= POOL_HALO - (win - 1), st_ref[:, :, cols], 0.0)
        acc = x + jnp.sum(prev, axis=1)
        diffs.append(acc / float(min(pos0 + 1, win)) - x)
    pool = _pool_project(diffs, pw_ref, ps_ref)
    o_ref[...] = (y_ref[...]
                  + jnp.dot(att_ref[...].astype(BF16), wo_ref[0:att_w, :],
                            preferred_element_type=F32)
                  + jnp.dot(pool, wo_ref[att_w:, :], preferred_element_type=F32))


def _mix_sample(att, xp, state16, pool_w, pool_scale, w_out, y, *, pos0):
    whole = lambda a: pl.BlockSpec(a.shape, lambda: (0,) * a.ndim)
    args = (att, xp, state16, pool_w, pool_scale, w_out, y)
    return pl.pallas_call(
        functools.partial(_mix_sample_kernel, pos0=pos0),
        in_specs=[whole(a) for a in args],
        out_specs=whole(y),
        out_shape=jax.ShapeDtypeStruct(y.shape, F32),
        compiler_params=pltpu.CompilerParams(
            vmem_limit_bytes=_vmem_limit(*(2 * a.size * a.dtype.itemsize for a in args),
                                         4 * state16.size * 4)),
        name="mix_sample",
    )(*args)


def kernel(x_prompt, x_sample, cache_k, cache_v, state_pool, page_table, w_in, w_out, ffn1_norm,
           ffn1_gate, ffn1_up, ffn1_down, mix_norm, ffn2_norm, ffn2_gate, ffn2_up, ffn2_down,
           lambda_q1, lambda_k1, lambda_q2, lambda_k2, subln_g, pool_w, pool_scale, rel_bias,
           final_norm):
    batch, seq, d = x_prompt.shape
    db, sd, _ = x_sample.shape
    depth, n_phys, page, n_heads, dv = cache_v.shape
    dk = dv // 2
    width = n_heads * dv
    pool_width = state_pool.shape[-1]
    assert sd == 1 and page == PAGE_SIZE and cache_k.shape[-1] == 2 * dk
    assert w_in.shape[-1] == 4 * width and pool_width == width
    assert state_pool.shape[2] == POOL_HALO - 1 and rel_bias.shape[0] == NUM_BUCKETS
    n_past = page_table.shape[1] * PAGE_SIZE
    scale = dk ** -0.5 * LOG2E

    tm_p = min(512, batch * seq)
    tm_ffn = min(1024, batch * seq)
    tf = 512
    t_att = min(512, seq)
    tq_att = 128
    pages_step = min(16, page_table.shape[1])
    assert seq % tm_p == 0 and seq % t_att == 0 and page_table.shape[1] % pages_step == 0

    row2 = lambda a: a.reshape(1, -1)
    yp = x_prompt.reshape(batch * seq, d)
    ys = x_sample.reshape(db * sd, d)
    bias = _bias_tiles(rel_bias, tq=tq_att)
    far = rel_bias[NUM_BUCKETS - 1]
    tab_dec = jnp.tile((rel_bias - far[None, :]).T * LOG2E, (2, 1))
    outs = {k: [] for k in ("kp", "vp", "pp", "ks", "vs", "ps")}

    for l in range(depth):
        lam_init = 0.8 - 0.6 * math.exp(-0.3 * l)
        lams = tuple(row2(a[l]) for a in (lambda_q1, lambda_k1, lambda_q2, lambda_k2))
        ffn_w = []
        for gate, up, down in ((ffn1_gate, ffn1_up, ffn1_down), (ffn2_gate, ffn2_up, ffn2_down)):
            ffn_w.append((gate[l].astype(BF16), up[l].astype(BF16), down[l].astype(BF16)))
        w_in_b = w_in[l].astype(BF16)
        w_out_b = w_out[l].astype(BF16)
        pool_w_b = pool_w[l].astype(BF16)
        pool_s = row2(pool_scale[l])
        last = l == depth - 1
        fin = row2(final_norm)

        yp = _ffn(yp, row2(ffn1_norm[l]), *ffn_w[0], fin, tm=tm_ffn, tf=tf, final=False,
                  name="ffn1_prompt")
        qx, k, v = _proj(yp, row2(mix_norm[l]), w_in_b, tm=tm_p, scale=scale,
                         name="proj_prompt")
        ys = _ffn(ys, row2(ffn1_norm[l]), *ffn_w[0], fin, tm=db, tf=tf, final=False,
                  name="ffn1_sample")
        qxs, ks, vs = _proj(ys, row2(mix_norm[l]), w_in_b, tm=db, scale=scale,
                            name="proj_sample")
        qs, xps = qxs[QX_Q], qxs[QX_POOL]

        att, att_s = _attention(
            functools.partial(_attn_prompt_operands, qx, k, v, bias, lams, row2(subln_g[l]),
                              seq=seq, n_heads=n_heads, t=t_att, tq=tq_att),
            functools.partial(_attn_decode_operands, qs.reshape(db, n_heads, dv),
                              ks.reshape(db, n_heads, dv), vs.reshape(db, n_heads, dv),
                              cache_k[l], cache_v[l], tab_dec, lams, row2(subln_g[l]),
                              n_pages_step=pages_step),
            page_table, grid_prompt=(n_heads, batch, seq // t_att),
            n_chunks=page_table.shape[1] // pages_step,
            attn_kw=dict(t=t_att, tq=tq_att, dk=dk, lam_init=lam_init),
            decode_kw=dict(n_pages_step=pages_step, n_heads=n_heads, dv=dv, lam_init=lam_init))

        yp = _mix_prompt(att, qx, pool_w_b, pool_s, w_out_b, yp, seq=seq, tm=tm_p)
        yp = _ffn(yp, row2(ffn2_norm[l]), *ffn_w[1], fin, tm=tm_ffn, tf=tf, final=last,
                  name="ffn2_prompt")
        outs["kp"].append(k.reshape(batch, seq, n_heads, dv))
        outs["vp"].append(v.reshape(batch, seq, n_heads, dv))
        outs["pp"].append(
            qx[QX_POOL].reshape(batch, seq, pool_width)[:, seq - (POOL_HALO - 1):])

        state16 = jnp.pad(state_pool[l], ((0, 0), (1, 0), (0, 0)))
        ys = _mix_sample(att_s.reshape(db, width), xps, state16, pool_w_b, pool_s, w_out_b, ys,
                         pos0=n_past)
        ys = _ffn(ys, row2(ffn2_norm[l]), *ffn_w[1], fin, tm=db, tf=tf, final=last,
                  name="ffn2_sample")
        outs["ks"].append(ks.reshape(db, sd, n_heads, dv))
        outs["vs"].append(vs.reshape(db, sd, n_heads, dv))
        outs["ps"].append(jnp.concatenate([state_pool[l][:, 1:], xps[:, None, :]], axis=1))

    st = lambda name: jnp.stack(outs[name], axis=0)
    return (yp.reshape(batch, seq, d), ys.reshape(db, sd, d), st("kp"), st("vp"), st("pp"),
            st("ks"), st("vs"), st("ps"))
```

```python
import functools
import math
from typing import NamedTuple

import jax
import jax.numpy as jnp
from jax import lax
from jax.experimental import pallas as pl
from jax.experimental.pallas import tpu as pltpu

F32 = jnp.float32
BF16 = jnp.bfloat16

V7X_LANES = 128
V7X_VMEM_BYTES = 64 * 1024 * 1024
V7X_VMEM_CAP = V7X_VMEM_BYTES - 8 * 1024 * 1024

EPS = 1e-6
SUBLN_EPS = 1e-5
NEG = -1e30
POOL_WINDOWS = (2, 4, 8, 16)
POOL_HALO = 16
NUM_BUCKETS = 32
MAX_EXACT = NUM_BUCKETS // 2
MAX_DISTANCE = 128
LOG2E = math.log2(math.e)
PAGE_SIZE = 128
DECODE_GROUP = 4


def _vmem_limit(*nbytes):
    est = int(sum(nbytes))
    return min(max(est, 16 * 1024 * 1024), V7X_VMEM_CAP)


def _rms(x, g, eps):
    return x * lax.rsqrt(jnp.mean(x * x, axis=-1, keepdims=True) + eps) * g


def _bucket(n):
    n = jnp.maximum(n, 0)
    nf = jnp.maximum(n, 1).astype(F32)
    large = MAX_EXACT + (jnp.log(nf / MAX_EXACT) / math.log(MAX_DISTANCE / MAX_EXACT)
                         * (NUM_BUCKETS - MAX_EXACT)).astype(jnp.int32)
    large = jnp.minimum(large, NUM_BUCKETS - 1)
    return jnp.where(n < MAX_EXACT, n, large)


def _ffn_kernel(x_ref, g_ref, wg_ref, wu_ref, wd_ref, fg_ref, o_ref, h_scr, *, n_f, last_tf,
                final):
    f = pl.program_id(1)
    tf = wg_ref.shape[1]

    @pl.when(f == 0)
    def _():
        h_scr[...] = _rms(x_ref[...], g_ref[...], EPS).astype(BF16)
        o_ref[...] = jnp.zeros(o_ref.shape, F32)

    def accumulate(cols):
        h = h_scr[...]
        gate = jnp.dot(h, wg_ref[:, 0:cols], preferred_element_type=F32)
        up = jnp.dot(h, wu_ref[:, 0:cols], preferred_element_type=F32)
        act = (gate / (1.0 + jnp.exp(-gate)) * up).astype(BF16)
        o_ref[...] += jnp.dot(act, wd_ref[0:cols, :], preferred_element_type=F32)

    if last_tf == tf:
        accumulate(tf)
    else:
        @pl.when(f < n_f - 1)
        def _():
            accumulate(tf)

        @pl.when(f == n_f - 1)
        def _():
            accumulate(last_tf)

    @pl.when(f == n_f - 1)
    def _():
        y = x_ref[...] + 0.5 * o_ref[...]
        if final:
            y = _rms(y, fg_ref[...], EPS)
        o_ref[...] = y


def _ffn(x, norm_g, wg, wu, wd, final_g, *, tm, tf, final, name):
    m, d = x.shape
    f_dim = wg.shape[1]
    n_f = pl.cdiv(f_dim, tf)
    last_tf = f_dim - (n_f - 1) * tf
    assert last_tf % V7X_LANES == 0
    vmem = _vmem_limit(2 * 2 * tm * d * 4,
                       2 * 3 * d * tf * 2,
                       tm * d * 2,
                       4 * tm * tf * 4 + tm * d * 4)
    return pl.pallas_call(
        functools.partial(_ffn_kernel, n_f=n_f, last_tf=last_tf, final=final),
        grid=(m // tm, n_f),
        in_specs=[
            pl.BlockSpec((tm, d), lambda i, f: (i, 0)),
            pl.BlockSpec((1, d), lambda i, f: (0, 0)),
            pl.BlockSpec((d, tf), lambda i, f: (0, f)),
            pl.BlockSpec((d, tf), lambda i, f: (0, f)),
            pl.BlockSpec((tf, d), lambda i, f: (f, 0)),
            pl.BlockSpec((1, d), lambda i, f: (0, 0)),
        ],
        out_specs=pl.BlockSpec((tm, d), lambda i, f: (i, 0)),
        out_shape=jax.ShapeDtypeStruct((m, d), F32),
        scratch_shapes=[pltpu.VMEM((tm, d), BF16)],
        compiler_params=pltpu.CompilerParams(
            dimension_semantics=("arbitrary", "arbitrary"), vmem_limit_bytes=vmem),
        name=name,
    )(x, norm_g, wg, wu, wd, final_g)


PROJ_Q, PROJ_K, PROJ_V, PROJ_POOL = range(4)
QX_Q, QX_POOL = range(2)


def _proj_kernel(x_ref, g_ref, w_ref, qx_ref, k_ref, v_ref, h_scr, *, scale):
    j = pl.program_id(1)

    @pl.when(j == 0)
    def _():
        h_scr[...] = _rms(x_ref[...], g_ref[...], EPS).astype(BF16)

    p = jnp.dot(h_scr[...], w_ref[...], preferred_element_type=F32)

    @pl.when(j == PROJ_K)
    def _():
        k_ref[...] = p

    @pl.when(j == PROJ_V)
    def _():
        v_ref[...] = p

    @pl.when((j == PROJ_Q) | (j == PROJ_POOL))
    def _():
        qx_ref[...] = p * jnp.where(j == PROJ_Q, scale, 1.0)


def _proj(x, norm_g, w_in, *, tm, scale, name):
    m, d = x.shape
    w = w_in.shape[1] // 4
    kv_spec = pl.BlockSpec((tm, w), lambda i, j: (i, 0))
    vmem = _vmem_limit(2 * tm * d * 4, 2 * d * w * 2, 3 * 2 * tm * w * 4, tm * d * 2,
                       2 * tm * w * 4)
    return pl.pallas_call(
        functools.partial(_proj_kernel, scale=scale),
        grid=(m // tm, 4),
        in_specs=[
            pl.BlockSpec((tm, d), lambda i, j: (i, 0)),
            pl.BlockSpec((1, d), lambda i, j: (0, 0)),
            pl.BlockSpec((d, w), lambda i, j: (0, j)),
        ],
        out_specs=[pl.BlockSpec((None, tm, w),
                                lambda i, j: (jnp.where(j == PROJ_POOL, QX_POOL, QX_Q), i, 0)),
                   kv_spec, kv_spec],
        out_shape=[jax.ShapeDtypeStruct((2, m, w), F32)] + [jax.ShapeDtypeStruct((m, w), F32)] * 2,
        scratch_shapes=[pltpu.VMEM((tm, d), BF16)],
        compiler_params=pltpu.CompilerParams(
            dimension_semantics=("arbitrary", "arbitrary"), vmem_limit_bytes=vmem),
        name=name,
    )(x, norm_g, w_in)


def _bias_tile_kernel(tab_ref, o_ref, *, tq):
    h = pl.program_id(0)
    r = lax.broadcasted_iota(jnp.int32, (2 * tq, 2 * tq), 0)
    c = lax.broadcasted_iota(jnp.int32, (2 * tq, 2 * tq), 1)
    key = r - tq
    qry = c % tq
    far = tab_ref[h * NUM_BUCKETS + NUM_BUCKETS - 1]
    bucket = _bucket(qry - key)
    val = jnp.zeros(r.shape, F32)
    for b in range(NUM_BUCKETS - 1):
        val = jnp.where(bucket == b, (tab_ref[h * NUM_BUCKETS + b] - far) * LOG2E, val)
    o_ref[...] = jnp.where(key <= qry, val, NEG)


def _bias_tiles(rel_bias, *, tq):
    n_heads = rel_bias.shape[1]
    tab = rel_bias.T.reshape(-1)
    return pl.pallas_call(
        functools.partial(_bias_tile_kernel, tq=tq),
        grid=(n_heads,),
        in_specs=[pl.BlockSpec(memory_space=pltpu.SMEM)],
        out_specs=pl.BlockSpec((None, 2 * tq, 2 * tq), lambda h: (h, 0, 0)),
        out_shape=jax.ShapeDtypeStruct((n_heads, 2 * tq, 2 * tq), F32),
        compiler_params=pltpu.CompilerParams(
            dimension_semantics=("arbitrary",),
            vmem_limit_bytes=_vmem_limit(16 * 2 * tq * 2 * tq * 4)),
        name="rel_bias_tiles",
    )(tab)


def _lambda(lq1_ref, lk1_ref, lq2_ref, lk2_ref, lam_init):
    a = jnp.sum(lq1_ref[...] * lk1_ref[...], axis=-1, keepdims=True)
    b = jnp.sum(lq2_ref[...] * lk2_ref[...], axis=-1, keepdims=True)
    return jnp.exp(a) - jnp.exp(b) + lam_init


def _attn_body(qi, lq1_ref, lk1_ref, lq2_ref, lk2_ref, g_ref, q_ref, k_ref, v_ref, bias_ref,
               o_ref, kb_scr, vt_scr, m_scr, l_scr, acc_scr, *, t, tq, dk, lam_init):
    n_strips = t // tq
    seq = k_ref.shape[0]
    sw = 2 * tq

    @pl.when(qi == 0)
    def _():
        kb_scr[0:tq, :] = jnp.zeros((tq, kb_scr.shape[1]), BF16)
        vt_scr[0] = jnp.zeros(vt_scr.shape[1:], BF16)
        kb_scr[tq:, :] = k_ref[...].astype(BF16)
        for c in range(seq // tq):
            vt_scr[c + 1] = v_ref[c * tq:(c + 1) * tq, :].T.astype(BF16)

    qa = q_ref[...].astype(F32)
    lane = lax.broadcasted_iota(jnp.int32, qa.shape, 1)
    q1 = jnp.where(lane < dk, qa, 0.0).astype(BF16)
    q2 = jnp.where(lane >= dk, qa, 0.0).astype(BF16)
    strips = [jnp.concatenate([q1[u * tq:(u + 1) * tq], q2[u * tq:(u + 1) * tq]], axis=0)
              for u in range(n_strips)]
    qs_all = jnp.concatenate(strips, axis=0)

    m_scr[...] = jnp.full(m_scr.shape, NEG, F32)
    l_scr[...] = jnp.zeros(l_scr.shape, F32)
    acc_scr[...] = jnp.zeros(acc_scr.shape, F32)

    def keys(row0, n):
        return kb_scr[pl.ds(pl.multiple_of(row0, tq), n), :]

    def values_t(chunk0, n_chunks):
        return jnp.concatenate([vt_scr[chunk0 + i] for i in range(n_chunks)], axis=1)

    def fold(lanes, pieces):
        m_prev = m_scr[:, lanes]
        m_new = m_prev
        for s, _ in pieces:
            m_new = jnp.maximum(m_new, jnp.max(s, axis=0, keepdims=True))
        alpha = jnp.exp2(m_prev - m_new)
        l_new = alpha * l_scr[:, lanes]
        pv = None
        for s, vt in pieces:
            p = jnp.exp2(s - m_new)
            l_new = l_new + jnp.sum(p, axis=0, keepdims=True)
            term = jnp.dot(vt, p.astype(BF16), preferred_element_type=F32)
            pv = term if pv is None else pv + term
        l_scr[:, lanes] = l_new
        acc_scr[:, lanes] = alpha * acc_scr[:, lanes] + pv
        m_scr[:, lanes] = m_new

    def scores(kc, qs):
        return lax.dot_general(kc, qs, (((1,), (1,)), ((), ())), preferred_element_type=F32)

    far_chunks = t // tq
    every = slice(0, n_strips * sw)

    def fold_far(blocks):
        scored = []
        for chunk0, n_chunks in blocks:
            kc = keys(chunk0 * tq, n_chunks * tq)
            scored.append((scores(kc, qs_all), values_t(chunk0, n_chunks)))
        for piece in scored:
            fold(every, [piece])

    n_far_blocks = jnp.maximum(qi - 1, 0)
    n_pairs = n_far_blocks // 2

    def far_body(j, carry):
        c0 = 1 + j * 2 * far_chunks
        fold_far([(c0, far_chunks), (c0 + far_chunks, far_chunks)])
        return carry

    lax.fori_loop(0, n_pairs, far_body, 0)
    tail0 = 1 + n_pairs * 2 * far_chunks

    @pl.when((qi > 0) & (n_far_blocks % 2 == 0))
    def _():
        fold_far([(tail0, far_chunks - 1)])

    @pl.when(n_far_blocks % 2 == 1)
    def _():
        fold_far([(tail0, far_chunks), (tail0 + far_chunks, far_chunks - 1)])

    n_absent = jnp.where(qi == 0, tq, 0)
    pending = []
    near_ahead = 1
    for u in range(n_strips):
        lanes = slice(u * sw, (u + 1) * sw)
        qs = strips[u]
        pieces = []
        if u > 0:
            s_a = scores(keys(qi * t, u * tq), qs)
            row = lax.broadcasted_iota(jnp.int32, s_a.shape, 0)
            s_a = jnp.where(row < n_absent, NEG, s_a)
            pieces.append((s_a, values_t(qi * far_chunks, u)))
        bias = bias_ref[...]
        if u == 0:
            row = lax.broadcasted_iota(jnp.int32, bias.shape, 0)
            bias = jnp.where(row < n_absent, NEG, bias)
        s_b = scores(keys(qi * t + u * tq, 2 * tq), qs) + bias
        pieces.append((s_b, values_t(qi * far_chunks + u, 2)))
        pending.append((lanes, pieces))
        if len(pending) > near_ahead:
            fold(*pending.pop(0))
    for item in pending:
        fold(*item)

    o = acc_scr[...] / l_scr[...]
    lam = _lambda(lq1_ref, lk1_ref, lq2_ref, lk2_ref, lam_init)
    for u in range(n_strips):
        w = o[:, u * sw:u * sw + tq] - lam * o[:, u * sw + tq:(u + 1) * sw]
        y = (w * lax.rsqrt(jnp.mean(w * w, axis=0, keepdims=True) + SUBLN_EPS)
             * g_ref[...] * (1.0 - lam_init))
        o_ref[u * tq:(u + 1) * tq, :] = y.T.astype(o_ref.dtype)


class _Operands(NamedTuple):
    args: tuple
    in_specs: list
    out_spec: pl.BlockSpec
    out_shape: jax.ShapeDtypeStruct
    scratch: list
    vmem_bytes: int


def _attn_prompt_operands(qx, k, v, bias, lams, subln_g, *, seq, n_heads, t, tq, index):
    dv = k.shape[1] // n_heads
    nq = seq // t
    assert t % tq == 0 and tq >= MAX_DISTANCE
    at = lambda f: (lambda *g: f(*index(*g)))
    g_cols = jnp.broadcast_to(subln_g.reshape(dv, 1), (dv, tq))
    lam_spec = pl.BlockSpec((1, dv // 2), at(lambda h, b, i: (0, 0)))
    kv_spec = pl.BlockSpec((seq, dv), at(lambda h, b, i: (b, h)))
    return _Operands(
        args=(*lams, g_cols, qx, k, v, bias),
        in_specs=[lam_spec, lam_spec, lam_spec, lam_spec,
                  pl.BlockSpec((dv, tq), at(lambda h, b, i: (0, 0))),
                  pl.BlockSpec((None, t, dv), at(lambda h, b, i: (QX_Q, b * nq + i, h))),
                  kv_spec, kv_spec,
                  pl.BlockSpec((None, 2 * tq, 2 * tq), at(lambda h, b, i: (h, 0, 0)))],
        out_spec=pl.BlockSpec((t, dv), at(lambda h, b, i: (b * nq + i, h))),
        out_shape=jax.ShapeDtypeStruct(k.shape, BF16),
        scratch=[pltpu.VMEM((seq + tq, dv), BF16), pltpu.VMEM((seq // tq + 1, dv, tq), BF16),
                 pltpu.VMEM((1, 2 * t), F32), pltpu.VMEM((1, 2 * t), F32),
                 pltpu.VMEM((dv, 2 * t), F32)],
        vmem_bytes=(2 * 2 * seq * dv * 4 + 2 * (seq + tq) * dv * 2 + 2 * 2 * tq * 2 * tq * 4
                    + 4 * t * dv * 2 + dv * 2 * t * 4 + 6 * t * 2 * t * 4))


def _page_copies(pt_ref, ck_ref, cv_ref, kbuf, vbuf, sems, step, *, n_c, p_n):
    seq, c, slot = lax.div(step, n_c), lax.rem(step, n_c), lax.rem(step, 2)
    copies = []
    for j in range(p_n):
        page = pt_ref[seq, c * p_n + j]
        copies.append(pltpu.make_async_copy(ck_ref.at[page], kbuf.at[slot, j], sems.at[0, slot]))
        copies.append(pltpu.make_async_copy(cv_ref.at[page], vbuf.at[slot, j], sems.at[1, slot]))
    return copies


def _decode_body(step, n_steps, n_c, pt_ref, lq1_ref, lk1_ref, lq2_ref, lk2_ref, g_ref, tab_ref,
                 q_ref, kn_ref, vn_ref, ck_ref, cv_ref, o_ref, qm_scr, bias_scr, m_scr, l_scr,
                 acc_scr, kbuf, vbuf, sems, *, n_pages_step, n_heads, dv, lam_init):
    p_n = n_pages_step
    c = lax.rem(step, n_c)
    slot = lax.rem(step, 2)
    rows = 2 * n_heads
    page_cols = PAGE_SIZE * n_heads
    copies = functools.partial(_page_copies, pt_ref, ck_ref, cv_ref, kbuf, vbuf, sems,
                               n_c=n_c, p_n=p_n)

    @pl.when(step == 0)
    def _():
        for cp in copies(step):
            cp.start()

    @pl.when(step + 1 < n_steps)
    def _():
        for cp in copies(step + 1):
            cp.start()

    for cp in copies(step):
        cp.wait()
    k_refs = [kbuf.at[slot, j] for j in range(p_n)]
    v_refs = [vbuf.at[slot, j] for j in range(p_n)]

    @pl.when(c == 0)
    def _():
        q = q_ref[...].astype(F32)
        lane = lax.broadcasted_iota(jnp.int32, q.shape, 1)
        qm_scr[...] = jnp.concatenate(
            [jnp.where(lane < dv // 2, q, 0.0), jnp.where(lane >= dv // 2, q, 0.0)],
            axis=0).astype(BF16)
        key_row = lax.broadcasted_iota(jnp.int32, (1, page_cols), 1) // n_heads
        bucket = _bucket(PAGE_SIZE - key_row)
        val = jnp.zeros((rows, page_cols), F32)
        for b in range(NUM_BUCKETS - 1):
            val = jnp.where(bucket == b, tab_ref[:, b:b + 1], val)
        bias_scr[...] = val
        m_scr[...] = jnp.full(m_scr.shape, NEG, F32)
        l_scr[...] = jnp.zeros(l_scr.shape, F32)
        acc_scr[...] = jnp.zeros(acc_scr.shape, F32)

    qm = qm_scr[...]
    row = lax.broadcasted_iota(jnp.int32, (rows, page_cols), 0)
    col = lax.broadcasted_iota(jnp.int32, (rows, page_cols), 1)
    own = (col % n_heads) == (row % n_heads)
    near = jnp.where(c == n_c - 1, bias_scr[...], 0.0)
    pieces = []
    for j in range(p_n):
        kf = k_refs[j][...].reshape(page_cols, dv).astype(BF16)
        sj = lax.dot_general(qm, kf, (((1,), (1,)), ((), ())), preferred_element_type=F32)
        if j == p_n - 1:
            sj = sj + near
        pieces.append(jnp.where(own, sj, NEG))

    def update(s, pv_fn):
        m_prev = m_scr[...]
        m_new = jnp.maximum(m_prev, jnp.max(s, axis=-1, keepdims=True))
        p = jnp.exp2(s - m_new)
        alpha = jnp.exp2(m_prev - m_new)
        l_scr[...] = alpha * l_scr[...] + jnp.sum(p, axis=-1, keepdims=True)
        acc_scr[...] = alpha * acc_scr[...] + pv_fn(p)
        m_scr[...] = m_new

    def pv_pages(first):
        def pv(p):
            pb = p.astype(BF16)
            out = None
            for j in range(DECODE_GROUP):
                vf = v_refs[first + j][...].reshape(page_cols, dv).astype(BF16)
                term = jnp.dot(pb[:, j * page_cols:(j + 1) * page_cols], vf,
                               preferred_element_type=F32)
                out = term if out is None else out + term
            return out
        return pv

    for first in range(0, p_n, DECODE_GROUP):
        update(jnp.concatenate(pieces[first:first + DECODE_GROUP], axis=1), pv_pages(first))

    @pl.when(c == n_c - 1)
    def _():
        kn = jnp.concatenate([kn_ref[...]] * 2, axis=0)
        vn = jnp.concatenate([vn_ref[...]] * 2, axis=0)
        s_self = jnp.sum(qm.astype(F32) * kn, axis=-1, keepdims=True) + tab_ref[:, 0:1]
        update(s_self, lambda p: p * vn)
        o = acc_scr[...] / l_scr[...]
        lam = _lambda(lq1_ref, lk1_ref, lq2_ref, lk2_ref, lam_init)
        w = o[0:n_heads] - lam * o[n_heads:rows]
        o_ref[...] = _rms(w, g_ref[...], SUBLN_EPS) * (1.0 - lam_init)


def _attn_decode_operands(q, k_new, v_new, cache_k, cache_v, tab, lams, subln_g, *,
                          n_pages_step, index):
    db, n_heads, dv = q.shape
    p_n = n_pages_step
    rows = 2 * n_heads
    page_cols = PAGE_SIZE * n_heads
    assert PAGE_SIZE >= MAX_DISTANCE and p_n % DECODE_GROUP == 0
    at = lambda f: (lambda *g: f(index(*g)))
    small = lambda shape: pl.BlockSpec(shape, at(lambda s: (0,) * len(shape)))
    row_spec = pl.BlockSpec((None, n_heads, dv), at(lambda s: (s, 0, 0)))
    hbm = pl.BlockSpec(memory_space=pl.ANY)
    slots = (2, p_n, PAGE_SIZE, n_heads, dv)
    return _Operands(
        args=(*lams, subln_g, tab, q, k_new, v_new, cache_k, cache_v),
        in_specs=[small((1, dv // 2))] * 4 + [small((1, dv)), small((rows, NUM_BUCKETS)),
                                              row_spec, row_spec, row_spec, hbm, hbm],
        out_spec=row_spec,
        out_shape=jax.ShapeDtypeStruct((db, n_heads, dv), F32),
        scratch=[pltpu.VMEM((rows, dv), BF16), pltpu.VMEM((rows, page_cols), F32),
                 pltpu.VMEM((rows, 1), F32), pltpu.VMEM((rows, 1), F32),
                 pltpu.VMEM((rows, dv), F32), pltpu.VMEM(slots, F32), pltpu.VMEM(slots, F32),
                 pltpu.SemaphoreType.DMA((2, 2))],
        vmem_bytes=(2 * 2 * p_n * page_cols * dv * 4
                    + 2 * 2 * page_cols * dv * 2
                    + 6 * rows * p_n * page_cols * 4))


def _attention(prompt, decode, page_table, *, grid_prompt, n_chunks, attn_kw, decode_kw):
    n_heads, batch, nq = grid_prompt

    def step_of(h, b, i):
        return (h * batch + b) * nq + i

    n_seq = page_table.shape[0]
    n_steps = n_seq * n_chunks
    if n_heads * batch * nq == n_steps:
        pr = prompt(index=lambda h, b, i, pt: (h, b, i))
        de = decode(index=lambda h, b, i, pt: lax.div(step_of(h, b, i), n_chunks))
        n_pr, n_de = len(pr.args), len(de.args)

        def fused(pt_ref, *refs):
            ins, outs, scr = refs[:n_pr + n_de], refs[n_pr + n_de:n_pr + n_de + 2], \
                refs[n_pr + n_de + 2:]
            h, b, i = pl.program_id(0), pl.program_id(1), pl.program_id(2)
            _decode_body(step_of(h, b, i), n_steps, n_chunks, pt_ref, *ins[n_pr:], outs[1],
                         *scr[len(pr.scratch):], **decode_kw)
            _attn_body(i, *ins[:n_pr], outs[0], *scr[:len(pr.scratch)], **attn_kw)

        return pl.pallas_call(
            fused,
            grid_spec=pltpu.PrefetchScalarGridSpec(
                num_scalar_prefetch=1, grid=grid_prompt,
                in_specs=pr.in_specs + de.in_specs, out_specs=[pr.out_spec, de.out_spec],
                scratch_shapes=pr.scratch + de.scratch),
            out_shape=[pr.out_shape, de.out_shape],
            compiler_params=pltpu.CompilerParams(
                dimension_semantics=("arbitrary",) * 3,
                vmem_limit_bytes=_vmem_limit(pr.vmem_bytes, de.vmem_bytes)),
            name="attn_prompt_decode",
        )(page_table, *pr.args, *de.args)

    pr = prompt(index=lambda h, b, i: (h, b, i))
    att = pl.pallas_call(
        lambda *refs: _attn_body(pl.program_id(2), *refs, **attn_kw),
        grid=grid_prompt, in_specs=pr.in_specs, out_specs=pr.out_spec, out_shape=pr.out_shape,
        scratch_shapes=pr.scratch,
        compiler_params=pltpu.CompilerParams(
            dimension_semantics=("arbitrary",) * 3, vmem_limit_bytes=_vmem_limit(pr.vmem_bytes)),
        name="attn_prompt",
    )(*pr.args)
    de = decode(index=lambda s, c, pt: s)
    att_s = pl.pallas_call(
        lambda pt_ref, *refs: _decode_body(pl.program_id(0) * n_chunks + pl.program_id(1),
                                           n_steps, n_chunks, pt_ref, *refs, **decode_kw),
        grid_spec=pltpu.PrefetchScalarGridSpec(
            num_scalar_prefetch=1, grid=(n_seq, n_chunks), in_specs=de.in_specs,
            out_specs=de.out_spec, scratch_shapes=de.scratch),
        out_shape=de.out_shape,
        compiler_params=pltpu.CompilerParams(
            dimension_semantics=("arbitrary",) * 2, vmem_limit_bytes=_vmem_limit(de.vmem_bytes)),
        name="attn_decode",
    )(page_table, *de.args)
    return att, att_s


def _pool_project(diffs, pw_ref, ps_ref):
    outs = [jnp.dot(d.astype(BF16), pw_ref[g], preferred_element_type=F32)
            for g, d in enumerate(diffs)]
    return (jnp.concatenate(outs, axis=1) * ps_ref[...]).astype(BF16)


def _mix_kernel(att_ref, xp_ref, halo_ref, pw_ref, ps_ref, wo_ref, y_ref, o_ref, full_scr,
                *, tm, tiles_per_seq):
    t_in_seq = lax.rem(pl.program_id(0), tiles_per_seq)
    att_w = att_ref.shape[1]
    gc = xp_ref.shape[1] // len(POOL_WINDOWS)
    out = y_ref[...] + jnp.dot(att_ref[...], wo_ref[0:att_w, :], preferred_element_type=F32)
    full_scr[0:POOL_HALO, :] = jnp.where(t_in_seq == 0, 0.0, halo_ref[...])
    full_scr[POOL_HALO:, :] = xp_ref[...]
    pos = t_in_seq * tm + lax.broadcasted_iota(jnp.int32, (tm, 1), 0)
    diffs = []
    for g, win in enumerate(POOL_WINDOWS):
        cols = slice(g * gc, (g + 1) * gc)
        x = xp_ref[:, cols]
        acc = x
        for d in range(1, win):
            acc = acc + full_scr[POOL_HALO - d:POOL_HALO - d + tm, cols]
        cnt = jnp.minimum(pos + 1, win).astype(F32)
        diffs.append(acc / cnt - x)
    pool = _pool_project(diffs, pw_ref, ps_ref)
    o_ref[...] = out + jnp.dot(pool, wo_ref[att_w:, :], preferred_element_type=F32)


def _mix_prompt(att, qx, pool_w, pool_scale, w_out, y, *, seq, tm):
    m, d = y.shape
    aw, pw = att.shape[1], qx.shape[2]
    n_g, gc = pool_w.shape[0], pool_w.shape[1]
    tiles_per_seq = seq // tm
    halo_blocks = tm // POOL_HALO
    vmem = _vmem_limit(2 * tm * (aw * 2 + pw * 4 + 2 * d * 4), 2 * POOL_HALO * pw * 4,
                       2 * (aw + pw) * d * 2, 2 * n_g * gc * gc * 2,
                       (tm + POOL_HALO) * pw * 4, 4 * tm * pw * 4)
    return pl.pallas_call(
        functools.partial(_mix_kernel, tm=tm, tiles_per_seq=tiles_per_seq),
        grid=(m // tm,),
        in_specs=[
            pl.BlockSpec((tm, aw), lambda i: (i, 0)),
            pl.BlockSpec((None, tm, pw), lambda i: (QX_POOL, i, 0)),
            pl.BlockSpec((None, POOL_HALO, pw),
                         lambda i: (QX_POOL, jnp.maximum(i * halo_blocks - 1, 0), 0)),
            pl.BlockSpec((n_g, gc, gc), lambda i: (0, 0, 0)),
            pl.BlockSpec((1, pw), lambda i: (0, 0)),
            pl.BlockSpec((aw + pw, d), lambda i: (0, 0)),
            pl.BlockSpec((tm, d), lambda i: (i, 0)),
        ],
        out_specs=pl.BlockSpec((tm, d), lambda i: (i, 0)),
        out_shape=jax.ShapeDtypeStruct((m, d), F32),
        scratch_shapes=[pltpu.VMEM((tm + POOL_HALO, pw), F32)],
        compiler_params=pltpu.CompilerParams(
            dimension_semantics=("arbitrary",), vmem_limit_bytes=vmem),
        name="mix_prompt",
    )(att, qx, qx, pool_w, pool_scale, w_out, y)


def _mix_sample_kernel(att_ref, xp_ref, st_ref, pw_ref, ps_ref, wo_ref, y_ref, o_ref, *, pos0):
    att_w = att_ref.shape[1]
    gc = xp_ref.shape[1] // len(POOL_WINDOWS)
    row = lax.broadcasted_iota(jnp.int32, (1, POOL_HALO, 1), 1)
    diffs = []
    for g, win in enumerate(POOL_WINDOWS):
        cols = slice(g * gc, (g + 1) * gc)
        x = xp_ref[:, cols]
        prev = jnp.where(row >= POOL_HALO - (win - 1), st_ref[:, :, cols], 0.0)
        acc = x + jnp.sum(prev, axis=1)
        diffs.append(acc / float(min(pos0 + 1, win)) - x)
    pool = _pool_project(diffs, pw_ref, ps_ref)
    o_ref[...] = (y_ref[...]
                  + jnp.dot(att_ref[...].astype(BF16), wo_ref[0:att_w, :],
                            preferred_element_type=F32)
                  + jnp.dot(pool, wo_ref[att_w:, :], preferred_element_type=F32))


def _mix_sample(att, xp, state16, pool_w, pool_scale, w_out, y, *, pos0):
    whole = lambda a: pl.BlockSpec(a.shape, lambda: (0,) * a.ndim)
    args = (att, xp, state16, pool_w, pool_scale, w_out, y)
    return pl.pallas_call(
        functools.partial(_mix_sample_kernel, pos0=pos0),
        in_specs=[whole(a) for a in args],
        out_specs=whole(y),
        out_shape=jax.ShapeDtypeStruct(y.shape, F32),
        compiler_params=pltpu.CompilerParams(
            vmem_limit_bytes=_vmem_limit(*(2 * a.size * a.dtype.itemsize for a in args),
                                         4 * state16.size * 4)),
        name="mix_sample",
    )(*args)


def kernel(x_prompt, x_sample, cache_k, cache_v, state_pool, page_table, w_in, w_out, ffn1_norm,
           ffn1_gate, ffn1_up, ffn1_down, mix_norm, ffn2_norm, ffn2_gate, ffn2_up, ffn2_down,
           lambda_q1, lambda_k1, lambda_q2, lambda_k2, subln_g, pool_w, pool_scale, rel_bias,
           final_norm):
    batch, seq, d = x_prompt.shape
    db, sd, _ = x_sample.shape
    depth, n_phys, page, n_heads, dv = cache_v.shape
    dk = dv // 2
    width = n_heads * dv
    pool_width = state_pool.shape[-1]
    assert sd == 1 and page == PAGE_SIZE and cache_k.shape[-1] == 2 * dk
    assert w_in.shape[-1] == 4 * width and pool_width == width
    assert state_pool.shape[2] == POOL_HALO - 1 and rel_bias.shape[0] == NUM_BUCKETS
    n_past = page_table.shape[1] * PAGE_SIZE
    scale = dk ** -0.5 * LOG2E

    tm_p = min(512, batch * seq)
    tm_ffn = min(1024, batch * seq)
    tf = 512
    t_att = min(512, seq)
    tq_att = 128
    pages_step = min(16, page_table.shape[1])
    assert seq % tm_p == 0 and seq % t_att == 0 and page_table.shape[1] % pages_step == 0

    row2 = lambda a: a.reshape(1, -1)
    yp = x_prompt.reshape(batch * seq, d)
    ys = x_sample.reshape(db * sd, d)
    bias = _bias_tiles(rel_bias, tq=tq_att)
    far = rel_bias[NUM_BUCKETS - 1]
    tab_dec = jnp.tile((rel_bias - far[None, :]).T * LOG2E, (2, 1))
    outs = {k: [] for k in ("kp", "vp", "pp", "ks", "vs", "ps")}

    for l in range(depth):
        lam_init = 0.8 - 0.6 * math.exp(-0.3 * l)
        lams = tuple(row2(a[l]) for a in (lambda_q1, lambda_k1, lambda_q2, lambda_k2))
        ffn_w = []
        for gate, up, down in ((ffn1_gate, ffn1_up, ffn1_down), (ffn2_gate, ffn2_up, ffn2_down)):
            ffn_w.append((gate[l].astype(BF16), up[l].astype(BF16), down[l].astype(BF16)))
        w_in_b = w_in[l].astype(BF16)
        w_out_b = w_out[l].astype(BF16)
        pool_w_b = pool_w[l].astype(BF16)
        pool_s = row2(pool_scale[l])
        last = l == depth - 1
        fin = row2(final_norm)

        yp = _ffn(yp, row2(ffn1_norm[l]), *ffn_w[0], fin, tm=tm_ffn, tf=tf, final=False,
                  name="ffn1_prompt")
        qx, k, v = _proj(yp, row2(mix_norm[l]), w_in_b, tm=tm_p, scale=scale,
                         name="proj_prompt")
        ys = _ffn(ys, row2(ffn1_norm[l]), *ffn_w[0], fin, tm=db, tf=tf, final=False,
                  name="ffn1_sample")
        qxs, ks, vs = _proj(ys, row2(mix_norm[l]), w_in_b, tm=db, scale=scale,
                            name="proj_sample")
        qs, xps = qxs[QX_Q], qxs[QX_POOL]

        att, att_s = _attention(
            functools.partial(_attn_prompt_operands, qx, k, v, bias, lams, row2(subln_g[l]),
                              seq=seq, n_heads=n_heads, t=t_att, tq=tq_att),
            functools.partial(_attn_decode_operands, qs.reshape(db, n_heads, dv),
                              ks.reshape(db, n_heads, dv), vs.reshape(db, n_heads, dv),
                              cache_k[l], cache_v[l], tab_dec, lams, row2(subln_g[l]),
                              n_pages_step=pages_step),
            page_table, grid_prompt=(n_heads, batch, seq // t_att),
            n_chunks=page_table.shape[1] // pages_step,
            attn_kw=dict(t=t_att, tq=tq_att, dk=dk, lam_init=lam_init),
            decode_kw=dict(n_pages_step=pages_step, n_heads=n_heads, dv=dv, lam_init=lam_init))

        yp = _mix_prompt(att, qx, pool_w_b, pool_s, w_out_b, yp, seq=seq, tm=tm_p)
        yp = _ffn(yp, row2(ffn2_norm[l]), *ffn_w[1], fin, tm=tm_ffn, tf=tf, final=last,
                  name="ffn2_prompt")
        outs["kp"].append(k.reshape(batch, seq, n_heads, dv))
        outs["vp"].append(v.reshape(batch, seq, n_heads, dv))
        outs["pp"].append(
            qx.reshape(2, batch, seq, pool_width)[QX_POOL, :, seq - (POOL_HALO - 1):])

        state16 = jnp.pad(state_pool[l], ((0, 0), (1, 0), (0, 0)))
        ys = _mix_sample(att_s.reshape(db, width), xps, state16, pool_w_b, pool_s, w_out_b, ys,
                         pos0=n_past)
        ys = _ffn(ys, row2(ffn2_norm[l]), *ffn_w[1], fin, tm=db, tf=tf, final=last,
                  name="ffn2_sample")
        outs["ks"].append(ks.reshape(db, sd, n_heads, dv))
        outs["vs"].append(vs.reshape(db, sd, n_heads, dv))
        outs["ps"].append(jnp.concatenate([state_pool[l][:, 1:], xps[:, None, :]], axis=1))

    st = lambda name: jnp.stack(outs[name], axis=0)
    return (yp.reshape(batch, seq, d), ys.reshape(db, sd, d), st("kp"), st("vp"), st("pp"),
            st("ks"), st("vs"), st("ps"))
```

```python
import functools
import math
from typing import NamedTuple

import jax
import jax.numpy as jnp
from jax import lax
from jax.experimental import pallas as pl
from jax.experimental.pallas import tpu as pltpu

F32 = jnp.float32
BF16 = jnp.bfloat16

V7X_LANES = 128
V7X_VMEM_BYTES = 64 * 1024 * 1024
V7X_VMEM_CAP = V7X_VMEM_BYTES - 8 * 1024 * 1024

EPS = 1e-6
SUBLN_EPS = 1e-5
NEG = -1e30
POOL_WINDOWS = (2, 4, 8, 16)
POOL_HALO = 16
NUM_BUCKETS = 32
MAX_EXACT = NUM_BUCKETS // 2
MAX_DISTANCE = 128
LOG2E = math.log2(math.e)
PAGE_SIZE = 128
DECODE_GROUP = 4


VMEM_LIMIT_FLOOR = 16 * 1024 * 1024


def _vmem_limit(*nbytes):
    est = int(sum(nbytes))
    return min(max(est, VMEM_LIMIT_FLOOR), V7X_VMEM_CAP)


def _rms(x, g, eps):
    return x * lax.rsqrt(jnp.mean(x * x, axis=-1, keepdims=True) + eps) * g


def _bucket(n):
    n = jnp.maximum(n, 0)
    nf = jnp.maximum(n, 1).astype(F32)
    large = MAX_EXACT + (jnp.log(nf / MAX_EXACT) / math.log(MAX_DISTANCE / MAX_EXACT)
                         * (NUM_BUCKETS - MAX_EXACT)).astype(jnp.int32)
    large = jnp.minimum(large, NUM_BUCKETS - 1)
    return jnp.where(n < MAX_EXACT, n, large)


def _ffn_kernel(x_ref, g_ref, wg_ref, wu_ref, wd_ref, fg_ref, o_ref, h_scr, *, n_f, last_tf,
                final):
    f = pl.program_id(1)
    tf = wg_ref.shape[1]

    @pl.when(f == 0)
    def _():
        h_scr[...] = _rms(x_ref[...], g_ref[...], EPS).astype(BF16)
        o_ref[...] = jnp.zeros(o_ref.shape, F32)

    def accumulate(cols):
        h = h_scr[...]
        gate = jnp.dot(h, wg_ref[:, 0:cols], preferred_element_type=F32)
        up = jnp.dot(h, wu_ref[:, 0:cols], preferred_element_type=F32)
        act = (gate / (1.0 + jnp.exp(-gate)) * up).astype(BF16)
        o_ref[...] += jnp.dot(act, wd_ref[0:cols, :], preferred_element_type=F32)

    if last_tf == tf:
        accumulate(tf)
    else:
        @pl.when(f < n_f - 1)
        def _():
            accumulate(tf)

        @pl.when(f == n_f - 1)
        def _():
            accumulate(last_tf)

    @pl.when(f == n_f - 1)
    def _():
        y = x_ref[...] + 0.5 * o_ref[...]
        if final:
            y = _rms(y, fg_ref[...], EPS)
        o_ref[...] = y


def _ffn(x, norm_g, wg, wu, wd, final_g, *, tm, tf, final, name):
    m, d = x.shape
    f_dim = wg.shape[1]
    n_f = pl.cdiv(f_dim, tf)
    last_tf = f_dim - (n_f - 1) * tf
    assert last_tf % V7X_LANES == 0
    vmem = _vmem_limit(2 * 2 * tm * d * 4,
                       2 * 3 * d * tf * 2,
                       tm * d * 2,
                       4 * tm * tf * 4 + tm * d * 4)
    return pl.pallas_call(
        functools.partial(_ffn_kernel, n_f=n_f, last_tf=last_tf, final=final),
        grid=(m // tm, n_f),
        in_specs=[
            pl.BlockSpec((tm, d), lambda i, f: (i, 0)),
            pl.BlockSpec((1, d), lambda i, f: (0, 0)),
            pl.BlockSpec((d, tf), lambda i, f: (0, f)),
            pl.BlockSpec((d, tf), lambda i, f: (0, f)),
            pl.BlockSpec((tf, d), lambda i, f: (f, 0)),
            pl.BlockSpec((1, d), lambda i, f: (0, 0)),
        ],
        out_specs=pl.BlockSpec((tm, d), lambda i, f: (i, 0)),
        out_shape=jax.ShapeDtypeStruct((m, d), F32),
        scratch_shapes=[pltpu.VMEM((tm, d), BF16)],
        compiler_params=pltpu.CompilerParams(
            dimension_semantics=("arbitrary", "arbitrary"), vmem_limit_bytes=vmem),
        name=name,
    )(x, norm_g, wg, wu, wd, final_g)


PROJ_Q, PROJ_K, PROJ_V, PROJ_POOL = range(4)
QX_Q, QX_POOL = range(2)


def _proj_kernel(x_ref, g_ref, w_ref, qx_ref, k_ref, v_ref, h_scr, *, scale):
    j = pl.program_id(1)

    @pl.when(j == 0)
    def _():
        h_scr[...] = _rms(x_ref[...], g_ref[...], EPS).astype(BF16)

    p = jnp.dot(h_scr[...], w_ref[...], preferred_element_type=F32)

    @pl.when(j == PROJ_K)
    def _():
        k_ref[...] = p

    @pl.when(j == PROJ_V)
    def _():
        v_ref[...] = p

    @pl.when((j == PROJ_Q) | (j == PROJ_POOL))
    def _():
        qx_ref[...] = p * jnp.where(j == PROJ_Q, scale, 1.0)


def _proj(x, norm_g, w_in, *, tm, scale, name):
    m, d = x.shape
    w = w_in.shape[1] // 4
    kv_spec = pl.BlockSpec((tm, w), lambda i, j: (i, 0))
    vmem = _vmem_limit(2 * tm * d * 4, 2 * d * w * 2, 3 * 2 * tm * w * 4, tm * d * 2,
                       2 * tm * w * 4)
    return pl.pallas_call(
        functools.partial(_proj_kernel, scale=scale),
        grid=(m // tm, 4),
        in_specs=[
            pl.BlockSpec((tm, d), lambda i, j: (i, 0)),
            pl.BlockSpec((1, d), lambda i, j: (0, 0)),
            pl.BlockSpec((d, w), lambda i, j: (0, j)),
        ],
        out_specs=[pl.BlockSpec((None, tm, w),
                                lambda i, j: (jnp.where(j == PROJ_POOL, QX_POOL, QX_Q), i, 0)),
                   kv_spec, kv_spec],
        out_shape=[jax.ShapeDtypeStruct((2, m, w), F32)] + [jax.ShapeDtypeStruct((m, w), F32)] * 2,
        scratch_shapes=[pltpu.VMEM((tm, d), BF16)],
        compiler_params=pltpu.CompilerParams(
            dimension_semantics=("arbitrary", "arbitrary"), vmem_limit_bytes=vmem),
        name=name,
    )(x, norm_g, w_in)


def _bias_tile_kernel(tab_ref, o_ref, *, tq):
    h = pl.program_id(0)
    r = lax.broadcasted_iota(jnp.int32, (2 * tq, 2 * tq), 0)
    c = lax.broadcasted_iota(jnp.int32, (2 * tq, 2 * tq), 1)
    key = r - tq
    qry = c % tq
    far = tab_ref[h * NUM_BUCKETS + NUM_BUCKETS - 1]
    bucket = _bucket(qry - key)
    val = jnp.zeros(r.shape, F32)
    for b in range(NUM_BUCKETS - 1):
        val = jnp.where(bucket == b, (tab_ref[h * NUM_BUCKETS + b] - far) * LOG2E, val)
    o_ref[...] = jnp.where(key <= qry, val, NEG)


def _bias_tiles(rel_bias, *, tq):
    n_heads = rel_bias.shape[1]
    tab = rel_bias.T.reshape(-1)
    return pl.pallas_call(
        functools.partial(_bias_tile_kernel, tq=tq),
        grid=(n_heads,),
        in_specs=[pl.BlockSpec(memory_space=pltpu.SMEM)],
        out_specs=pl.BlockSpec((None, 2 * tq, 2 * tq), lambda h: (h, 0, 0)),
        out_shape=jax.ShapeDtypeStruct((n_heads, 2 * tq, 2 * tq), F32),
        compiler_params=pltpu.CompilerParams(
            dimension_semantics=("arbitrary",),
            vmem_limit_bytes=_vmem_limit(16 * 2 * tq * 2 * tq * 4)),
        name="rel_bias_tiles",
    )(tab)


def _lambda(lq1_ref, lk1_ref, lq2_ref, lk2_ref, lam_init):
    a = jnp.sum(lq1_ref[...] * lk1_ref[...], axis=-1, keepdims=True)
    b = jnp.sum(lq2_ref[...] * lk2_ref[...], axis=-1, keepdims=True)
    return jnp.exp(a) - jnp.exp(b) + lam_init


def _attn_body(qi, lq1_ref, lk1_ref, lq2_ref, lk2_ref, g_ref, q_ref, k_ref, v_ref, bias_ref,
               o_ref, kb_scr, vt_scr, m_scr, l_scr, acc_scr, *, t, tq, dk, lam_init):
    n_strips = t // tq
    seq = k_ref.shape[0]
    sw = 2 * tq

    @pl.when(qi == 0)
    def _():
        kb_scr[0:tq, :] = jnp.zeros((tq, kb_scr.shape[1]), BF16)
        vt_scr[0] = jnp.zeros(vt_scr.shape[1:], BF16)
        kb_scr[tq:, :] = k_ref[...].astype(BF16)
        for c in range(seq // tq):
            vt_scr[c + 1] = v_ref[c * tq:(c + 1) * tq, :].T.astype(BF16)

    qa = q_ref[...].astype(F32)
    lane = lax.broadcasted_iota(jnp.int32, qa.shape, 1)
    q1 = jnp.where(lane < dk, qa, 0.0).astype(BF16)
    q2 = jnp.where(lane >= dk, qa, 0.0).astype(BF16)
    strips = [jnp.concatenate([q1[u * tq:(u + 1) * tq], q2[u * tq:(u + 1) * tq]], axis=0)
              for u in range(n_strips)]
    qs_all = jnp.concatenate(strips, axis=0)

    m_scr[...] = jnp.full(m_scr.shape, NEG, F32)
    l_scr[...] = jnp.zeros(l_scr.shape, F32)
    acc_scr[...] = jnp.zeros(acc_scr.shape, F32)

    def keys(row0, n):
        return kb_scr[pl.ds(pl.multiple_of(row0, tq), n), :]

    def values_t(chunk0, n_chunks):
        return jnp.concatenate([vt_scr[chunk0 + i] for i in range(n_chunks)], axis=1)

    def fold(lanes, pieces):
        m_prev = m_scr[:, lanes]
        m_new = m_prev
        for s, _ in pieces:
            m_new = jnp.maximum(m_new, jnp.max(s, axis=0, keepdims=True))
        alpha = jnp.exp2(m_prev - m_new)
        l_new = alpha * l_scr[:, lanes]
        pv = None
        for s, vt in pieces:
            p = jnp.exp2(s - m_new)
            l_new = l_new + jnp.sum(p, axis=0, keepdims=True)
            term = jnp.dot(vt, p.astype(BF16), preferred_element_type=F32)
            pv = term if pv is None else pv + term
        l_scr[:, lanes] = l_new
        acc_scr[:, lanes] = alpha * acc_scr[:, lanes] + pv
        m_scr[:, lanes] = m_new

    def scores(kc, qs):
        return lax.dot_general(kc, qs, (((1,), (1,)), ((), ())), preferred_element_type=F32)

    far_chunks = t // tq
    every = slice(0, n_strips * sw)

    def fold_far(blocks):
        scored = []
        for chunk0, n_chunks in blocks:
            kc = keys(chunk0 * tq, n_chunks * tq)
            scored.append((scores(kc, qs_all), values_t(chunk0, n_chunks)))
        for piece in scored:
            fold(every, [piece])

    n_far_blocks = jnp.maximum(qi - 1, 0)
    n_pairs = n_far_blocks // 2

    def far_body(j, carry):
        c0 = 1 + j * 2 * far_chunks
        fold_far([(c0, far_chunks), (c0 + far_chunks, far_chunks)])
        return carry

    lax.fori_loop(0, n_pairs, far_body, 0)
    tail0 = 1 + n_pairs * 2 * far_chunks

    @pl.when((qi > 0) & (n_far_blocks % 2 == 0))
    def _():
        fold_far([(tail0, far_chunks - 1)])

    @pl.when(n_far_blocks % 2 == 1)
    def _():
        fold_far([(tail0, far_chunks), (tail0 + far_chunks, far_chunks - 1)])

    n_absent = jnp.where(qi == 0, tq, 0)
    pending = []
    near_ahead = 1
    for u in range(n_strips):
        lanes = slice(u * sw, (u + 1) * sw)
        qs = strips[u]
        pieces = []
        if u > 0:
            s_a = scores(keys(qi * t, u * tq), qs)
            row = lax.broadcasted_iota(jnp.int32, s_a.shape, 0)
            s_a = jnp.where(row < n_absent, NEG, s_a)
            pieces.append((s_a, values_t(qi * far_chunks, u)))
        bias = bias_ref[...]
        if u == 0:
            row = lax.broadcasted_iota(jnp.int32, bias.shape, 0)
            bias = jnp.where(row < n_absent, NEG, bias)
        s_b = scores(keys(qi * t + u * tq, 2 * tq), qs) + bias
        pieces.append((s_b, values_t(qi * far_chunks + u, 2)))
        pending.append((lanes, pieces))
        if len(pending) > near_ahead:
            fold(*pending.pop(0))
    for item in pending:
        fold(*item)

    o = acc_scr[...] / l_scr[...]
    lam = _lambda(lq1_ref, lk1_ref, lq2_ref, lk2_ref, lam_init)
    for u in range(n_strips):
        w = o[:, u * sw:u * sw + tq] - lam * o[:, u * sw + tq:(u + 1) * sw]
        y = (w * lax.rsqrt(jnp.mean(w * w, axis=0, keepdims=True) + SUBLN_EPS)
             * g_ref[...] * (1.0 - lam_init))
        o_ref[u * tq:(u + 1) * tq, :] = y.T.astype(o_ref.dtype)


class _Operands(NamedTuple):
    args: tuple
    in_specs: list
    out_spec: pl.BlockSpec
    out_shape: jax.ShapeDtypeStruct
    scratch: list
    vmem_bytes: int


def _attn_prompt_operands(qx, k, v, bias, lams, subln_g, *, seq, n_heads, t, tq, index):
    dv = k.shape[1] // n_heads
    nq = seq // t
    assert t % tq == 0 and tq >= MAX_DISTANCE
    at = lambda f: (lambda *g: f(*index(*g)))
    g_cols = jnp.broadcast_to(subln_g.reshape(dv, 1), (dv, tq))
    lam_spec = pl.BlockSpec((1, dv // 2), at(lambda h, b, i: (0, 0)))
    kv_spec = pl.BlockSpec((seq, dv), at(lambda h, b, i: (b, h)))
    return _Operands(
        args=(*lams, g_cols, qx, k, v, bias),
        in_specs=[lam_spec, lam_spec, lam_spec, lam_spec,
                  pl.BlockSpec((dv, tq), at(lambda h, b, i: (0, 0))),
                  pl.BlockSpec((None, t, dv), at(lambda h, b, i: (QX_Q, b * nq + i, h))),
                  kv_spec, kv_spec,
                  pl.BlockSpec((None, 2 * tq, 2 * tq), at(lambda h, b, i: (h, 0, 0)))],
        out_spec=pl.BlockSpec((t, dv), at(lambda h, b, i: (b * nq + i, h))),
        out_shape=jax.ShapeDtypeStruct(k.shape, BF16),
        scratch=[pltpu.VMEM((seq + tq, dv), BF16), pltpu.VMEM((seq // tq + 1, dv, tq), BF16),
                 pltpu.VMEM((1, 2 * t), F32), pltpu.VMEM((1, 2 * t), F32),
                 pltpu.VMEM((dv, 2 * t), F32)],
        vmem_bytes=(2 * 2 * seq * dv * 4 + 2 * (seq + tq) * dv * 2 + 2 * 2 * tq * 2 * tq * 4
                    + 4 * t * dv * 2 + dv * 2 * t * 4 + 6 * t * 2 * t * 4))


def _page_copies(pt_ref, ck_ref, cv_ref, kbuf, vbuf, sems, step, *, n_c, p_n):
    seq, c, slot = lax.div(step, n_c), lax.rem(step, n_c), lax.rem(step, 2)
    copies = []
    for j in range(p_n):
        page = pt_ref[seq, c * p_n + j]
        copies.append(pltpu.make_async_copy(ck_ref.at[page], kbuf.at[slot, j], sems.at[0, slot]))
        copies.append(pltpu.make_async_copy(cv_ref.at[page], vbuf.at[slot, j], sems.at[1, slot]))
    return copies


def _decode_body(step, n_steps, n_c, pt_ref, lq1_ref, lk1_ref, lq2_ref, lk2_ref, g_ref, tab_ref,
                 q_ref, kn_ref, vn_ref, ck_ref, cv_ref, o_ref, qm_scr, bias_scr, m_scr, l_scr,
                 acc_scr, kbuf, vbuf, sems, *, n_pages_step, n_heads, dv, lam_init):
    p_n = n_pages_step
    c = lax.rem(step, n_c)
    slot = lax.rem(step, 2)
    rows = 2 * n_heads
    page_cols = PAGE_SIZE * n_heads
    copies = functools.partial(_page_copies, pt_ref, ck_ref, cv_ref, kbuf, vbuf, sems,
                               n_c=n_c, p_n=p_n)

    @pl.when(step == 0)
    def _():
        for cp in copies(step):
            cp.start()

    @pl.when(step + 1 < n_steps)
    def _():
        for cp in copies(step + 1):
            cp.start()

    for cp in copies(step):
        cp.wait()
    k_refs = [kbuf.at[slot, j] for j in range(p_n)]
    v_refs = [vbuf.at[slot, j] for j in range(p_n)]

    @pl.when(c == 0)
    def _():
        q = q_ref[...].astype(F32)
        lane = lax.broadcasted_iota(jnp.int32, q.shape, 1)
        qm_scr[...] = jnp.concatenate(
            [jnp.where(lane < dv // 2, q, 0.0), jnp.where(lane >= dv // 2, q, 0.0)],
            axis=0).astype(BF16)
        key_row = lax.broadcasted_iota(jnp.int32, (1, page_cols), 1) // n_heads
        bucket = _bucket(PAGE_SIZE - key_row)
        val = jnp.zeros((rows, page_cols), F32)
        for b in range(NUM_BUCKETS - 1):
            val = jnp.where(bucket == b, tab_ref[:, b:b + 1], val)
        bias_scr[...] = val
        m_scr[...] = jnp.full(m_scr.shape, NEG, F32)
        l_scr[...] = jnp.zeros(l_scr.shape, F32)
        acc_scr[...] = jnp.zeros(acc_scr.shape, F32)

    qm = qm_scr[...]
    row = lax.broadcasted_iota(jnp.int32, (rows, page_cols), 0)
    col = lax.broadcasted_iota(jnp.int32, (rows, page_cols), 1)
    own = (col % n_heads) == (row % n_heads)
    near = jnp.where(c == n_c - 1, bias_scr[...], 0.0)
    pieces = []
    for j in range(p_n):
        kf = k_refs[j][...].reshape(page_cols, dv).astype(BF16)
        sj = lax.dot_general(qm, kf, (((1,), (1,)), ((), ())), preferred_element_type=F32)
        if j == p_n - 1:
            sj = sj + near
        pieces.append(jnp.where(own, sj, NEG))

    def update(s, pv_fn):
        m_prev = m_scr[...]
        m_new = jnp.maximum(m_prev, jnp.max(s, axis=-1, keepdims=True))
        p = jnp.exp2(s - m_new)
        alpha = jnp.exp2(m_prev - m_new)
        l_scr[...] = alpha * l_scr[...] + jnp.sum(p, axis=-1, keepdims=True)
        acc_scr[...] = alpha * acc_scr[...] + pv_fn(p)
        m_scr[...] = m_new

    def pv_pages(first):
        def pv(p):
            pb = p.astype(BF16)
            out = None
            for j in range(DECODE_GROUP):
                vf = v_refs[first + j][...].reshape(page_cols, dv).astype(BF16)
                term = jnp.dot(pb[:, j * page_cols:(j + 1) * page_cols], vf,
                               preferred_element_type=F32)
                out = term if out is None else out + term
            return out
        return pv

    for first in range(0, p_n, DECODE_GROUP):
        update(jnp.concatenate(pieces[first:first + DECODE_GROUP], axis=1), pv_pages(first))

    @pl.when(c == n_c - 1)
    def _():
        kn = jnp.concatenate([kn_ref[...]] * 2, axis=0)
        vn = jnp.concatenate([vn_ref[...]] * 2, axis=0)
        s_self = jnp.sum(qm.astype(F32) * kn, axis=-1, keepdims=True) + tab_ref[:, 0:1]
        update(s_self, lambda p: p * vn)
        o = acc_scr[...] / l_scr[...]
        lam = _lambda(lq1_ref, lk1_ref, lq2_ref, lk2_ref, lam_init)
        w = o[0:n_heads] - lam * o[n_heads:rows]
        o_ref[...] = _rms(w, g_ref[...], SUBLN_EPS) * (1.0 - lam_init)


def _attn_decode_operands(q, k_new, v_new, cache_k, cache_v, tab, lams, subln_g, *,
                          n_pages_step, index):
    db, n_heads, dv = q.shape
    p_n = n_pages_step
    rows = 2 * n_heads
    page_cols = PAGE_SIZE * n_heads
    assert PAGE_SIZE >= MAX_DISTANCE and p_n % DECODE_GROUP == 0
    at = lambda f: (lambda *g: f(index(*g)))
    small = lambda shape: pl.BlockSpec(shape, at(lambda s: (0,) * len(shape)))
    row_spec = pl.BlockSpec((None, n_heads, dv), at(lambda s: (s, 0, 0)))
    hbm = pl.BlockSpec(memory_space=pl.ANY)
    slots = (2, p_n, PAGE_SIZE, n_heads, dv)
    return _Operands(
        args=(*lams, subln_g, tab, q, k_new, v_new, cache_k, cache_v),
        in_specs=[small((1, dv // 2))] * 4 + [small((1, dv)), small((rows, NUM_BUCKETS)),
                                              row_spec, row_spec, row_spec, hbm, hbm],
        out_spec=row_spec,
        out_shape=jax.ShapeDtypeStruct((db, n_heads, dv), F32),
        scratch=[pltpu.VMEM((rows, dv), BF16), pltpu.VMEM((rows, page_cols), F32),
                 pltpu.VMEM((rows, 1), F32), pltpu.VMEM((rows, 1), F32),
                 pltpu.VMEM((rows, dv), F32), pltpu.VMEM(slots, F32), pltpu.VMEM(slots, F32),
                 pltpu.SemaphoreType.DMA((2, 2))],
        vmem_bytes=(2 * 2 * p_n * page_cols * dv * 4
                    + 2 * 2 * page_cols * dv * 2
                    + 6 * rows * p_n * page_cols * 4))


def _attention(prompt, decode, page_table, *, grid_prompt, n_chunks, attn_kw, decode_kw):
    n_heads, batch, nq = grid_prompt

    def step_of(h, b, i):
        return (h * batch + b) * nq + i

    n_seq = page_table.shape[0]
    n_steps = n_seq * n_chunks
    if n_heads * batch * nq == n_steps:
        pr = prompt(index=lambda h, b, i, pt: (h, b, i))
        de = decode(index=lambda h, b, i, pt: lax.div(step_of(h, b, i), n_chunks))
        n_pr, n_de = len(pr.args), len(de.args)

        def fused(pt_ref, *refs):
            ins, outs, scr = refs[:n_pr + n_de], refs[n_pr + n_de:n_pr + n_de + 2], \
                refs[n_pr + n_de + 2:]
            h, b, i = pl.program_id(0), pl.program_id(1), pl.program_id(2)
            _decode_body(step_of(h, b, i), n_steps, n_chunks, pt_ref, *ins[n_pr:], outs[1],
                         *scr[len(pr.scratch):], **decode_kw)
            _attn_body(i, *ins[:n_pr], outs[0], *scr[:len(pr.scratch)], **attn_kw)

        return pl.pallas_call(
            fused,
            grid_spec=pltpu.PrefetchScalarGridSpec(
                num_scalar_prefetch=1, grid=grid_prompt,
                in_specs=pr.in_specs + de.in_specs, out_specs=[pr.out_spec, de.out_spec],
                scratch_shapes=pr.scratch + de.scratch),
            out_shape=[pr.out_shape, de.out_shape],
            compiler_params=pltpu.CompilerParams(
                dimension_semantics=("arbitrary",) * 3,
                vmem_limit_bytes=_vmem_limit(pr.vmem_bytes, de.vmem_bytes)),
            name="attn_prompt_decode",
        )(page_table, *pr.args, *de.args)

    pr = prompt(index=lambda h, b, i: (h, b, i))
    att = pl.pallas_call(
        lambda *refs: _attn_body(pl.program_id(2), *refs, **attn_kw),
        grid=grid_prompt, in_specs=pr.in_specs, out_specs=pr.out_spec, out_shape=pr.out_shape,
        scratch_shapes=pr.scratch,
        compiler_params=pltpu.CompilerParams(
            dimension_semantics=("arbitrary",) * 3, vmem_limit_bytes=_vmem_limit(pr.vmem_bytes)),
        name="attn_prompt",
    )(*pr.args)
    de = decode(index=lambda s, c, pt: s)
    att_s = pl.pallas_call(
        lambda pt_ref, *refs: _decode_body(pl.program_id(0) * n_chunks + pl.program_id(1),
                                           n_steps, n_chunks, pt_ref, *refs, **decode_kw),
        grid_spec=pltpu.PrefetchScalarGridSpec(
            num_scalar_prefetch=1, grid=(n_seq, n_chunks), in_specs=de.in_specs,
            out_specs=de.out_spec, scratch_shapes=de.scratch),
        out_shape=de.out_shape,
        compiler_params=pltpu.CompilerParams(
            dimension_semantics=("arbitrary",) * 2, vmem_limit_bytes=_vmem_limit(de.vmem_bytes)),
        name="attn_decode",
    )(page_table, *de.args)
    return att, att_s


def _pool_project(diffs, pw_ref, ps_ref):
    outs = [jnp.dot(d.astype(BF16), pw_ref[g], preferred_element_type=F32)
            for g, d in enumerate(diffs)]
    return (jnp.concatenate(outs, axis=1) * ps_ref[...]).astype(BF16)


def _mix_kernel(att_ref, xp_ref, halo_ref, pw_ref, ps_ref, wo_ref, y_ref, o_ref, full_scr,
                *, tm, tiles_per_seq):
    t_in_seq = lax.rem(pl.program_id(0), tiles_per_seq)
    att_w = att_ref.shape[1]
    gc = xp_ref.shape[1] // len(POOL_WINDOWS)
    out = y_ref[...] + jnp.dot(att_ref[...], wo_ref[0:att_w, :], preferred_element_type=F32)
    full_scr[0:POOL_HALO, :] = jnp.where(t_in_seq == 0, 0.0, halo_ref[...])
    full_scr[POOL_HALO:, :] = xp_ref[...]
    pos = t_in_seq * tm + lax.broadcasted_iota(jnp.int32, (tm, 1), 0)
    diffs = []
    for g, win in enumerate(POOL_WINDOWS):
        cols = slice(g * gc, (g + 1) * gc)
        x = xp_ref[:, cols]
        acc = x
        for d in range(1, win):
            acc = acc + full_scr[POOL_HALO - d:POOL_HALO - d + tm, cols]
        cnt = jnp.minimum(pos + 1, win).astype(F32)
        diffs.append(acc / cnt - x)
    pool = _pool_project(diffs, pw_ref, ps_ref)
    o_ref[...] = out + jnp.dot(pool, wo_ref[att_w:, :], preferred_element_type=F32)


def _mix_prompt(att, qx, pool_w, pool_scale, w_out, y, *, seq, tm):
    m, d = y.shape
    aw, pw = att.shape[1], qx.shape[2]
    n_g, gc = pool_w.shape[0], pool_w.shape[1]
    tiles_per_seq = seq // tm
    halo_blocks = tm // POOL_HALO
    vmem = _vmem_limit(2 * tm * (aw * 2 + pw * 4 + 2 * d * 4), 2 * POOL_HALO * pw * 4,
                       2 * (aw + pw) * d * 2, 2 * n_g * gc * gc * 2,
                       (tm + POOL_HALO) * pw * 4, 4 * tm * pw * 4)
    return pl.pallas_call(
        functools.partial(_mix_kernel, tm=tm, tiles_per_seq=tiles_per_seq),
        grid=(m // tm,),
        in_specs=[
            pl.BlockSpec((tm, aw), lambda i: (i, 0)),
            pl.BlockSpec((None, tm, pw), lambda i: (QX_POOL, i, 0)),
            pl.BlockSpec((None, POOL_HALO, pw),
                         lambda i: (QX_POOL, jnp.maximum(i * halo_blocks - 1, 0), 0)),
            pl.BlockSpec((n_g, gc, gc), lambda i: (0, 0, 0)),
            pl.BlockSpec((1, pw), lambda i: (0, 0)),
            pl.BlockSpec((aw + pw, d), lambda i: (0, 0)),
            pl.BlockSpec((tm, d), lambda i: (i, 0)),
        ],
        out_specs=pl.BlockSpec((tm, d), lambda i: (i, 0)),
        out_shape=jax.ShapeDtypeStruct((m, d), F32),
        scratch_shapes=[pltpu.VMEM((tm + POOL_HALO, pw), F32)],
        compiler_params=pltpu.CompilerParams(
            dimension_semantics=("arbitrary",), vmem_limit_bytes=vmem),
        name="mix_prompt",
    )(att, qx, qx, pool_w, pool_scale, w_out, y)


def _mix_sample_kernel(att_ref, xp_ref, st_ref, pw_ref, ps_ref, wo_ref, y_ref, o_ref, *, pos0):
    att_w = att_ref.shape[1]
    gc = xp_ref.shape[1] // len(POOL_WINDOWS)
    row = lax.broadcasted_iota(jnp.int32, (1, POOL_HALO, 1), 1)
    diffs = []
    for g, win in enumerate(POOL_WINDOWS):
        cols = slice(g * gc, (g + 1) * gc)
        x = xp_ref[:, cols]
        prev = jnp.where(row >= POOL_HALO - (win - 1), st_ref[:, :, cols], 0.0)
        acc = x + jnp.sum(prev, axis=1)
        diffs.append(acc / float(min(pos0 + 1, win)) - x)
    pool = _pool_project(diffs, pw_ref, ps_ref)
    o_ref[...] = (y_ref[...]
                  + jnp.dot(att_ref[...].astype(BF16), wo_ref[0:att_w, :],
                            preferred_element_type=F32)
                  + jnp.dot(pool, wo_ref[att_w:, :], preferred_element_type=F32))


def _mix_sample(att, xp, state16, pool_w, pool_scale, w_out, y, *, pos0):
    whole = lambda a: pl.BlockSpec(a.shape, lambda: (0,) * a.ndim)
    args = (att, xp, state16, pool_w, pool_scale, w_out, y)
    return pl.pallas_call(
        functools.partial(_mix_sample_kernel, pos0=pos0),
        in_specs=[whole(a) for a in args],
        out_specs=whole(y),
        out_shape=jax.ShapeDtypeStruct(y.shape, F32),
        compiler_params=pltpu.CompilerParams(
            vmem_limit_bytes=_vmem_limit(*(2 * a.size * a.dtype.itemsize for a in args),
                                         4 * state16.size * 4)),
        name="mix_sample",
    )(*args)


def kernel(x_prompt, x_sample, cache_k, cache_v, state_pool, page_table, w_in, w_out, ffn1_norm,
           ffn1_gate, ffn1_up, ffn1_down, mix_norm, ffn2_norm, ffn2_gate, ffn2_up, ffn2_down,
           lambda_q1, lambda_k1, lambda_q2, lambda_k2, subln_g, pool_w, pool_scale, rel_bias,
           final_norm):
    batch, seq, d = x_prompt.shape
    db, sd, _ = x_sample.shape
    depth, n_phys, page, n_heads, dv = cache_v.shape
    dk = dv // 2
    width = n_heads * dv
    pool_width = state_pool.shape[-1]
    assert sd == 1 and page == PAGE_SIZE and cache_k.shape[-1] == 2 * dk
    assert w_in.shape[-1] == 4 * width and pool_width == width
    assert state_pool.shape[2] == POOL_HALO - 1 and rel_bias.shape[0] == NUM_BUCKETS
    n_past = page_table.shape[1] * PAGE_SIZE
    scale = dk ** -0.5 * LOG2E

    tm_p = min(512, batch * seq)
    tm_ffn = min(1024, batch * seq)
    tf = 512
    t_att = min(512, seq)
    tq_att = 128
    pages_step = min(16, page_table.shape[1])
    assert seq % tm_p == 0 and seq % t_att == 0 and page_table.shape[1] % pages_step == 0

    row2 = lambda a: a.reshape(1, -1)
    yp = x_prompt.reshape(batch * seq, d)
    ys = x_sample.reshape(db * sd, d)
    bias = _bias_tiles(rel_bias, tq=tq_att)
    far = rel_bias[NUM_BUCKETS - 1]
    tab_dec = jnp.tile((rel_bias - far[None, :]).T * LOG2E, (2, 1))
    outs = {k: [] for k in ("kp", "vp", "pp", "ks", "vs", "ps")}

    for l in range(depth):
        lam_init = 0.8 - 0.6 * math.exp(-0.3 * l)
        lams = tuple(row2(a[l]) for a in (lambda_q1, lambda_k1, lambda_q2, lambda_k2))
        ffn_w = []
        for gate, up, down in ((ffn1_gate, ffn1_up, ffn1_down), (ffn2_gate, ffn2_up, ffn2_down)):
            ffn_w.append((gate[l].astype(BF16), up[l].astype(BF16), down[l].astype(BF16)))
        w_in_b = w_in[l].astype(BF16)
        w_out_b = w_out[l].astype(BF16)
        pool_w_b = pool_w[l].astype(BF16)
        pool_s = row2(pool_scale[l])
        last = l == depth - 1
        fin = row2(final_norm)

        yp = _ffn(yp, row2(ffn1_norm[l]), *ffn_w[0], fin, tm=tm_ffn, tf=tf, final=False,
                  name="ffn1_prompt")
        qx, k, v = _proj(yp, row2(mix_norm[l]), w_in_b, tm=tm_p, scale=scale,
                         name="proj_prompt")
        ys = _ffn(ys, row2(ffn1_norm[l]), *ffn_w[0], fin, tm=db, tf=tf, final=False,
                  name="ffn1_sample")
        qxs, ks, vs = _proj(ys, row2(mix_norm[l]), w_in_b, tm=db, scale=scale,
                            name="proj_sample")
        qs, xps = qxs[QX_Q], qxs[QX_POOL]

        att, att_s = _attention(
            functools.partial(_attn_prompt_operands, qx, k, v, bias, lams, row2(subln_g[l]),
                              seq=seq, n_heads=n_heads, t=t_att, tq=tq_att),
            functools.partial(_attn_decode_operands, qs.reshape(db, n_heads, dv),
                              ks.reshape(db, n_heads, dv), vs.reshape(db, n_heads, dv),
                              cache_k[l], cache_v[l], tab_dec, lams, row2(subln_g[l]),
                              n_pages_step=pages_step),
            page_table, grid_prompt=(n_heads, batch, seq // t_att),
            n_chunks=page_table.shape[1] // pages_step,
            attn_kw=dict(t=t_att, tq=tq_att, dk=dk, lam_init=lam_init),
            decode_kw=dict(n_pages_step=pages_step, n_heads=n_heads, dv=dv, lam_init=lam_init))

        yp = _mix_prompt(att, qx, pool_w_b, pool_s, w_out_b, yp, seq=seq, tm=tm_p)
        yp = _ffn(yp, row2(ffn2_norm[l]), *ffn_w[1], fin, tm=tm_ffn, tf=tf, final=last,
                  name="ffn2_prompt")
        outs["kp"].append(k.reshape(batch, seq, n_heads, dv))
        outs["vp"].append(v.reshape(batch, seq, n_heads, dv))
        outs["pp"].append(
            qx.reshape(2, batch, seq, pool_width)[QX_POOL, :, seq - (POOL_HALO - 1):])

        state16 = jnp.pad(state_pool[l], ((0, 0), (1, 0), (0, 0)))
        ys = _mix_sample(att_s.reshape(db, width), xps, state16, pool_w_b, pool_s, w_out_b, ys,
                         pos0=n_past)
        ys = _ffn(ys, row2(ffn2_norm[l]), *ffn_w[1], fin, tm=db, tf=tf, final=last,
                  name="ffn2_sample")
        outs["ks"].append(ks.reshape(db, sd, n_heads, dv))
        outs["vs"].append(vs.reshape(db, sd, n_heads, dv))
        outs["ps"].append(jnp.concatenate([state_pool[l][:, 1:], xps[:, None, :]], axis=1))

    st = lambda name: jnp.stack(outs[name], axis=0)
    return (yp.reshape(batch, seq, d), ys.reshape(db, sd, d), st("kp"), st("vp"), st("pp"),
            st("ks"), st("vs"), st("ps"))
```

```python
import functools
import math
from typing import NamedTuple

import jax
import jax.numpy as jnp
from jax import lax
from jax.experimental import pallas as pl
from jax.experimental.pallas import tpu as pltpu

F32 = jnp.float32
BF16 = jnp.bfloat16

V7X_LANES = 128
V7X_VMEM_BYTES = 64 * 1024 * 1024
V7X_VMEM_CAP = V7X_VMEM_BYTES - 8 * 1024 * 1024

EPS = 1e-6
SUBLN_EPS = 1e-5
NEG = -1e30
POOL_WINDOWS = (2, 4, 8, 16)
POOL_HALO = 16
NUM_BUCKETS = 32
MAX_EXACT = NUM_BUCKETS // 2
MAX_DISTANCE = 128
LOG2E = math.log2(math.e)
PAGE_SIZE = 128
DECODE_GROUP = 4


VMEM_LIMIT_FLOOR = 16 * 1024 * 1024


def _vmem_limit(*nbytes):
    est = int(sum(nbytes))
    return min(max(est, VMEM_LIMIT_FLOOR), V7X_VMEM_CAP)


def _rms(x, g, eps):
    return x * lax.rsqrt(jnp.mean(x * x, axis=-1, keepdims=True) + eps) * g


def _bucket(n):
    n = jnp.maximum(n, 0)
    nf = jnp.maximum(n, 1).astype(F32)
    large = MAX_EXACT + (jnp.log(nf / MAX_EXACT) / math.log(MAX_DISTANCE / MAX_EXACT)
                         * (NUM_BUCKETS - MAX_EXACT)).astype(jnp.int32)
    large = jnp.minimum(large, NUM_BUCKETS - 1)
    return jnp.where(n < MAX_EXACT, n, large)


def _ffn_kernel(x_ref, g_ref, wg_ref, wu_ref, wd_ref, fg_ref, o_ref, h_scr, *, n_f, last_tf,
                final):
    f = pl.program_id(1)
    tf = wg_ref.shape[1]

    @pl.when(f == 0)
    def _():
        h_scr[...] = _rms(x_ref[...], g_ref[...], EPS).astype(BF16)
        o_ref[...] = jnp.zeros(o_ref.shape, F32)

    def accumulate(cols):
        h = h_scr[...]
        gate = jnp.dot(h, wg_ref[:, 0:cols], preferred_element_type=F32)
        up = jnp.dot(h, wu_ref[:, 0:cols], preferred_element_type=F32)
        act = (gate / (1.0 + jnp.exp(-gate)) * up).astype(BF16)
        o_ref[...] += jnp.dot(act, wd_ref[0:cols, :], preferred_element_type=F32)

    if last_tf == tf:
        accumulate(tf)
    else:
        @pl.when(f < n_f - 1)
        def _():
            accumulate(tf)

        @pl.when(f == n_f - 1)
        def _():
            accumulate(last_tf)

    @pl.when(f == n_f - 1)
    def _():
        y = x_ref[...] + 0.5 * o_ref[...]
        if final:
            y = _rms(y, fg_ref[...], EPS)
        o_ref[...] = y


def _ffn(x, norm_g, wg, wu, wd, final_g, *, tm, tf, final, name):
    m, d = x.shape
    f_dim = wg.shape[1]
    n_f = pl.cdiv(f_dim, tf)
    last_tf = f_dim - (n_f - 1) * tf
    assert last_tf % V7X_LANES == 0
    vmem = _vmem_limit(2 * 2 * tm * d * 4,
                       2 * 3 * d * tf * 2,
                       tm * d * 2,
                       4 * tm * tf * 4 + tm * d * 4)
    return pl.pallas_call(
        functools.partial(_ffn_kernel, n_f=n_f, last_tf=last_tf, final=final),
        grid=(m // tm, n_f),
        in_specs=[
            pl.BlockSpec((tm, d), lambda i, f: (i, 0)),
            pl.BlockSpec((1, d), lambda i, f: (0, 0)),
            pl.BlockSpec((d, tf), lambda i, f: (0, f)),
            pl.BlockSpec((d, tf), lambda i, f: (0, f)),
            pl.BlockSpec((tf, d), lambda i, f: (f, 0)),
            pl.BlockSpec((1, d), lambda i, f: (0, 0)),
        ],
        out_specs=pl.BlockSpec((tm, d), lambda i, f: (i, 0)),
        out_shape=jax.ShapeDtypeStruct((m, d), F32),
        scratch_shapes=[pltpu.VMEM((tm, d), BF16)],
        compiler_params=pltpu.CompilerParams(
            dimension_semantics=("arbitrary", "arbitrary"), vmem_limit_bytes=vmem),
        name=name,
    )(x, norm_g, wg, wu, wd, final_g)


PROJ_Q, PROJ_K, PROJ_V, PROJ_POOL = range(4)
QX_Q, QX_POOL = range(2)


def _proj_kernel(x_ref, g_ref, w_ref, qx_ref, k_ref, v_ref, h_scr, *, scale):
    j = pl.program_id(1)

    @pl.when(j == 0)
    def _():
        h_scr[...] = _rms(x_ref[...], g_ref[...], EPS).astype(BF16)

    p = jnp.dot(h_scr[...], w_ref[...], preferred_element_type=F32)

    @pl.when(j == PROJ_K)
    def _():
        k_ref[...] = p

    @pl.when(j == PROJ_V)
    def _():
        v_ref[...] = p

    @pl.when((j == PROJ_Q) | (j == PROJ_POOL))
    def _():
        qx_ref[...] = p * jnp.where(j == PROJ_Q, scale, 1.0)


def _proj(x, norm_g, w_in, *, tm, scale, name):
    m, d = x.shape
    w = w_in.shape[1] // 4
    kv_spec = pl.BlockSpec((tm, w), lambda i, j: (i, 0))
    vmem = _vmem_limit(2 * tm * d * 4, 2 * d * w * 2, 3 * 2 * tm * w * 4, tm * d * 2,
                       2 * tm * w * 4)
    return pl.pallas_call(
        functools.partial(_proj_kernel, scale=scale),
        grid=(m // tm, 4),
        in_specs=[
            pl.BlockSpec((tm, d), lambda i, j: (i, 0)),
            pl.BlockSpec((1, d), lambda i, j: (0, 0)),
            pl.BlockSpec((d, w), lambda i, j: (0, j)),
        ],
        out_specs=[pl.BlockSpec((None, tm, w),
                                lambda i, j: (jnp.where(j == PROJ_POOL, QX_POOL, QX_Q), i, 0)),
                   kv_spec, kv_spec],
        out_shape=[jax.ShapeDtypeStruct((2, m, w), F32)] + [jax.ShapeDtypeStruct((m, w), F32)] * 2,
        scratch_shapes=[pltpu.VMEM((tm, d), BF16)],
        compiler_params=pltpu.CompilerParams(
            dimension_semantics=("arbitrary", "arbitrary"), vmem_limit_bytes=vmem),
        name=name,
    )(x, norm_g, w_in)


def _bias_tile_kernel(tab_ref, o_ref, *, tq):
    h = pl.program_id(0)
    r = lax.broadcasted_iota(jnp.int32, (2 * tq, 2 * tq), 0)
    c = lax.broadcasted_iota(jnp.int32, (2 * tq, 2 * tq), 1)
    key = r - tq
    qry = c % tq
    far = tab_ref[h * NUM_BUCKETS + NUM_BUCKETS - 1]
    bucket = _bucket(qry - key)
    val = jnp.zeros(r.shape, F32)
    for b in range(NUM_BUCKETS - 1):
        val = jnp.where(bucket == b, (tab_ref[h * NUM_BUCKETS + b] - far) * LOG2E, val)
    o_ref[...] = jnp.where(key <= qry, val, NEG)


def _bias_tiles(rel_bias, *, tq):
    n_heads = rel_bias.shape[1]
    tab = rel_bias.T.reshape(-1)
    return pl.pallas_call(
        functools.partial(_bias_tile_kernel, tq=tq),
        grid=(n_heads,),
        in_specs=[pl.BlockSpec(memory_space=pltpu.SMEM)],
        out_specs=pl.BlockSpec((None, 2 * tq, 2 * tq), lambda h: (h, 0, 0)),
        out_shape=jax.ShapeDtypeStruct((n_heads, 2 * tq, 2 * tq), F32),
        compiler_params=pltpu.CompilerParams(
            dimension_semantics=("arbitrary",),
            vmem_limit_bytes=_vmem_limit(16 * 2 * tq * 2 * tq * 4)),
        name="rel_bias_tiles",
    )(tab)


def _lambda(lq1_ref, lk1_ref, lq2_ref, lk2_ref, lam_init):
    a = jnp.sum(lq1_ref[...] * lk1_ref[...], axis=-1, keepdims=True)
    b = jnp.sum(lq2_ref[...] * lk2_ref[...], axis=-1, keepdims=True)
    return jnp.exp(a) - jnp.exp(b) + lam_init


def _attn_body(qi, lq1_ref, lk1_ref, lq2_ref, lk2_ref, g_ref, q_ref, k_ref, v_ref, bias_ref,
               o_ref, kb_scr, vt_scr, m_scr, l_scr, acc_scr, *, t, tq, dk, lam_init):
    n_strips = t // tq
    seq = k_ref.shape[0]
    sw = 2 * tq

    @pl.when(qi == 0)
    def _():
        kb_scr[0:tq, :] = jnp.zeros((tq, kb_scr.shape[1]), BF16)
        vt_scr[0] = jnp.zeros(vt_scr.shape[1:], BF16)
        kb_scr[tq:, :] = k_ref[...].astype(BF16)
        for c in range(seq // tq):
            vt_scr[c + 1] = v_ref[c * tq:(c + 1) * tq, :].T.astype(BF16)

    qa = q_ref[...].astype(F32)
    lane = lax.broadcasted_iota(jnp.int32, qa.shape, 1)
    q1 = jnp.where(lane < dk, qa, 0.0).astype(BF16)
    q2 = jnp.where(lane >= dk, qa, 0.0).astype(BF16)
    strips = [jnp.concatenate([q1[u * tq:(u + 1) * tq], q2[u * tq:(u + 1) * tq]], axis=0)
              for u in range(n_strips)]
    qs_all = jnp.concatenate(strips, axis=0)

    m_scr[...] = jnp.full(m_scr.shape, NEG, F32)
    l_scr[...] = jnp.zeros(l_scr.shape, F32)
    acc_scr[...] = jnp.zeros(acc_scr.shape, F32)

    def keys(row0, n):
        return kb_scr[pl.ds(pl.multiple_of(row0, tq), n), :]

    def values_t(chunk0, n_chunks):
        return jnp.concatenate([vt_scr[chunk0 + i] for i in range(n_chunks)], axis=1)

    def fold(lanes, pieces):
        m_prev = m_scr[:, lanes]
        m_new = m_prev
        for s, _ in pieces:
            m_new = jnp.maximum(m_new, jnp.max(s, axis=0, keepdims=True))
        alpha = jnp.exp2(m_prev - m_new)
        l_new = alpha * l_scr[:, lanes]
        pv = None
        for s, vt in pieces:
            p = jnp.exp2(s - m_new)
            l_new = l_new + jnp.sum(p, axis=0, keepdims=True)
            term = jnp.dot(vt, p.astype(BF16), preferred_element_type=F32)
            pv = term if pv is None else pv + term
        l_scr[:, lanes] = l_new
        acc_scr[:, lanes] = alpha * acc_scr[:, lanes] + pv
        m_scr[:, lanes] = m_new

    def scores(kc, qs):
        return lax.dot_general(kc, qs, (((1,), (1,)), ((), ())), preferred_element_type=F32)

    far_chunks = t // tq
    every = slice(0, n_strips * sw)

    def fold_far(blocks):
        scored = []
        for chunk0, n_chunks in blocks:
            kc = keys(chunk0 * tq, n_chunks * tq)
            scored.append((scores(kc, qs_all), values_t(chunk0, n_chunks)))
        for piece in scored:
            fold(every, [piece])

    n_far_blocks = jnp.maximum(qi - 1, 0)
    n_pairs = n_far_blocks // 2

    def far_body(j, carry):
        c0 = 1 + j * 2 * far_chunks
        fold_far([(c0, far_chunks), (c0 + far_chunks, far_chunks)])
        return carry

    lax.fori_loop(0, n_pairs, far_body, 0)
    tail0 = 1 + n_pairs * 2 * far_chunks

    @pl.when((qi > 0) & (n_far_blocks % 2 == 0))
    def _():
        fold_far([(tail0, far_chunks - 1)])

    @pl.when(n_far_blocks % 2 == 1)
    def _():
        fold_far([(tail0, far_chunks), (tail0 + far_chunks, far_chunks - 1)])

    n_absent = jnp.where(qi == 0, tq, 0)
    pending = []
    near_ahead = 1
    for u in range(n_strips):
        lanes = slice(u * sw, (u + 1) * sw)
        qs = strips[u]
        pieces = []
        if u > 0:
            s_a = scores(keys(qi * t, u * tq), qs)
            row = lax.broadcasted_iota(jnp.int32, s_a.shape, 0)
            s_a = jnp.where(row < n_absent, NEG, s_a)
            pieces.append((s_a, values_t(qi * far_chunks, u)))
        bias = bias_ref[...]
        if u == 0:
            row = lax.broadcasted_iota(jnp.int32, bias.shape, 0)
            bias = jnp.where(row < n_absent, NEG, bias)
        s_b = scores(keys(qi * t + u * tq, 2 * tq), qs) + bias
        pieces.append((s_b, values_t(qi * far_chunks + u, 2)))
        pending.append((lanes, pieces))
        if len(pending) > near_ahead:
            fold(*pending.pop(0))
    for item in pending:
        fold(*item)

    o = acc_scr[...] / l_scr[...]
    lam = _lambda(lq1_ref, lk1_ref, lq2_ref, lk2_ref, lam_init)
    for u in range(n_strips):
        w = o[:, u * sw:u * sw + tq] - lam * o[:, u * sw + tq:(u + 1) * sw]
        y = (w * lax.rsqrt(jnp.mean(w * w, axis=0, keepdims=True) + SUBLN_EPS)
             * g_ref[...] * (1.0 - lam_init))
        o_ref[u * tq:(u + 1) * tq, :] = y.T.astype(o_ref.dtype)


class _Operands(NamedTuple):
    args: tuple
    in_specs: list
    out_spec: pl.BlockSpec
    out_shape: jax.ShapeDtypeStruct
    scratch: list
    vmem_bytes: int


def _attn_prompt_operands(qx, k, v, bias, lams, subln_g, *, seq, n_heads, t, tq, index):
    dv = k.shape[1] // n_heads
    nq = seq // t
    assert t % tq == 0 and tq >= MAX_DISTANCE
    at = lambda f: (lambda *g: f(*index(*g)))
    g_cols = jnp.broadcast_to(subln_g.reshape(dv, 1), (dv, tq))
    lam_spec = pl.BlockSpec((1, dv // 2), at(lambda h, b, i: (0, 0)))
    kv_spec = pl.BlockSpec((seq, dv), at(lambda h, b, i: (b, h)))
    return _Operands(
        args=(*lams, g_cols, qx, k, v, bias),
        in_specs=[lam_spec, lam_spec, lam_spec, lam_spec,
                  pl.BlockSpec((dv, tq), at(lambda h, b, i: (0, 0))),
                  pl.BlockSpec((None, t, dv), at(lambda h, b, i: (QX_Q, b * nq + i, h))),
                  kv_spec, kv_spec,
                  pl.BlockSpec((None, 2 * tq, 2 * tq), at(lambda h, b, i: (h, 0, 0)))],
        out_spec=pl.BlockSpec((t, dv), at(lambda h, b, i: (b * nq + i, h))),
        out_shape=jax.ShapeDtypeStruct(k.shape, BF16),
        scratch=[pltpu.VMEM((seq + tq, dv), BF16), pltpu.VMEM((seq // tq + 1, dv, tq), BF16),
                 pltpu.VMEM((1, 2 * t), F32), pltpu.VMEM((1, 2 * t), F32),
                 pltpu.VMEM((dv, 2 * t), F32)],
        vmem_bytes=(2 * 2 * seq * dv * 4 + 2 * (seq + tq) * dv * 2 + 2 * 2 * tq * 2 * tq * 4
                    + 4 * t * dv * 2 + dv * 2 * t * 4 + 6 * t * 2 * t * 4))


def _page_copies(pt_ref, ck_ref, cv_ref, kbuf, vbuf, sems, step, *, n_c, p_n):
    seq, c, slot = lax.div(step, n_c), lax.rem(step, n_c), lax.rem(step, 2)
    copies = []
    for j in range(p_n):
        page = pt_ref[seq, c * p_n + j]
        copies.append(pltpu.make_async_copy(ck_ref.at[page], kbuf.at[slot, j], sems.at[0, slot]))
        copies.append(pltpu.make_async_copy(cv_ref.at[page], vbuf.at[slot, j], sems.at[1, slot]))
    return copies


def _decode_body(step, n_steps, n_c, pt_ref, lq1_ref, lk1_ref, lq2_ref, lk2_ref, g_ref, tab_ref,
                 q_ref, kn_ref, vn_ref, ck_ref, cv_ref, o_ref, qm_scr, bias_scr, m_scr, l_scr,
                 acc_scr, kbuf, vbuf, sems, *, n_pages_step, n_heads, dv, lam_init):
    p_n = n_pages_step
    c = lax.rem(step, n_c)
    slot = lax.rem(step, 2)
    rows = 2 * n_heads
    page_cols = PAGE_SIZE * n_heads
    copies = functools.partial(_page_copies, pt_ref, ck_ref, cv_ref, kbuf, vbuf, sems,
                               n_c=n_c, p_n=p_n)

    @pl.when(step == 0)
    def _():
        for cp in copies(step):
            cp.start()

    @pl.when(step + 1 < n_steps)
    def _():
        for cp in copies(step + 1):
            cp.start()

    for cp in copies(step):
        cp.wait()
    k_refs = [kbuf.at[slot, j] for j in range(p_n)]
    v_refs = [vbuf.at[slot, j] for j in range(p_n)]

    @pl.when(c == 0)
    def _():
        q = q_ref[...].astype(F32)
        lane = lax.broadcasted_iota(jnp.int32, q.shape, 1)
        qm_scr[...] = jnp.concatenate(
            [jnp.where(lane < dv // 2, q, 0.0), jnp.where(lane >= dv // 2, q, 0.0)],
            axis=0).astype(BF16)
        key_row = lax.broadcasted_iota(jnp.int32, (1, page_cols), 1) // n_heads
        bucket = _bucket(PAGE_SIZE - key_row)
        val = jnp.zeros((rows, page_cols), F32)
        for b in range(NUM_BUCKETS - 1):
            val = jnp.where(bucket == b, tab_ref[:, b:b + 1], val)
        bias_scr[...] = val
        m_scr[...] = jnp.full(m_scr.shape, NEG, F32)
        l_scr[...] = jnp.zeros(l_scr.shape, F32)
        acc_scr[...] = jnp.zeros(acc_scr.shape, F32)

    qm = qm_scr[...]
    row = lax.broadcasted_iota(jnp.int32, (rows, page_cols), 0)
    col = lax.broadcasted_iota(jnp.int32, (rows, page_cols), 1)
    own = (col % n_heads) == (row % n_heads)
    near = jnp.where(c == n_c - 1, bias_scr[...], 0.0)
    pieces = []
    for j in range(p_n):
        kf = k_refs[j][...].reshape(page_cols, dv).astype(BF16)
        sj = lax.dot_general(qm, kf, (((1,), (1,)), ((), ())), preferred_element_type=F32)
        if j == p_n - 1:
            sj = sj + near
        pieces.append(jnp.where(own, sj, NEG))

    def update(s, pv_fn):
        m_prev = m_scr[...]
        m_new = jnp.maximum(m_prev, jnp.max(s, axis=-1, keepdims=True))
        p = jnp.exp2(s - m_new)
        alpha = jnp.exp2(m_prev - m_new)
        l_scr[...] = alpha * l_scr[...] + jnp.sum(p, axis=-1, keepdims=True)
        acc_scr[...] = alpha * acc_scr[...] + pv_fn(p)
        m_scr[...] = m_new

    def pv_pages(first):
        def pv(p):
            pb = p.astype(BF16)
            out = None
            for j in range(DECODE_GROUP):
                vf = v_refs[first + j][...].reshape(page_cols, dv).astype(BF16)
                term = jnp.dot(pb[:, j * page_cols:(j + 1) * page_cols], vf,
                               preferred_element_type=F32)
                out = term if out is None else out + term
            return out
        return pv

    for first in range(0, p_n, DECODE_GROUP):
        update(jnp.concatenate(pieces[first:first + DECODE_GROUP], axis=1), pv_pages(first))

    @pl.when(c == n_c - 1)
    def _():
        kn = jnp.concatenate([kn_ref[...]] * 2, axis=0)
        vn = jnp.concatenate([vn_ref[...]] * 2, axis=0)
        s_self = jnp.sum(qm.astype(F32) * kn, axis=-1, keepdims=True) + tab_ref[:, 0:1]
        update(s_self, lambda p: p * vn)
        o = acc_scr[...] / l_scr[...]
        lam = _lambda(lq1_ref, lk1_ref, lq2_ref, lk2_ref, lam_init)
        w = o[0:n_heads] - lam * o[n_heads:rows]
        o_ref[...] = _rms(w, g_ref[...], SUBLN_EPS) * (1.0 - lam_init)


def _attn_decode_operands(q, k_new, v_new, cache_k, cache_v, tab, lams, subln_g, *,
                          n_pages_step, index):
    db, n_heads, dv = q.shape
    p_n = n_pages_step
    rows = 2 * n_heads
    page_cols = PAGE_SIZE * n_heads
    assert PAGE_SIZE >= MAX_DISTANCE and p_n % DECODE_GROUP == 0
    at = lambda f: (lambda *g: f(index(*g)))
    small = lambda shape: pl.BlockSpec(shape, at(lambda s: (0,) * len(shape)))
    row_spec = pl.BlockSpec((None, n_heads, dv), at(lambda s: (s, 0, 0)))
    hbm = pl.BlockSpec(memory_space=pl.ANY)
    slots = (2, p_n, PAGE_SIZE, n_heads, dv)
    return _Operands(
        args=(*lams, subln_g, tab, q, k_new, v_new, cache_k, cache_v),
        in_specs=[small((1, dv // 2))] * 4 + [small((1, dv)), small((rows, NUM_BUCKETS)),
                                              row_spec, row_spec, row_spec, hbm, hbm],
        out_spec=row_spec,
        out_shape=jax.ShapeDtypeStruct((db, n_heads, dv), F32),
        scratch=[pltpu.VMEM((rows, dv), BF16), pltpu.VMEM((rows, page_cols), F32),
                 pltpu.VMEM((rows, 1), F32), pltpu.VMEM((rows, 1), F32),
                 pltpu.VMEM((rows, dv), F32), pltpu.VMEM(slots, F32), pltpu.VMEM(slots, F32),
                 pltpu.SemaphoreType.DMA((2, 2))],
        vmem_bytes=(2 * 2 * p_n * page_cols * dv * 4
                    + 2 * 2 * page_cols * dv * 2
                    + 6 * rows * p_n * page_cols * 4))


def _attention(prompt, decode, page_table, *, grid_prompt, n_chunks, attn_kw, decode_kw):
    n_heads, batch, nq = grid_prompt

    def step_of(h, b, i):
        return (h * batch + b) * nq + i

    n_seq = page_table.shape[0]
    n_steps = n_seq * n_chunks
    if n_heads * batch * nq == n_steps:
        pr = prompt(index=lambda h, b, i, pt: (h, b, i))
        de = decode(index=lambda h, b, i, pt: lax.div(step_of(h, b, i), n_chunks))
        n_pr, n_de = len(pr.args), len(de.args)

        def fused(pt_ref, *refs):
            ins, outs, scr = refs[:n_pr + n_de], refs[n_pr + n_de:n_pr + n_de + 2], \
                refs[n_pr + n_de + 2:]
            h, b, i = pl.program_id(0), pl.program_id(1), pl.program_id(2)
            _attn_body(i, *ins[:n_pr], outs[0], *scr[:len(pr.scratch)], **attn_kw)
            _decode_body(step_of(h, b, i), n_steps, n_chunks, pt_ref, *ins[n_pr:], outs[1],
                         *scr[len(pr.scratch):], **decode_kw)

        return pl.pallas_call(
            fused,
            grid_spec=pltpu.PrefetchScalarGridSpec(
                num_scalar_prefetch=1, grid=grid_prompt,
                in_specs=pr.in_specs + de.in_specs, out_specs=[pr.out_spec, de.out_spec],
                scratch_shapes=pr.scratch + de.scratch),
            out_shape=[pr.out_shape, de.out_shape],
            compiler_params=pltpu.CompilerParams(
                dimension_semantics=("arbitrary",) * 3,
                vmem_limit_bytes=_vmem_limit(pr.vmem_bytes, de.vmem_bytes)),
            name="attn_prompt_decode",
        )(page_table, *pr.args, *de.args)

    pr = prompt(index=lambda h, b, i: (h, b, i))
    att = pl.pallas_call(
        lambda *refs: _attn_body(pl.program_id(2), *refs, **attn_kw),
        grid=grid_prompt, in_specs=pr.in_specs, out_specs=pr.out_spec, out_shape=pr.out_shape,
        scratch_shapes=pr.scratch,
        compiler_params=pltpu.CompilerParams(
            dimension_semantics=("arbitrary",) * 3, vmem_limit_bytes=_vmem_limit(pr.vmem_bytes)),
        name="attn_prompt",
    )(*pr.args)
    de = decode(index=lambda s, c, pt: s)
    att_s = pl.pallas_call(
        lambda pt_ref, *refs: _decode_body(pl.program_id(0) * n_chunks + pl.program_id(1),
                                           n_steps, n_chunks, pt_ref, *refs, **decode_kw),
        grid_spec=pltpu.PrefetchScalarGridSpec(
            num_scalar_prefetch=1, grid=(n_seq, n_chunks), in_specs=de.in_specs,
            out_specs=de.out_spec, scratch_shapes=de.scratch),
        out_shape=de.out_shape,
        compiler_params=pltpu.CompilerParams(
            dimension_semantics=("arbitrary",) * 2, vmem_limit_bytes=_vmem_limit(de.vmem_bytes)),
        name="attn_decode",
    )(page_table, *de.args)
    return att, att_s


def _pool_project(diffs, pw_ref, ps_ref):
    outs = [jnp.dot(d.astype(BF16), pw_ref[g], preferred_element_type=F32)
            for g, d in enumerate(diffs)]
    return (jnp.concatenate(outs, axis=1) * ps_ref[...]).astype(BF16)


def _mix_kernel(att_ref, xp_ref, halo_ref, pw_ref, ps_ref, wo_ref, y_ref, o_ref, full_scr,
                *, tm, tiles_per_seq):
    t_in_seq = lax.rem(pl.program_id(0), tiles_per_seq)
    att_w = att_ref.shape[1]
    gc = xp_ref.shape[1] // len(POOL_WINDOWS)
    out = y_ref[...] + jnp.dot(att_ref[...], wo_ref[0:att_w, :], preferred_element_type=F32)
    full_scr[0:POOL_HALO, :] = jnp.where(t_in_seq == 0, 0.0, halo_ref[...])
    full_scr[POOL_HALO:, :] = xp_ref[...]
    pos = t_in_seq * tm + lax.broadcasted_iota(jnp.int32, (tm, 1), 0)
    diffs = []
    for g, win in enumerate(POOL_WINDOWS):
        cols = slice(g * gc, (g + 1) * gc)
        x = xp_ref[:, cols]
        acc = x
        for d in range(1, win):
            acc = acc + full_scr[POOL_HALO - d:POOL_HALO - d + tm, cols]
        cnt = jnp.minimum(pos + 1, win).astype(F32)
        diffs.append(acc / cnt - x)
    pool = _pool_project(diffs, pw_ref, ps_ref)
    o_ref[...] = out + jnp.dot(pool, wo_ref[att_w:, :], preferred_element_type=F32)


def _mix_prompt(att, qx, pool_w, pool_scale, w_out, y, *, seq, tm):
    m, d = y.shape
    aw, pw = att.shape[1], qx.shape[2]
    n_g, gc = pool_w.shape[0], pool_w.shape[1]
    tiles_per_seq = seq // tm
    halo_blocks = tm // POOL_HALO
    vmem = _vmem_limit(2 * tm * (aw * 2 + pw * 4 + 2 * d * 4), 2 * POOL_HALO * pw * 4,
                       2 * (aw + pw) * d * 2, 2 * n_g * gc * gc * 2,
                       (tm + POOL_HALO) * pw * 4, 4 * tm * pw * 4)
    return pl.pallas_call(
        functools.partial(_mix_kernel, tm=tm, tiles_per_seq=tiles_per_seq),
        grid=(m // tm,),
        in_specs=[
            pl.BlockSpec((tm, aw), lambda i: (i, 0)),
            pl.BlockSpec((None, tm, pw), lambda i: (QX_POOL, i, 0)),
            pl.BlockSpec((None, POOL_HALO, pw),
                         lambda i: (QX_POOL, jnp.maximum(i * halo_blocks - 1, 0), 0)),
            pl.BlockSpec((n_g, gc, gc), lambda i: (0, 0, 0)),
            pl.BlockSpec((1, pw), lambda i: (0, 0)),
            pl.BlockSpec((aw + pw, d), lambda i: (0, 0)),
            pl.BlockSpec((tm, d), lambda i: (i, 0)),
        ],
        out_specs=pl.BlockSpec((tm, d), lambda i: (i, 0)),
        out_shape=jax.ShapeDtypeStruct((m, d), F32),
        scratch_shapes=[pltpu.VMEM((tm + POOL_HALO, pw), F32)],
        compiler_params=pltpu.CompilerParams(
            dimension_semantics=("arbitrary",), vmem_limit_bytes=vmem),
        name="mix_prompt",
    )(att, qx, qx, pool_w, pool_scale, w_out, y)


def _mix_sample_kernel(att_ref, xp_ref, st_ref, pw_ref, ps_ref, wo_ref, y_ref, o_ref, *, pos0):
    att_w = att_ref.shape[1]
    gc = xp_ref.shape[1] // len(POOL_WINDOWS)
    row = lax.broadcasted_iota(jnp.int32, (1, POOL_HALO, 1), 1)
    diffs = []
    for g, win in enumerate(POOL_WINDOWS):
        cols = slice(g * gc, (g + 1) * gc)
        x = xp_ref[:, cols]
        prev = jnp.where(row >= POOL_HALO - (win - 1), st_ref[:, :, cols], 0.0)
        acc = x + jnp.sum(prev, axis=1)
        diffs.append(acc / float(min(pos0 + 1, win)) - x)
    pool = _pool_project(diffs, pw_ref, ps_ref)
    o_ref[...] = (y_ref[...]
                  + jnp.dot(att_ref[...].astype(BF16), wo_ref[0:att_w, :],
                            preferred_element_type=F32)
                  + jnp.dot(pool, wo_ref[att_w:, :], preferred_element_type=F32))


def _mix_sample(att, xp, state16, pool_w, pool_scale, w_out, y, *, pos0):
    whole = lambda a: pl.BlockSpec(a.shape, lambda: (0,) * a.ndim)
    args = (att, xp, state16, pool_w, pool_scale, w_out, y)
    return pl.pallas_call(
        functools.partial(_mix_sample_kernel, pos0=pos0),
        in_specs=[whole(a) for a in args],
        out_specs=whole(y),
        out_shape=jax.ShapeDtypeStruct(y.shape, F32),
        compiler_params=pltpu.CompilerParams(
            vmem_limit_bytes=_vmem_limit(*(2 * a.size * a.dtype.itemsize for a in args),
                                         4 * state16.size * 4)),
        name="mix_sample",
    )(*args)


def kernel(x_prompt, x_sample, cache_k, cache_v, state_pool, page_table, w_in, w_out, ffn1_norm,
           ffn1_gate, ffn1_up, ffn1_down, mix_norm, ffn2_norm, ffn2_gate, ffn2_up, ffn2_down,
           lambda_q1, lambda_k1, lambda_q2, lambda_k2, subln_g, pool_w, pool_scale, rel_bias,
           final_norm):
    batch, seq, d = x_prompt.shape
    db, sd, _ = x_sample.shape
    depth, n_phys, page, n_heads, dv = cache_v.shape
    dk = dv // 2
    width = n_heads * dv
    pool_width = state_pool.shape[-1]
    assert sd == 1 and page == PAGE_SIZE and cache_k.shape[-1] == 2 * dk
    assert w_in.shape[-1] == 4 * width and pool_width == width
    assert state_pool.shape[2] == POOL_HALO - 1 and rel_bias.shape[0] == NUM_BUCKETS
    n_past = page_table.shape[1] * PAGE_SIZE
    scale = dk ** -0.5 * LOG2E

    tm_p = min(512, batch * seq)
    tm_ffn = min(1024, batch * seq)
    tf = 512
    t_att = min(512, seq)
    tq_att = 128
    pages_step = min(16, page_table.shape[1])
    assert seq % tm_p == 0 and seq % t_att == 0 and page_table.shape[1] % pages_step == 0

    row2 = lambda a: a.reshape(1, -1)
    yp = x_prompt.reshape(batch * seq, d)
    ys = x_sample.reshape(db * sd, d)
    bias = _bias_tiles(rel_bias, tq=tq_att)
    far = rel_bias[NUM_BUCKETS - 1]
    tab_dec = jnp.tile((rel_bias - far[None, :]).T * LOG2E, (2, 1))
    outs = {k: [] for k in ("kp", "vp", "pp", "ks", "vs", "ps")}

    for l in range(depth):
        lam_init = 0.8 - 0.6 * math.exp(-0.3 * l)
        lams = tuple(row2(a[l]) for a in (lambda_q1, lambda_k1, lambda_q2, lambda_k2))
        ffn_w = []
        for gate, up, down in ((ffn1_gate, ffn1_up, ffn1_down), (ffn2_gate, ffn2_up, ffn2_down)):
            ffn_w.append((gate[l].astype(BF16), up[l].astype(BF16), down[l].astype(BF16)))
        w_in_b = w_in[l].astype(BF16)
        w_out_b = w_out[l].astype(BF16)
        pool_w_b = pool_w[l].astype(BF16)
        pool_s = row2(pool_scale[l])
        last = l == depth - 1
        fin = row2(final_norm)

        yp = _ffn(yp, row2(ffn1_norm[l]), *ffn_w[0], fin, tm=tm_ffn, tf=tf, final=False,
                  name="ffn1_prompt")
        qx, k, v = _proj(yp, row2(mix_norm[l]), w_in_b, tm=tm_p, scale=scale,
                         name="proj_prompt")
        ys = _ffn(ys, row2(ffn1_norm[l]), *ffn_w[0], fin, tm=db, tf=tf, final=False,
                  name="ffn1_sample")
        qxs, ks, vs = _proj(ys, row2(mix_norm[l]), w_in_b, tm=db, scale=scale,
                            name="proj_sample")
        qs, xps = qxs[QX_Q], qxs[QX_POOL]

        att, att_s = _attention(
            functools.partial(_attn_prompt_operands, qx, k, v, bias, lams, row2(subln_g[l]),
                              seq=seq, n_heads=n_heads, t=t_att, tq=tq_att),
            functools.partial(_attn_decode_operands, qs.reshape(db, n_heads, dv),
                              ks.reshape(db, n_heads, dv), vs.reshape(db, n_heads, dv),
                              cache_k[l], cache_v[l], tab_dec, lams, row2(subln_g[l]),
                              n_pages_step=pages_step),
            page_table, grid_prompt=(n_heads, batch, seq // t_att),
            n_chunks=page_table.shape[1] // pages_step,
            attn_kw=dict(t=t_att, tq=tq_att, dk=dk, lam_init=lam_init),
            decode_kw=dict(n_pages_step=pages_step, n_heads=n_heads, dv=dv, lam_init=lam_init))

        yp = _mix_prompt(att, qx, pool_w_b, pool_s, w_out_b, yp, seq=seq, tm=tm_p)
        yp = _ffn(yp, row2(ffn2_norm[l]), *ffn_w[1], fin, tm=tm_ffn, tf=tf, final=last,
                  name="ffn2_prompt")
        outs["kp"].append(k.reshape(batch, seq, n_heads, dv))
        outs["vp"].append(v.reshape(batch, seq, n_heads, dv))
        outs["pp"].append(
            qx.reshape(2, batch, seq, pool_width)[QX_POOL, :, seq - (POOL_HALO - 1):])

        state16 = jnp.pad(state_pool[l], ((0, 0), (1, 0), (0, 0)))
        ys = _mix_sample(att_s.reshape(db, width), xps, state16, pool_w_b, pool_s, w_out_b, ys,
                         pos0=n_past)
        ys = _ffn(ys, row2(ffn2_norm[l]), *ffn_w[1], fin, tm=db, tf=tf, final=last,
                  name="ffn2_sample")
        outs["ks"].append(ks.reshape(db, sd, n_heads, dv))
        outs["vs"].append(vs.reshape(db, sd, n_heads, dv))
        outs["ps"].append(jnp.concatenate([state_pool[l][:, 1:], xps[:, None, :]], axis=1))

    st = lambda name: jnp.stack(outs[name], axis=0)
    return (yp.reshape(batch, seq, d), ys.reshape(db, sd, d), st("kp"), st("vp"), st("pp"),
            st("ks"), st("vs"), st("ps"))
```
